```python
import math
import jax
import jax.numpy as jnp
from jax import lax
import numpy as np

D_MODEL = 2048
BATCH = 4
SEQ = 2048
DEPTH = 2

GRID_W = 64
CTX_LEN = 256
MIX_WIDTH = D_MODEL
GROUP_WIDTH = MIX_WIDTH // 4
HEAD_DIM = 64
ROPE_BASE = 10000.0
EPS = 1e-6
NEG_INF = -1e30

POOL_WINDOWS = (2, 4, 8, 16)
POOL_GROUPS = 4
POOL_GROUP_DIM = GROUP_WIDTH // POOL_GROUPS

DIFF_HEADS = GROUP_WIDTH // (2 * HEAD_DIM)
Q_BLOCK = 128

SGU_CHUNK = 128
SGU_GROUPS = 4
SGU_GROUP_DIM = GROUP_WIDTH // SGU_GROUPS

WIN_HEADS = GROUP_WIDTH // HEAD_DIM
WIN_KV_HEADS = 2
WIN_REP = WIN_HEADS // WIN_KV_HEADS
WINDOW = 128
WIN_BLOCK = 128

IN_SIZES = (GROUP_WIDTH, GROUP_WIDTH, GROUP_WIDTH, GROUP_WIDTH, 2 * GROUP_WIDTH,
            WIN_HEADS * HEAD_DIM, WIN_KV_HEADS * HEAD_DIM, WIN_KV_HEADS * HEAD_DIM)
IN_COLS = 6 * GROUP_WIDTH + WIN_HEADS * HEAD_DIM + 2 * WIN_KV_HEADS * HEAD_DIM

D_FF = 5632
N_EXPERTS = 8
TOP_K = 2
N_DENSE = (DEPTH + 1) // 2
N_MOE = DEPTH // 2

kernel_name = "hybrid_diffusion_parallel_mixers_ctx_prefix"


def _rmsnorm(x, g):
    xf = x.astype(jnp.float32)
    y = xf * lax.rsqrt(jnp.mean(xf * xf, axis=-1, keepdims=True) + EPS)
    return (y * g.astype(jnp.float32)).astype(x.dtype)


def _layernorm(x, g, b):
    xf = x.astype(jnp.float32)
    mu = jnp.mean(xf, axis=-1, keepdims=True)
    var = jnp.mean(jnp.square(xf - mu), axis=-1, keepdims=True)
    y = (xf - mu) * lax.rsqrt(var + EPS) * g.astype(jnp.float32) + b.astype(jnp.float32)
    return y.astype(x.dtype)


def _modulate(x, g, shift, scale):
    return _rmsnorm(x, g) * (1 + scale) + shift


def _axial_rope(length, dim):
    n_rows = length // GRID_W
    rows = jnp.repeat(jnp.arange(n_rows, dtype=jnp.float32), GRID_W)
    cols = jnp.tile(jnp.arange(GRID_W, dtype=jnp.float32), n_rows)
    n_freq = dim // 4
    inv_freq = ROPE_BASE ** (-jnp.arange(n_freq, dtype=jnp.float32) / n_freq)
    ang = jnp.concatenate([rows[:, None] * inv_freq, cols[:, None] * inv_freq], axis=-1)
    return jnp.cos(ang), jnp.sin(ang)


def _apply_rope(x, cos, sin):
    bshape = (cos.shape[0],) + (1,) * (x.ndim - 3) + (cos.shape[1],)
    cos = cos.reshape(bshape)
    sin = sin.reshape(bshape)
    x1, x2 = jnp.split(x.astype(jnp.float32), 2, axis=-1)
    return jnp.concatenate([x1 * cos - x2 * sin, x1 * sin + x2 * cos], axis=-1).astype(x.dtype)


def _split_in(p):
    outs, start = [], 0
    for size in IN_SIZES:
        outs.append(p[..., start:start + size])
        start += size
    return outs


def _pool_mix(a, w, scale):
    B, T, _ = a.shape
    af = a.astype(jnp.float32)
    csum = jnp.concatenate([jnp.zeros_like(af[:, :1]), jnp.cumsum(af, axis=1)], axis=1)
    t = jnp.arange(T)
    pooled = []
    for g, win in enumerate(POOL_WINDOWS):
        lo = jnp.maximum(t - win // 2, 0)
        hi = jnp.minimum(t + win // 2, T)
        cs = csum[..., g * POOL_GROUP_DIM:(g + 1) * POOL_GROUP_DIM]
        pooled.append((cs[:, hi] - cs[:, lo]) / (hi - lo).astype(jnp.float32)[:, None])
    d = (jnp.concatenate(pooled, axis=-1) - af).astype(a.dtype)
    d = d.reshape(B, T, POOL_GROUPS, POOL_GROUP_DIM)
    y = jnp.einsum('btgc,gcd->btgd', d, w).reshape(B, T, GROUP_WIDTH)
    return y * scale


def _diff_mix(q, k, v, lam):
    s = jnp.einsum('bqhmd,bkhmd->bhmqk', q, k).astype(jnp.float32) * HEAD_DIM ** -0.5
    p = jax.nn.softmax(s, axis=-1)
    a = p[:, :, 0] - lam * p[:, :, 1]
    return jnp.einsum('bhqk,bkhd->bqhd', a.astype(v.dtype), v)


def _diff_out(o, g, lam_init):
    B, T = o.shape[:2]
    return (_rmsnorm(o, g) * (1.0 - lam_init)).reshape(B, T, GROUP_WIDTH)


def _sgu_mix(z, g, b, w_s, b_s):
    B, T, _ = z.shape
    u, v = jnp.split(jax.nn.gelu(z), 2, axis=-1)
    v = _layernorm(v, g, b).reshape(B, T // SGU_CHUNK, SGU_CHUNK, SGU_GROUPS, SGU_GROUP_DIM)
    v = jnp.einsum('gpq,bnqgc->bnpgc', w_s, v) + b_s.T[:, :, None]
    return u * v.reshape(B, T, GROUP_WIDTH)


def _window_attend(q, k, v, k_ctx, v_ctx, sink):
    B, T, G, R, dh = q.shape
    nb = T // WIN_BLOCK
    band = 3 * WIN_BLOCK

    def to_band(a):
        ap = jnp.pad(a, ((0, 0), (WIN_BLOCK, WIN_BLOCK), (0, 0), (0, 0)))
        ap = ap.reshape(B, nb + 2, WIN_BLOCK, G, dh)
        return jnp.concatenate([ap[:, :-2], ap[:, 1:-1], ap[:, 2:]], axis=2)

    k_band, v_band = to_band(k), to_band(v)
    qb = q.reshape(B, nb, WIN_BLOCK, G, R, dh)
    scale = dh ** -0.5
    s_band = jnp.einsum('bnqgrd,bnkgd->bngrqk', qb, k_band).astype(jnp.float32) * scale
    q_off = jnp.arange(WIN_BLOCK)[:, None]
    k_off = jnp.arange(band)[None, :] - WIN_BLOCK
    k_pos = jnp.arange(nb)[:, None, None] * WIN_BLOCK + k_off[None]
    mask = (jnp.abs(k_off - q_off) <= WINDOW)[None] & (k_pos >= 0) & (k_pos < T)
    s_band = jnp.where(mask[None, :, None, None], s_band, NEG_INF)
    s_ctx = jnp.einsum('bnqgrd,bcgd->bngrqc', qb, k_ctx).astype(jnp.float32) * scale
    s_sink = jnp.broadcast_to(sink.astype(jnp.float32).reshape(1, 1, G, R, 1, 1),
                              s_band.shape[:-1] + (1,))
    p = jax.nn.softmax(jnp.concatenate([s_band, s_ctx, s_sink], axis=-1), axis=-1).astype(v.dtype)
    o = (jnp.einsum('bngrqk,bnkgd->bnqgrd', p[..., :band], v_band)
         + jnp.einsum('bngrqc,bcgd->bnqgrd', p[..., band:-1], v_ctx))
    return o.reshape(B, T, G * R * dh)


def _sink_attend(q, k, v, sink):
    B, T, G, R, dh = q.shape
    s = jnp.einsum('bqgrd,bkgd->bgrqk', q, k).astype(jnp.float32) * dh ** -0.5
    s_sink = jnp.broadcast_to(sink.astype(jnp.float32).reshape(1, G, R, 1, 1), s.shape[:-1] + (1,))
    p = jax.nn.softmax(jnp.concatenate([s, s_sink], axis=-1), axis=-1)[..., :-1].astype(v.dtype)
    return jnp.einsum('bgrqk,bkgd->bqgrd', p, v).reshape(B, T, G * R * dh)


def _swiglu(h, w1, w3, w2):
    return (jax.nn.silu(h @ w1) * (h @ w3)) @ w2


def _moe_swiglu(h, router, w1, w3, w2):
    logits = (h @ router).astype(jnp.float32)
    top_v, top_i = lax.top_k(logits, TOP_K)
    top_w = jax.nn.softmax(top_v, axis=-1)
    gates = jnp.sum(jax.nn.one_hot(top_i, N_EXPERTS, dtype=jnp.float32) * top_w[..., None], axis=-2)
    y = jnp.zeros_like(h)
    for e in range(N_EXPERTS):
        y = y + gates[..., e:e + 1].astype(h.dtype) * _swiglu(h, w1[e], w3[e], w2[e])
    return y


def setup_inputs(seed: int = 0) -> dict:
    key = jax.random.key(seed)
    keys = iter(jax.random.split(key, 32))
    D = D_MODEL

    def nrm(shape, s):
        return jax.random.normal(next(keys), shape, jnp.float32) * s

    return {
        'x': nrm((BATCH, SEQ, D), 1.0),
        'c': nrm((BATCH, D), 1.0),
        'ctx': nrm((BATCH, CTX_LEN, D), 1.0),
        'c_ctx': nrm((D,), 1.0),
        'w_mod': nrm((DEPTH, D, 6 * D), 0.5 * D ** -0.5),
        'b_mod': nrm((DEPTH, 6 * D), 0.02),
        'norm_mix': 1.0 + nrm((DEPTH, D), 0.05),
        'norm_ffn': 1.0 + nrm((DEPTH, D), 0.05),
        'w_in': nrm((DEPTH, D, IN_COLS), D ** -0.5),
        'w_out': nrm((DEPTH, MIX_WIDTH, D), MIX_WIDTH ** -0.5),
        'pool_w': nrm((DEPTH, POOL_GROUPS, POOL_GROUP_DIM, POOL_GROUP_DIM), POOL_GROUP_DIM ** -0.5),
        'pool_scale': 1.0 + nrm((DEPTH, GROUP_WIDTH), 0.1),
        'diff_lam': nrm((DEPTH, 4, HEAD_DIM), 0.1),
        'diff_subln': 1.0 + nrm((DEPTH, 2 * HEAD_DIM), 0.05),
        'sgu_ln_g': 1.0 + nrm((DEPTH, GROUP_WIDTH), 0.05),
        'sgu_ln_b': nrm((DEPTH, GROUP_WIDTH), 0.02),
        'sgu_w': nrm((DEPTH, SGU_GROUPS, SGU_CHUNK, SGU_CHUNK), SGU_CHUNK ** -0.5),
        'sgu_b': 1.0 + nrm((DEPTH, SGU_GROUPS, SGU_CHUNK), 0.1),
        'win_sink': nrm((DEPTH, WIN_HEADS), 0.5),
        'ffn_w1': nrm((N_DENSE, D, D_FF), D ** -0.5),
        'ffn_w3': nrm((N_DENSE, D, D_FF), D ** -0.5),
        'ffn_w2': nrm((N_DENSE, D_FF, D), D_FF ** -0.5),
        'moe_router': nrm((N_MOE, D, N_EXPERTS), D ** -0.5),
        'moe_w1': nrm((N_MOE, N_EXPERTS, D, D_FF), D ** -0.5),
        'moe_w3': nrm((N_MOE, N_EXPERTS, D, D_FF), D ** -0.5),
        'moe_w2': nrm((N_MOE, N_EXPERTS, D_FF, D), D_FF ** -0.5),
        'norm_final': 1.0 + nrm((D,), 0.05),
    }


def reference(x, c, ctx, c_ctx, w_mod, b_mod, norm_mix, norm_ffn, w_in, w_out,
              pool_w, pool_scale, diff_lam, diff_subln, sgu_ln_g, sgu_ln_b, sgu_w, sgu_b,
              win_sink, ffn_w1, ffn_w3, ffn_w2, moe_router, moe_w1, moe_w3, moe_w2,
              norm_final):
    B, L, _ = x.shape
    C = ctx.shape[1]
    dh = HEAD_DIM
    cos, sin = _axial_rope(L, dh)
    s_lat = jax.nn.silu(c)
    s_ctx = jax.nn.silu(c_ctx)[None]
    h_lat, h_ctx = x, ctx
    for l in range(DEPTH):
        last = l == DEPTH - 1
        m_lat = jnp.split((s_lat @ w_mod[l] + b_mod[l])[:, None, :], 6, axis=-1)
        m_ctx = jnp.split((s_ctx @ w_mod[l] + b_mod[l])[:, None, :], 6, axis=-1)

        n_lat = _modulate(h_lat, norm_mix[l], m_lat[0], m_lat[1])
        n_ctx = _modulate(h_ctx, norm_mix[l], m_ctx[0], m_ctx[1])
        p_all = jnp.concatenate([n_ctx, n_lat], axis=1) @ w_in[l]
        pa_c, dq_c, dk_c, dv_c, z_c, wq_c, wk_c, wv_c = _split_in(p_all[:, :C])
        pa_l, dq_l, dk_l, dv_l, z_l, wq_l, wk_l, wv_l = _split_in(p_all[:, C:])

        lam_init = 0.8 - 0.6 * math.exp(-0.3 * l)
        lq1, lk1, lq2, lk2 = diff_lam[l].astype(jnp.float32)
        lam = jnp.exp(jnp.sum(lq1 * lk1)) - jnp.exp(jnp.sum(lq2 * lk2)) + lam_init
        dk_c = dk_c.reshape(B, C, DIFF_HEADS, 2, dh)
        dv_c = dv_c.reshape(B, C, DIFF_HEADS, 2 * dh)
        dq_l = _apply_rope(dq_l.reshape(B, L, DIFF_HEADS, 2, dh), cos, sin)
        dk_l = _apply_rope(dk_l.reshape(B, L, DIFF_HEADS, 2, dh), cos, sin)
        k_all = jnp.concatenate([dk_l, dk_c], axis=1)
        v_all = jnp.concatenate([dv_l.reshape(B, L, DIFF_HEADS, 2 * dh), dv_c], axis=1)
        q_blocks = dq_l.reshape(B, L // Q_BLOCK, Q_BLOCK, DIFF_HEADS, 2, dh).swapaxes(0, 1)
        o_blocks = lax.map(lambda qb: _diff_mix(qb, k_all, v_all, lam), q_blocks)
        diff_l = _diff_out(o_blocks.swapaxes(0, 1).reshape(B, L, DIFF_HEADS, 2 * dh),
                           diff_subln[l], lam_init)

        wk_c = wk_c.reshape(B, C, WIN_KV_HEADS, dh)
        wv_c = wv_c.reshape(B, C, WIN_KV_HEADS, dh)
        wq_l = _apply_rope(wq_l.reshape(B, L, WIN_KV_HEADS, WIN_REP, dh), cos, sin)
        wk_l = _apply_rope(wk_l.reshape(B, L, WIN_KV_HEADS, dh), cos, sin)
        win_l = _window_attend(wq_l, wk_l, wv_l.reshape(B, L, WIN_KV_HEADS, dh),
                               wk_c, wv_c, win_sink[l])

        mix_l = jnp.concatenate([
            _pool_mix(pa_l, pool_w[l], pool_scale[l]),
            diff_l,
            _sgu_mix(z_l, sgu_ln_g[l], sgu_ln_b[l], sgu_w[l], sgu_b[l]),
            win_l], axis=-1)
        h_lat = h_lat + m_lat[2] * (mix_l @ w_out[l])
        if not last:
            diff_c = _diff_out(_diff_mix(dq_c.reshape(B, C, DIFF_HEADS, 2, dh), dk_c, dv_c, lam),
                               diff_subln[l], lam_init)
            win_c = _sink_attend(wq_c.reshape(B, C, WIN_KV_HEADS, WIN_REP, dh), wk_c, wv_c, win_sink[l])
            mix_c = jnp.concatenate([
                _pool_mix(pa_c, pool_w[l], pool_scale[l]),
                diff_c,
                _sgu_mix(z_c, sgu_ln_g[l], sgu_ln_b[l], sgu_w[l], sgu_b[l]),
                win_c], axis=-1)
            h_ctx = h_ctx + m_ctx[2] * (mix_c @ w_out[l])

        f_lat = _modulate(h_lat, norm_ffn[l], m_lat[3], m_lat[4])
        if last:
            f_in = f_lat
        else:
            f_in = jnp.concatenate([_modulate(h_ctx, norm_ffn[l], m_ctx[3], m_ctx[4]), f_lat], axis=1)
        i = l // 2
        if l % 2 == 0:
            y = _swiglu(f_in, ffn_w1[i], ffn_w3[i], ffn_w2[i])
        else:
            y = _moe_swiglu(f_in, moe_router[i], moe_w1[i], moe_w3[i], moe_w2[i])
        if not last:
            h_ctx = h_ctx + m_ctx[5] * y[:, :C]
        h_lat = h_lat + m_lat[5] * y[:, -L:]
    return _rmsnorm(h_lat, norm_final)
```

```python
import functools
import math

import jax
import jax.numpy as jnp
from jax import lax
from jax.experimental import pallas as pl
from jax.experimental.pallas import tpu as pltpu

F32 = jnp.float32
BF16 = jnp.bfloat16

GRID_W = 64
GROUP_WIDTH = 512
HEAD_DIM = 64
ROPE_BASE = 10000.0
EPS = 1e-6
NEG_INF = -1e30
LANES = 128

POOL_WINDOWS = (2, 4, 8, 16)
POOL_PAD = 16
DIFF_HEADS = GROUP_WIDTH // (2 * HEAD_DIM)
SGU_CHUNK = 128
SGU_GROUPS = 4
WIN_HEADS = GROUP_WIDTH // HEAD_DIM
WIN_KV_HEADS = 2
WIN_REP = WIN_HEADS // WIN_KV_HEADS
WINDOW = 128
WIN_BLOCK = 128
N_EXPERTS = 8

COL_POOL = 0
COL_DQ = GROUP_WIDTH
COL_DK = 2 * GROUP_WIDTH
COL_DV = 3 * GROUP_WIDTH
COL_Z = 4 * GROUP_WIDTH
COL_WQ = 6 * GROUP_WIDTH
COL_WK = COL_WQ + WIN_HEADS * HEAD_DIM
COL_WV = COL_WK + WIN_KV_HEADS * HEAD_DIM
IN_COLS = COL_WV + WIN_KV_HEADS * HEAD_DIM
ROPE_RANGES = ((COL_DQ, COL_DV), (COL_WQ, COL_WV))

VMEM_LIMIT = 56 * 1024 * 1024

NT_DIMS = (((1,), (1,)), ((), ()))


def _params(sem, vmem=VMEM_LIMIT):
    return pltpu.CompilerParams(dimension_semantics=sem, vmem_limit_bytes=vmem)


def _sigmoid(x):
    return 1.0 / (1.0 + jnp.exp(-x))


def _row_is_ctx(tile_idx, tm, n_ctx):
    rows = tile_idx * tm + lax.broadcasted_iota(jnp.int32, (tm, 1), 0)
    return rows < n_ctx


def _mod_row(m_ref, k, is_ctx):
    return jnp.where(is_ctx, m_ref[0, k:k + 1, :], m_ref[1, k:k + 1, :])


def _norm_modulate(x, g, m_ref, k_shift, is_ctx):
    ms = jnp.mean(x * x, axis=-1, keepdims=True)
    y = x * lax.rsqrt(ms + EPS) * g
    return y * (1.0 + _mod_row(m_ref, k_shift + 1, is_ctx)) + _mod_row(m_ref, k_shift, is_ctx)


def _adaln_kernel(c_ref, w_ref, b_ref, o_ref):
    cv = c_ref[...]
    s = (cv * _sigmoid(cv)).astype(BF16)
    o_ref[0] = jnp.dot(s, w_ref[0].astype(BF16), preferred_element_type=F32) + b_ref[0]


def _adaln(c_rows, w_mod, b_mod, tn=1024):
    depth, d, n = w_mod.shape
    r = c_rows.shape[0]
    return pl.pallas_call(
        _adaln_kernel,
        grid=(depth, n // tn),
        in_specs=[pl.BlockSpec((r, d), lambda l, j: (0, 0)),
                  pl.BlockSpec((1, d, tn), lambda l, j: (l, 0, j)),
                  pl.BlockSpec((1, 1, tn), lambda l, j: (l, 0, j))],
        out_specs=pl.BlockSpec((1, r, tn), lambda l, j: (l, 0, j)),
        out_shape=jax.ShapeDtypeStruct((depth, r, n), F32),
        compiler_params=_params(("arbitrary", "arbitrary")),
        name="adaln",
    )(c_rows, w_mod, b_mod.reshape(depth, 1, n))


def _is_rope_col(col):
    return any(lo <= col < hi for lo, hi in ROPE_RANGES)


def _inproj_kernel(x_ref, m_ref, g_ref, w_ref, cos_ref, sin_ref, o_ref, xn_ref, *, tm, tn, n_ctx):
    ti = pl.program_id(1)
    j = pl.program_id(2)

    @pl.when(j == 0)
    def _():
        is_ctx = _row_is_ctx(ti, tm, n_ctx)
        xn_ref[...] = _norm_modulate(x_ref[0], g_ref[...], m_ref, 0, is_ctx).astype(BF16)

    acc = jnp.dot(xn_ref[...], w_ref[...].astype(BF16), preferred_element_type=F32)
    lane = lax.broadcasted_iota(jnp.int32, (1, LANES), 1)
    first_half = (lane % HEAD_DIM) < (HEAD_DIM // 2)

    for jj in range(IN_COLS // tn):
        @pl.when(j == jj)
        def _(jj=jj):
            for cc in range(tn // LANES):
                a = acc[:, cc * LANES:(cc + 1) * LANES]
                if _is_rope_col(jj * tn + cc * LANES):
                    partner = jnp.where(first_half,
                                        pltpu.roll(a, LANES - HEAD_DIM // 2, 1),
                                        pltpu.roll(a, HEAD_DIM // 2, 1))
                    a = a * cos_ref[...] + partner * sin_ref[...]
                o_ref[0, :, cc * LANES:(cc + 1) * LANES] = a


def _inproj(h, modsel, g, w, cos_t, sin_t, n_ctx, tm=768, tn=768):
    b, t, d = h.shape
    kern = functools.partial(_inproj_kernel, tm=tm, tn=tn, n_ctx=n_ctx)
    return pl.pallas_call(
        kern,
        grid=(b, t // tm, IN_COLS // tn),
        in_specs=[pl.BlockSpec((1, tm, d), lambda bi, ti, j: (bi, ti, 0)),
                  pl.BlockSpec((None, 2, 6, d), lambda bi, ti, j: (bi, 0, 0, 0)),
                  pl.BlockSpec((1, d), lambda bi, ti, j: (0, 0)),
                  pl.BlockSpec((d, tn), lambda bi, ti, j: (0, j)),
                  pl.BlockSpec((tm, LANES), lambda bi, ti, j: (ti, 0)),
                  pl.BlockSpec((tm, LANES), lambda bi, ti, j: (ti, 0))],
        out_specs=pl.BlockSpec((1, tm, tn), lambda bi, ti, j: (bi, ti, j)),
        out_shape=jax.ShapeDtypeStruct((b, t, IN_COLS), F32),
        scratch_shapes=[pltpu.VMEM((tm, d), BF16)],
        compiler_params=_params(("arbitrary", "arbitrary", "arbitrary")),
        name="inproj",
    )(h, modsel, g.reshape(1, d), w, cos_t, sin_t)


def _rope_tables(n_ctx, n_lat):
    n_rows = n_lat // GRID_W
    rows = jnp.repeat(jnp.arange(n_rows, dtype=F32), GRID_W)
    cols = jnp.tile(jnp.arange(GRID_W, dtype=F32), n_rows)
    n_freq = HEAD_DIM // 4
    inv_freq = ROPE_BASE ** (-jnp.arange(n_freq, dtype=F32) / n_freq)
    ang = jnp.concatenate([rows[:, None] * inv_freq, cols[:, None] * inv_freq], axis=-1)
    cos, sin = jnp.cos(ang), jnp.sin(ang)
    cos_h = jnp.concatenate([cos, cos], axis=-1)
    sin_h = jnp.concatenate([-sin, sin], axis=-1)
    cos_t = jnp.concatenate([jnp.ones((n_ctx, HEAD_DIM), F32), cos_h], axis=0)
    sin_t = jnp.concatenate([jnp.zeros((n_ctx, HEAD_DIM), F32), sin_h], axis=0)
    rep = LANES // HEAD_DIM
    return jnp.tile(cos_t, (1, rep)), jnp.tile(sin_t, (1, rep))


def _pool_kernel(a_ref, w_ref, s_ref, o_ref, pad_ref, *, segments):
    for g, win in enumerate(POOL_WINDOWS):
        half = win // 2
        cs = slice(g * LANES, (g + 1) * LANES)
        wg = w_ref[g].astype(BF16)
        for s0, n in segments:
            a = a_ref[0, s0:s0 + n, cs]
            pad_ref[0:POOL_PAD, :] = jnp.zeros((POOL_PAD, LANES), F32)
            pad_ref[POOL_PAD:POOL_PAD + n, :] = a
            pad_ref[POOL_PAD + n:2 * POOL_PAD + n, :] = jnp.zeros((POOL_PAD, LANES), F32)
            tot = pad_ref[POOL_PAD - half:POOL_PAD - half + n, :]
            for k in range(-half + 1, half):
                tot = tot + pad_ref[POOL_PAD + k:POOL_PAD + k + n, :]
            t = lax.broadcasted_iota(jnp.int32, (n, 1), 0)
            cnt = (jnp.minimum(t + half, n) - jnp.maximum(t - half, 0)).astype(F32)
            dlt = (tot / cnt - a).astype(BF16)
            y = jnp.dot(dlt, wg, preferred_element_type=F32)
            o_ref[0, s0:s0 + n, cs] = y * s_ref[:, cs]


def _pool(p_all, w, scale, segments):
    b, t, _ = p_all.shape
    max_n = max(n for _, n in segments)
    kern = functools.partial(_pool_kernel, segments=segments)
    return pl.pallas_call(
        kern,
        grid=(b,),
        in_specs=[pl.BlockSpec((1, t, GROUP_WIDTH), lambda bi: (bi, 0, COL_POOL // GROUP_WIDTH)),
                  pl.BlockSpec(w.shape, lambda bi: (0, 0, 0)),
                  pl.BlockSpec((1, GROUP_WIDTH), lambda bi: (0, 0))],
        out_specs=pl.BlockSpec((1, t, GROUP_WIDTH), lambda bi: (bi, 0, 0)),
        out_shape=jax.ShapeDtypeStruct((b, t, GROUP_WIDTH), F32),
        scratch_shapes=[pltpu.VMEM((max_n + 2 * POOL_PAD, LANES), F32)],
        compiler_params=_params(("arbitrary",)),
        name="pool",
    )(p_all, w, scale.reshape(1, GROUP_WIDTH))


def _diff_kernel(lam_ref, q_ref, k_ref, v_ref, g_ref, o_ref, *, tq, n_ctx, out_scale):
    ti = pl.program_id(2)
    lam = lam_ref[0]
    lane = lax.broadcasted_iota(jnp.int32, (1, 2 * HEAD_DIM), 1)
    q = q_ref[0] * (HEAD_DIM ** -0.5)
    q1 = jnp.where(lane < HEAD_DIM, q, 0.0).astype(BF16)
    q2 = jnp.where(lane >= HEAD_DIM, q, 0.0).astype(BF16)

    def attend(n_keys):
        k = k_ref[0, 0:n_keys, :].astype(BF16)
        v = v_ref[0, 0:n_keys, :].astype(BF16)
        s1 = lax.dot_general(q1, k, NT_DIMS, preferred_element_type=F32)
        s2 = lax.dot_general(q2, k, NT_DIMS, preferred_element_type=F32)
        e1 = jnp.exp(s1 - jnp.max(s1, axis=-1, keepdims=True))
        e2 = jnp.exp(s2 - jnp.max(s2, axis=-1, keepdims=True))
        r1 = 1.0 / jnp.sum(e1, axis=-1, keepdims=True)
        r2 = lam / jnp.sum(e2, axis=-1, keepdims=True)
        a = (e1 * r1 - e2 * r2).astype(BF16)
        o = jnp.dot(a, v, preferred_element_type=F32)
        ms = jnp.mean(o * o, axis=-1, keepdims=True)
        o_ref[0] = o * lax.rsqrt(ms + EPS) * g_ref[...] * out_scale

    @pl.when(ti * tq < n_ctx)
    def _():
        attend(n_ctx)

    @pl.when(ti * tq >= n_ctx)
    def _():
        attend(k_ref.shape[1])


def _diff(p_all, lam, subln, n_ctx, lam_init, tq=256):
    b, t, _ = p_all.shape
    assert n_ctx % tq == 0 and t % tq == 0
    w = 2 * HEAD_DIM
    kern = functools.partial(_diff_kernel, tq=tq, n_ctx=n_ctx, out_scale=1.0 - lam_init)
    return pl.pallas_call(
        kern,
        grid=(b, DIFF_HEADS, t // tq),
        in_specs=[pl.BlockSpec(memory_space=pltpu.SMEM),
                  pl.BlockSpec((1, tq, w), lambda bi, hi, ti: (bi, ti, COL_DQ // w + hi)),
                  pl.BlockSpec((1, t, w), lambda bi, hi, ti: (bi, 0, COL_DK // w + hi)),
                  pl.BlockSpec((1, t, w), lambda bi, hi, ti: (bi, 0, COL_DV // w + hi)),
                  pl.BlockSpec((1, w), lambda bi, hi, ti: (0, 0))],
        out_specs=pl.BlockSpec((1, tq, w), lambda bi, hi, ti: (bi, ti, hi)),
        out_shape=jax.ShapeDtypeStruct((b, t, GROUP_WIDTH), F32),
        compiler_params=_params(("arbitrary", "arbitrary", "arbitrary")),
        name="diff_attn",
    )(lam.reshape(1), p_all, p_all, p_all, subln.reshape(1, w))


def _gelu_tanh(x):
    return 0.5 * x * (1.0 + jnp.tanh(math.sqrt(2.0 / math.pi) * (x + 0.044715 * (x * x * x))))


def _sgu_kernel(z_ref, g_ref, b_ref, w_ref, bs_ref, o_ref, *, tm):
    z = _gelu_tanh(z_ref[0])
    u = z[:, :GROUP_WIDTH]
    v = z[:, GROUP_WIDTH:]
    mu = jnp.mean(v, axis=-1, keepdims=True)
    var = jnp.mean(jnp.square(v - mu), axis=-1, keepdims=True)
    vn = ((v - mu) * lax.rsqrt(var + EPS) * g_ref[...] + b_ref[...]).astype(BF16)
    for g in range(SGU_GROUPS):
        wg = w_ref[g].astype(BF16)
        bias = bs_ref[:, g:g + 1]
        cs = slice(g * LANES, (g + 1) * LANES)
        for c in range(tm // SGU_CHUNK):
            rs = slice(c * SGU_CHUNK, (c + 1) * SGU_CHUNK)
            sv = jnp.dot(wg, vn[rs, cs], preferred_element_type=F32) + bias
            o_ref[0, rs, cs] = u[rs, cs] * sv


def _sgu(p_all, ln_g, ln_b, w_s, b_s, tm=768):
    b, t, _ = p_all.shape
    zw = 2 * GROUP_WIDTH
    kern = functools.partial(_sgu_kernel, tm=tm)
    return pl.pallas_call(
        kern,
        grid=(b, t // tm),
        in_specs=[pl.BlockSpec((1, tm, zw), lambda bi, ti: (bi, ti, COL_Z // zw)),
                  pl.BlockSpec((1, GROUP_WIDTH), lambda bi, ti: (0, 0)),
                  pl.BlockSpec((1, GROUP_WIDTH), lambda bi, ti: (0, 0)),
                  pl.BlockSpec(w_s.shape, lambda bi, ti: (0, 0, 0)),
                  pl.BlockSpec((SGU_CHUNK, SGU_GROUPS), lambda bi, ti: (0, 0))],
        out_specs=pl.BlockSpec((1, tm, GROUP_WIDTH), lambda bi, ti: (bi, ti, 0)),
        out_shape=jax.ShapeDtypeStruct((b, t, GROUP_WIDTH), F32),
        compiler_params=_params(("arbitrary", "arbitrary")),
        name="sgu",
    )(p_all, ln_g.reshape(1, GROUP_WIDTH), ln_b.reshape(1, GROUP_WIDTH), w_s, b_s.T)


def _win_kernel(sink_ref, q_ref, k_ref, v_ref, o_ref, kk_ref, vv_ref, *, n_ctx):
    g = pl.program_id(1)
    n = pl.program_id(2)
    t = k_ref.shape[1]
    blk = WIN_BLOCK
    band = 3 * blk
    lane = lax.broadcasted_iota(jnp.int32, (1, LANES), 1)
    low = lane < HEAD_DIM

    @pl.when(n == 0)
    def _():
        mine = (lane >= g * HEAD_DIM) & (lane < (g + 1) * HEAD_DIM)
        km = jnp.where(mine, k_ref[0], 0.0)
        vm = jnp.where(mine, v_ref[0], 0.0)
        kk_ref[...] = (km + pltpu.roll(km, HEAD_DIM, 1)).astype(BF16)
        vv_ref[...] = (vm + pltpu.roll(vm, HEAD_DIM, 1)).astype(BF16)

    start = pl.multiple_of(jnp.clip((n - 1) * blk, 0, t - band), blk)
    k_ctx = kk_ref[0:n_ctx, :]
    v_ctx = vv_ref[0:n_ctx, :]
    k_band = kk_ref[pl.ds(start, band), :]
    v_band = vv_ref[pl.ds(start, band), :]
    qpos = n * blk + lax.broadcasted_iota(jnp.int32, (blk, 1), 0)
    kpos = start + lax.broadcasted_iota(jnp.int32, (1, band), 1)
    valid = (kpos >= n_ctx) & (qpos >= n_ctx) & (jnp.abs(kpos - qpos) <= WINDOW)

    q = q_ref[0] * (HEAD_DIM ** -0.5)
    for pair in range(WIN_REP // 2):
        qp = q[:, pair * LANES:(pair + 1) * LANES]
        halves = []
        for hf in range(2):
            qm = jnp.where(low if hf == 0 else jnp.logical_not(low), qp, 0.0).astype(BF16)
            s_c = lax.dot_general(qm, k_ctx, NT_DIMS, preferred_element_type=F32)
            s_b = lax.dot_general(qm, k_band, NT_DIMS, preferred_element_type=F32)
            s_b = jnp.where(valid, s_b, NEG_INF)
            sink = sink_ref[g * WIN_REP + pair * 2 + hf]
            m = jnp.maximum(jnp.maximum(jnp.max(s_c, axis=-1, keepdims=True),
                                        jnp.max(s_b, axis=-1, keepdims=True)), sink)
            e_c = jnp.exp(s_c - m)
            e_b = jnp.exp(s_b - m)
            den = (jnp.sum(e_c, axis=-1, keepdims=True) + jnp.sum(e_b, axis=-1, keepdims=True)
                   + jnp.exp(sink - m))
            o = (jnp.dot(e_c.astype(BF16), v_ctx, preferred_element_type=F32)
                 + jnp.dot(e_b.astype(BF16), v_band, preferred_element_type=F32))
            halves.append(o / den)
        o_ref[0, :, pair * LANES:(pair + 1) * LANES] = jnp.where(low, halves[0], halves[1])


def _win(p_all, sink, n_ctx):
    b, t, _ = p_all.shape
    qw = WIN_REP * HEAD_DIM
    kern = functools.partial(_win_kernel, n_ctx=n_ctx)
    return pl.pallas_call(
        kern,
        grid=(b, WIN_KV_HEADS, t // WIN_BLOCK),
        in_specs=[pl.BlockSpec(memory_space=pltpu.SMEM),
                  pl.BlockSpec((1, WIN_BLOCK, qw), lambda bi, gi, ni: (bi, ni, COL_WQ // qw + gi)),
                  pl.BlockSpec((1, t, LANES), lambda bi, gi, ni: (bi, 0, COL_WK // LANES)),
                  pl.BlockSpec((1, t, LANES), lambda bi, gi, ni: (bi, 0, COL_WV // LANES))],
        out_specs=pl.BlockSpec((1, WIN_BLOCK, qw), lambda bi, gi, ni: (bi, ni, gi)),
        out_shape=jax.ShapeDtypeStruct((b, t, GROUP_WIDTH), F32),
        scratch_shapes=[pltpu.VMEM((t, LANES), BF16), pltpu.VMEM((t, LANES), BF16)],
        compiler_params=_params(("arbitrary", "arbitrary", "arbitrary")),
        name="win_attn",
    )(sink, p_all, p_all, p_all)


def _outproj_kernel(a_ref, b_ref, c_ref, d_ref, w_ref, h_ref, m_ref, o_ref, mix_ref, *, tm, n_ctx):
    ti = pl.program_id(1)

    @pl.when(pl.program_id(2) == 0)
    def _():
        for i, r in enumerate((a_ref, b_ref, c_ref, d_ref)):
            mix_ref[:, i * GROUP_WIDTH:(i + 1) * GROUP_WIDTH] = r[0].astype(BF16)

    y = jnp.dot(mix_ref[...], w_ref[...].astype(BF16), preferred_element_type=F32)
    gate = _mod_row(m_ref, 2, _row_is_ctx(ti, tm, n_ctx))
    o_ref[0] = h_ref[0] + gate * y


def _outproj(mixes, w, h, modsel, n_ctx, tm=768, tn=512):
    b, t, d = h.shape
    kern = functools.partial(_outproj_kernel, tm=tm, n_ctx=n_ctx)
    mix_spec = pl.BlockSpec((1, tm, GROUP_WIDTH), lambda bi, ti, j: (bi, ti, 0))
    return pl.pallas_call(
        kern,
        grid=(b, t // tm, d // tn),
        in_specs=[mix_spec, mix_spec, mix_spec, mix_spec,
                  pl.BlockSpec((4 * GROUP_WIDTH, tn), lambda bi, ti, j: (0, j)),
                  pl.BlockSpec((1, tm, tn), lambda bi, ti, j: (bi, ti, j)),
                  pl.BlockSpec((None, 2, 6, tn), lambda bi, ti, j: (bi, 0, 0, j))],
        out_specs=pl.BlockSpec((1, tm, tn), lambda bi, ti, j: (bi, ti, j)),
        out_shape=jax.ShapeDtypeStruct((b, t, d), F32),
        scratch_shapes=[pltpu.VMEM((tm, 4 * GROUP_WIDTH), BF16)],
        compiler_params=_params(("arbitrary", "arbitrary", "arbitrary")),
        name="outproj",
    )(*mixes, w, h, modsel)


def _ffn_in_kernel(x_ref, m_ref, g_ref, o_ref, *, tm, n_ctx, row0):
    is_ctx = _row_is_ctx(pl.program_id(1), tm, n_ctx - row0)
    o_ref[0] = _norm_modulate(x_ref[0], g_ref[...], m_ref, 3, is_ctx).astype(BF16)


def _route_kernel(x_ref, m_ref, g_ref, r_ref, o_ref, gate_ref, sel_ref, *, tm, n_ctx, row0):
    is_ctx = _row_is_ctx(pl.program_id(1), tm, n_ctx - row0)
    f = _norm_modulate(x_ref[0], g_ref[...], m_ref, 3, is_ctx)
    o_ref[0] = f.astype(BF16)
    logits = jnp.dot(f, r_ref[...], preferred_element_type=F32, precision=lax.Precision.HIGHEST)
    lane = lax.broadcasted_iota(jnp.int32, (1, LANES), 1)
    logits = jnp.where(lane < N_EXPERTS, logits, -jnp.inf)
    m1 = jnp.max(logits, axis=-1, keepdims=True)
    i1 = jnp.min(jnp.where(logits == m1, lane, LANES), axis=-1, keepdims=True)
    rest = jnp.where(lane == i1, -jnp.inf, logits)
    m2 = jnp.max(rest, axis=-1, keepdims=True)
    i2 = jnp.min(jnp.where(rest == m2, lane, LANES), axis=-1, keepdims=True)
    e2 = jnp.exp(m2 - m1)
    w1 = 1.0 / (1.0 + e2)
    w2 = e2 / (1.0 + e2)
    gate_ref[0] = jnp.where(lane == i1, w1, 0.0) + jnp.where(lane == i2, w2, 0.0)
    sel_ref[0] = ((lane == i1) | (lane == i2)).astype(jnp.int32)


def _ffn_in(h, modsel, g, n_ctx, row0, router=None, tm=256):
    b, t, d = h.shape
    nt = (t - row0) // tm
    off = row0 // tm
    in_specs = [pl.BlockSpec((1, tm, d), lambda bi, ti: (bi, ti + off, 0)),
                pl.BlockSpec((None, 2, 6, d), lambda bi, ti: (bi, 0, 0, 0)),
                pl.BlockSpec((1, d), lambda bi, ti: (0, 0))]
    f_spec = pl.BlockSpec((1, tm, d), lambda bi, ti: (bi, ti, 0))
    f_shape = jax.ShapeDtypeStruct((b, t - row0, d), BF16)
    if router is None:
        kern = functools.partial(_ffn_in_kernel, tm=tm, n_ctx=n_ctx, row0=row0)
        return pl.pallas_call(
            kern, grid=(b, nt), in_specs=in_specs, out_specs=f_spec, out_shape=f_shape,
            compiler_params=_params(("arbitrary", "arbitrary")), name="ffn_in",
        )(h, modsel, g.reshape(1, d))
    r_pad = jnp.zeros((d, LANES), F32).at[:, :N_EXPERTS].set(router)
    e_spec = pl.BlockSpec((1, tm, LANES), lambda bi, ti: (bi, ti, 0))
    kern = functools.partial(_route_kernel, tm=tm, n_ctx=n_ctx, row0=row0)
    return pl.pallas_call(
        kern, grid=(b, nt),
        in_specs=in_specs + [pl.BlockSpec((d, LANES), lambda bi, ti: (0, 0))],
        out_specs=[f_spec, e_spec, e_spec],
        out_shape=[f_shape, jax.ShapeDtypeStruct((b, t - row0, LANES), F32),
                   jax.ShapeDtypeStruct((b, t - row0, LANES), jnp.int32)],
        compiler_params=_params(("arbitrary", "arbitrary")), name="ffn_in_route",
    )(h, modsel, g.reshape(1, d), r_pad)


FFN_TM = 1024
FFN_SUB = 256


def _ffn_kernel(te_ref, tr_ref, x_ref, w1_ref, w3_ref, w2_ref, o_ref, b1_ref, b3_ref, b2_ref):
    i = pl.program_id(0)
    j = pl.program_id(1)
    rows = tr_ref[i]

    @pl.when(j == 0)
    def _():
        o_ref[...] = jnp.zeros_like(o_ref)

    @pl.when(rows > 0)
    def _():
        b1_ref[...] = w1_ref[0].astype(BF16)
        b3_ref[...] = w3_ref[0].astype(BF16)
        b2_ref[...] = w2_ref[0].astype(BF16)
        for sb in range(FFN_TM // FFN_SUB):
            @pl.when(sb * FFN_SUB < rows)
            def _(sb=sb):
                rs = slice(sb * FFN_SUB, (sb + 1) * FFN_SUB)
                xs = x_ref[rs, :]
                h1 = jnp.dot(xs, b1_ref[...], preferred_element_type=F32)
                h3 = jnp.dot(xs, b3_ref[...], preferred_element_type=F32)
                act = (h1 * _sigmoid(h1) * h3).astype(BF16)
                o_ref[rs, :] += jnp.dot(act, b2_ref[...], preferred_element_type=F32)


def _ffn(x_rows, w1, w3, w2, tile_expert, tile_rows, tf=256):
    nr, d = x_rows.shape
    n_exp, _, f = w1.shape
    nj = f // tf

    def wcol(i, j, te, tr):
        return (te[i], 0, jnp.where(tr[i] > 0, j, nj - 1))

    def wrow(i, j, te, tr):
        return (te[i], jnp.where(tr[i] > 0, j, nj - 1), 0)

    return pl.pallas_call(
        _ffn_kernel,
        grid_spec=pltpu.PrefetchScalarGridSpec(
            num_scalar_prefetch=2,
            grid=(nr // FFN_TM, nj),
            in_specs=[pl.BlockSpec((FFN_TM, d), lambda i, j, te, tr: (i, 0)),
                      pl.BlockSpec((1, d, tf), wcol),
                      pl.BlockSpec((1, d, tf), wcol),
                      pl.BlockSpec((1, tf, d), wrow)],
            out_specs=pl.BlockSpec((FFN_TM, d), lambda i, j, te, tr: (i, 0)),
            scratch_shapes=[pltpu.VMEM((d, tf), BF16), pltpu.VMEM((d, tf), BF16),
                            pltpu.VMEM((tf, d), BF16)]),
        out_shape=jax.ShapeDtypeStruct((nr, d), F32),
        compiler_params=_params(("arbitrary", "arbitrary")),
        name="ffn",
    )(tile_expert, tile_rows, x_rows, w1, w3, w2)


def _residual_kernel(h_ref, y_ref, m_ref, o_ref, *, tm, n_ctx):
    gate = _mod_row(m_ref, 5, _row_is_ctx(pl.program_id(1), tm, n_ctx))
    o_ref[0] = h_ref[0] + gate * y_ref[0]


def _residual(h, y, modsel, n_ctx, tm=256):
    b, t, d = h.shape
    spec = pl.BlockSpec((1, tm, d), lambda bi, ti: (bi, ti, 0))
    kern = functools.partial(_residual_kernel, tm=tm, n_ctx=n_ctx)
    return pl.pallas_call(
        kern, grid=(b, t // tm),
        in_specs=[spec, spec, pl.BlockSpec((None, 2, 6, d), lambda bi, ti: (bi, 0, 0, 0))],
        out_specs=spec, out_shape=jax.ShapeDtypeStruct((b, t, d), F32),
        compiler_params=_params(("arbitrary", "arbitrary")), name="residual",
    )(h, y, modsel)


def _moe_final_kernel(h_ref, y0_ref, y1_ref, w0_ref, w1_ref, m_ref, g_ref, o_ref):
    y = w0_ref[0] * y0_ref[0] + w1_ref[0] * y1_ref[0]
    hn = h_ref[0] + m_ref[1, 5:6, :] * y
    ms = jnp.mean(hn * hn, axis=-1, keepdims=True)
    o_ref[0] = hn * lax.rsqrt(ms + EPS) * g_ref[...]


def _moe_final(h, y0, y1, w0, w1, modsel, g, row0, tm=256):
    b, t, d = h.shape
    n_lat = t - row0
    off = row0 // tm
    spec = pl.BlockSpec((1, tm, d), lambda bi, ti: (bi, ti, 0))
    wspec = pl.BlockSpec((1, tm, 1), lambda bi, ti: (bi, ti, 0))
    return pl.pallas_call(
        _moe_final_kernel, grid=(b, n_lat // tm),
        in_specs=[pl.BlockSpec((1, tm, d), lambda bi, ti: (bi, ti + off, 0)), spec, spec, wspec, wspec,
                  pl.BlockSpec((None, 2, 6, d), lambda bi, ti: (bi, 0, 0, 0)),
                  pl.BlockSpec((1, d), lambda bi, ti: (0, 0))],
        out_specs=spec, out_shape=jax.ShapeDtypeStruct((b, n_lat, d), F32),
        compiler_params=_params(("arbitrary", "arbitrary")), name="moe_final",
    )(h, y0, y1, w0, w1, modsel, g.reshape(1, d))


def _final_norm_kernel(h_ref, g_ref, o_ref):
    hn = h_ref[0]
    ms = jnp.mean(hn * hn, axis=-1, keepdims=True)
    o_ref[0] = hn * lax.rsqrt(ms + EPS) * g_ref[...]


def _routing_tables(sel, gates, n_tiles):
    n, n_exp = sel.shape
    seli = sel.astype(jnp.int32)
    counts = jnp.sum(seli, axis=0)
    rank = jnp.cumsum(seli, axis=0) - seli
    tiles_e = (counts + FFN_TM - 1) // FFN_TM
    tile_end = jnp.cumsum(tiles_e)
    tile_start = tile_end - tiles_e
    pos = tile_start[None, :] * FFN_TM + rank
    n_rows = n_tiles * FFN_TM
    dest = jnp.where(sel, pos, n_rows).reshape(-1)
    tok = jnp.broadcast_to(jnp.arange(n, dtype=jnp.int32)[:, None], (n, n_exp)).reshape(-1)
    src = jnp.zeros((n_rows,), jnp.int32).at[dest].set(tok, mode="drop")

    e_lo = jnp.argmax(seli, axis=1)
    e_hi = n_exp - 1 - jnp.argmax(seli[:, ::-1], axis=1)
    pick = lambda a, e: jnp.take_along_axis(a, e[:, None], axis=1)[:, 0]
    pos0, pos1 = pick(pos, e_lo), pick(pos, e_hi)
    w0, w1 = pick(gates, e_lo), pick(gates, e_hi)

    tile = jnp.arange(n_tiles, dtype=jnp.int32)
    te = jnp.minimum(jnp.sum((tile_end[None, :] <= tile[:, None]).astype(jnp.int32), axis=1), n_exp - 1)
    tr = jnp.clip(counts[te] - (tile - tile_start[te]) * FFN_TM, 0, FFN_TM)
    tr = jnp.where(tile < tile_end[-1], tr, 0)
    return src, pos0, pos1, w0, w1, te.astype(jnp.int32), tr.astype(jnp.int32)


def kernel(x, c, ctx, c_ctx, w_mod, b_mod, norm_mix, norm_ffn, w_in, w_out, pool_w, pool_scale, diff_lam,
           diff_subln, sgu_ln_g, sgu_ln_b, sgu_w, sgu_b, win_sink, ffn_w1, ffn_w3, ffn_w2, moe_router,
           moe_w1, moe_w3, moe_w2, norm_final):
    b, n_lat, d = x.shape
    n_ctx = ctx.shape[1]
    t = n_ctx + n_lat
    depth = w_mod.shape[0]
    segments = ((0, n_ctx), (n_ctx, n_lat))

    c_rows = jnp.zeros((8, d), F32).at[:b].set(c).at[b].set(c_ctx)
    mod = _adaln(c_rows, w_mod, b_mod).reshape(depth, 8, 6, d)
    cos_t, sin_t = _rope_tables(n_ctx, n_lat)
    h = jnp.concatenate([ctx, x], axis=1)

    out = None
    for l in range(depth):
        last = l == depth - 1
        lat_m = mod[l, :b]
        ctx_m = jnp.broadcast_to(mod[l, b:b + 1], lat_m.shape)
        modsel = jnp.stack([ctx_m, lat_m], axis=1)

        p_all = _inproj(h, modsel, norm_mix[l], w_in[l], cos_t, sin_t, n_ctx)

        lam_init = 0.8 - 0.6 * math.exp(-0.3 * l)
        lq1, lk1, lq2, lk2 = diff_lam[l].astype(F32)
        lam = jnp.exp(jnp.sum(lq1 * lk1)) - jnp.exp(jnp.sum(lq2 * lk2)) + lam_init

        mixes = (_pool(p_all, pool_w[l], pool_scale[l], segments),
                 _diff(p_all, lam, diff_subln[l], n_ctx, lam_init),
                 _sgu(p_all, sgu_ln_g[l], sgu_ln_b[l], sgu_w[l], sgu_b[l]),
                 _win(p_all, win_sink[l], n_ctx))
        h = _outproj(mixes, w_out[l], h, modsel, n_ctx)

        i = l // 2
        row0 = n_ctx if last else 0
        rows = b * (t - row0)
        if l % 2 == 0:
            f = _ffn_in(h, modsel, norm_ffn[l], n_ctx, row0).reshape(rows, d)
            n_tiles = rows // FFN_TM
            te = jnp.zeros((n_tiles,), jnp.int32)
            tr = jnp.full((n_tiles,), FFN_TM, jnp.int32)
            y = _ffn(f, ffn_w1[i:i + 1], ffn_w3[i:i + 1], ffn_w2[i:i + 1], te, tr)
            assert not last
            h = _residual(h, y.reshape(b, t - row0, d), modsel, n_ctx)
        else:
            assert last
            f, gates, sel = _ffn_in(h, modsel, norm_ffn[l], n_ctx, row0, router=moe_router[i])
            f = f.reshape(rows, d)
            n_exp = moe_w1.shape[1]
            n_tiles = (2 * rows) // FFN_TM + n_exp
            src, pos0, pos1, w0, w1, te, tr = _routing_tables(
                sel.reshape(rows, LANES)[:, :n_exp] > 0, gates.reshape(rows, LANES)[:, :n_exp], n_tiles)
            ys = _ffn(jnp.take(f, src, axis=0), moe_w1[i], moe_w3[i], moe_w2[i], te, tr)
            shp = (b, t - row0, d)
            out = _moe_final(h, jnp.take(ys, pos0, axis=0).reshape(shp), jnp.take(ys, pos1, axis=0).reshape(shp),
                             w0.reshape(b, t - row0, 1), w1.reshape(b, t - row0, 1), modsel, norm_final, row0)
    return out
```

```python
import functools
import math

import jax
import jax.numpy as jnp
from jax import lax
from jax.experimental import pallas as pl
from jax.experimental.pallas import tpu as pltpu

F32 = jnp.float32
BF16 = jnp.bfloat16

GRID_W = 64
GROUP_WIDTH = 512
HEAD_DIM = 64
ROPE_BASE = 10000.0
EPS = 1e-6
NEG_INF = -1e30
LANES = 128

POOL_WINDOWS = (2, 4, 8, 16)
POOL_PAD = 16
DIFF_HEADS = GROUP_WIDTH // (2 * HEAD_DIM)
SGU_CHUNK = 128
SGU_GROUPS = 4
WIN_HEADS = GROUP_WIDTH // HEAD_DIM
WIN_KV_HEADS = 2
WIN_REP = WIN_HEADS // WIN_KV_HEADS
WINDOW = 128
WIN_BLOCK = 128
N_EXPERTS = 8

COL_POOL = 0
COL_DQ = GROUP_WIDTH
COL_DK = 2 * GROUP_WIDTH
COL_DV = 3 * GROUP_WIDTH
COL_Z = 4 * GROUP_WIDTH
COL_WQ = 6 * GROUP_WIDTH
COL_WK = COL_WQ + WIN_HEADS * HEAD_DIM
COL_WV = COL_WK + WIN_KV_HEADS * HEAD_DIM
IN_COLS = COL_WV + WIN_KV_HEADS * HEAD_DIM
ROPE_RANGES = ((COL_DQ, COL_DV), (COL_WQ, COL_WV))

VMEM_LIMIT = 56 * 1024 * 1024

NT_DIMS = (((1,), (1,)), ((), ()))


def _params(sem, vmem=VMEM_LIMIT):
    return pltpu.CompilerParams(dimension_semantics=sem, vmem_limit_bytes=vmem)


def _sigmoid(x):
    return 1.0 / (1.0 + jnp.exp(-x))


def _row_is_ctx(tile_idx, tm, n_ctx):
    rows = tile_idx * tm + lax.broadcasted_iota(jnp.int32, (tm, 1), 0)
    return rows < n_ctx


def _mod_row(m_ref, k, is_ctx):
    return jnp.where(is_ctx, m_ref[0, k:k + 1, :], m_ref[1, k:k + 1, :])


def _norm_modulate(x, g, m_ref, k_shift, is_ctx):
    ms = jnp.mean(x * x, axis=-1, keepdims=True)
    y = x * lax.rsqrt(ms + EPS) * g
    return y * (1.0 + _mod_row(m_ref, k_shift + 1, is_ctx)) + _mod_row(m_ref, k_shift, is_ctx)


def _adaln_kernel(c_ref, w_ref, b_ref, o_ref):
    cv = c_ref[...]
    s = (cv * _sigmoid(cv)).astype(BF16)
    o_ref[0] = jnp.dot(s, w_ref[0].astype(BF16), preferred_element_type=F32) + b_ref[0]


def _adaln(c_rows, w_mod, b_mod, tn=1024):
    depth, d, n = w_mod.shape
    r = c_rows.shape[0]
    return pl.pallas_call(
        _adaln_kernel,
        grid=(depth, n // tn),
        in_specs=[pl.BlockSpec((r, d), lambda l, j: (0, 0)),
                  pl.BlockSpec((1, d, tn), lambda l, j: (l, 0, j)),
                  pl.BlockSpec((1, 1, tn), lambda l, j: (l, 0, j))],
        out_specs=pl.BlockSpec((1, r, tn), lambda l, j: (l, 0, j)),
        out_shape=jax.ShapeDtypeStruct((depth, r, n), F32),
        compiler_params=_params(("arbitrary", "arbitrary")),
        name="adaln",
    )(c_rows, w_mod, b_mod.reshape(depth, 1, n))


def _is_rope_col(col):
    return any(lo <= col < hi for lo, hi in ROPE_RANGES)


def _inproj_kernel(x_ref, m_ref, g_ref, w_ref, cos_ref, sin_ref, o_ref, xn_ref, *, tm, tn, n_ctx):
    ti = pl.program_id(1)
    j = pl.program_id(2)

    @pl.when(j == 0)
    def _():
        is_ctx = _row_is_ctx(ti, tm, n_ctx)
        xn_ref[...] = _norm_modulate(x_ref[0], g_ref[...], m_ref, 0, is_ctx).astype(BF16)

    acc = jnp.dot(xn_ref[...], w_ref[...].astype(BF16), preferred_element_type=F32)
    lane = lax.broadcasted_iota(jnp.int32, (1, LANES), 1)
    first_half = (lane % HEAD_DIM) < (HEAD_DIM // 2)

    for jj in range(IN_COLS // tn):
        @pl.when(j == jj)
        def _(jj=jj):
            for cc in range(tn // LANES):
                a = acc[:, cc * LANES:(cc + 1) * LANES]
                if _is_rope_col(jj * tn + cc * LANES):
                    partner = jnp.where(first_half,
                                        pltpu.roll(a, LANES - HEAD_DIM // 2, 1),
                                        pltpu.roll(a, HEAD_DIM // 2, 1))
                    a = a * cos_ref[...] + partner * sin_ref[...]
                o_ref[0, :, cc * LANES:(cc + 1) * LANES] = a


def _inproj(h, modsel, g, w, layer, cos_t, sin_t, n_ctx, tm=768, tn=768):
    b, t, d = h.shape
    kern = functools.partial(_inproj_kernel, tm=tm, tn=tn, n_ctx=n_ctx)
    return pl.pallas_call(
        kern,
        grid=(b, t // tm, IN_COLS // tn),
        in_specs=[pl.BlockSpec((1, tm, d), lambda bi, ti, j: (bi, ti, 0)),
                  pl.BlockSpec((None, 2, 6, d), lambda bi, ti, j: (bi, 0, 0, 0)),
                  pl.BlockSpec((1, d), lambda bi, ti, j: (0, 0)),
                  pl.BlockSpec((None, d, tn), lambda bi, ti, j: (layer, 0, j)),
                  pl.BlockSpec((tm, LANES), lambda bi, ti, j: (ti, 0)),
                  pl.BlockSpec((tm, LANES), lambda bi, ti, j: (ti, 0))],
        out_specs=pl.BlockSpec((1, tm, tn), lambda bi, ti, j: (bi, ti, j)),
        out_shape=jax.ShapeDtypeStruct((b, t, IN_COLS), F32),
        scratch_shapes=[pltpu.VMEM((tm, d), BF16)],
        compiler_params=_params(("arbitrary", "arbitrary", "arbitrary")),
        name="inproj",
    )(h, modsel, g.reshape(1, d), w, cos_t, sin_t)


def _rope_tables(n_ctx, n_lat):
    n_rows = n_lat // GRID_W
    rows = jnp.repeat(jnp.arange(n_rows, dtype=F32), GRID_W)
    cols = jnp.tile(jnp.arange(GRID_W, dtype=F32), n_rows)
    n_freq = HEAD_DIM // 4
    inv_freq = ROPE_BASE ** (-jnp.arange(n_freq, dtype=F32) / n_freq)
    ang = jnp.concatenate([rows[:, None] * inv_freq, cols[:, None] * inv_freq], axis=-1)
    cos, sin = jnp.cos(ang), jnp.sin(ang)
    cos_h = jnp.concatenate([cos, cos], axis=-1)
    sin_h = jnp.concatenate([-sin, sin], axis=-1)
    cos_t = jnp.concatenate([jnp.ones((n_ctx, HEAD_DIM), F32), cos_h], axis=0)
    sin_t = jnp.concatenate([jnp.zeros((n_ctx, HEAD_DIM), F32), sin_h], axis=0)
    rep = LANES // HEAD_DIM
    return jnp.tile(cos_t, (1, rep)), jnp.tile(sin_t, (1, rep))


def _pool_kernel(a_ref, w_ref, s_ref, o_ref, pad_ref, *, segments):
    for g, win in enumerate(POOL_WINDOWS):
        half = win // 2
        cs = slice(g * LANES, (g + 1) * LANES)
        wg = w_ref[g].astype(BF16)
        for s0, n in segments:
            a = a_ref[0, s0:s0 + n, cs]
            pad_ref[0:POOL_PAD, :] = jnp.zeros((POOL_PAD, LANES), F32)
            pad_ref[POOL_PAD:POOL_PAD + n, :] = a
            pad_ref[POOL_PAD + n:2 * POOL_PAD + n, :] = jnp.zeros((POOL_PAD, LANES), F32)
            tot = pad_ref[POOL_PAD - half:POOL_PAD - half + n, :]
            for k in range(-half + 1, half):
                tot = tot + pad_ref[POOL_PAD + k:POOL_PAD + k + n, :]
            t = lax.broadcasted_iota(jnp.int32, (n, 1), 0)
            cnt = (jnp.minimum(t + half, n) - jnp.maximum(t - half, 0)).astype(F32)
            dlt = (tot / cnt - a).astype(BF16)
            y = jnp.dot(dlt, wg, preferred_element_type=F32)
            o_ref[0, s0:s0 + n, cs] = y * s_ref[:, cs]


def _pool(p_all, w, scale, segments):
    b, t, _ = p_all.shape
    max_n = max(n for _, n in segments)
    kern = functools.partial(_pool_kernel, segments=segments)
    return pl.pallas_call(
        kern,
        grid=(b,),
        in_specs=[pl.BlockSpec((1, t, GROUP_WIDTH), lambda bi: (bi, 0, COL_POOL // GROUP_WIDTH)),
                  pl.BlockSpec(w.shape, lambda bi: (0, 0, 0)),
                  pl.BlockSpec((1, GROUP_WIDTH), lambda bi: (0, 0))],
        out_specs=pl.BlockSpec((1, t, GROUP_WIDTH), lambda bi: (bi, 0, 0)),
        out_shape=jax.ShapeDtypeStruct((b, t, GROUP_WIDTH), F32),
        scratch_shapes=[pltpu.VMEM((max_n + 2 * POOL_PAD, LANES), F32)],
        compiler_params=_params(("arbitrary",)),
        name="pool",
    )(p_all, w, scale.reshape(1, GROUP_WIDTH))


def _diff_kernel(lam_ref, q_ref, k_ref, v_ref, g_ref, o_ref, *, tq, n_ctx, out_scale):
    ti = pl.program_id(2)
    lam = lam_ref[0]
    lane = lax.broadcasted_iota(jnp.int32, (1, 2 * HEAD_DIM), 1)
    q = q_ref[0] * (HEAD_DIM ** -0.5)
    q1 = jnp.where(lane < HEAD_DIM, q, 0.0).astype(BF16)
    q2 = jnp.where(lane >= HEAD_DIM, q, 0.0).astype(BF16)

    def attend(n_keys):
        k = k_ref[0, 0:n_keys, :].astype(BF16)
        v = v_ref[0, 0:n_keys, :].astype(BF16)
        s1 = lax.dot_general(q1, k, NT_DIMS, preferred_element_type=F32)
        s2 = lax.dot_general(q2, k, NT_DIMS, preferred_element_type=F32)
        e1 = jnp.exp(s1 - jnp.max(s1, axis=-1, keepdims=True))
        e2 = jnp.exp(s2 - jnp.max(s2, axis=-1, keepdims=True))
        r1 = 1.0 / jnp.sum(e1, axis=-1, keepdims=True)
        r2 = lam / jnp.sum(e2, axis=-1, keepdims=True)
        a = (e1 * r1 - e2 * r2).astype(BF16)
        o = jnp.dot(a, v, preferred_element_type=F32)
        ms = jnp.mean(o * o, axis=-1, keepdims=True)
        o_ref[0] = o * lax.rsqrt(ms + EPS) * g_ref[...] * out_scale

    @pl.when(ti * tq < n_ctx)
    def _():
        attend(n_ctx)

    @pl.when(ti * tq >= n_ctx)
    def _():
        attend(k_ref.shape[1])


def _diff(p_all, lam, subln, n_ctx, lam_init, tq=256):
    b, t, _ = p_all.shape
    assert n_ctx % tq == 0 and t % tq == 0
    w = 2 * HEAD_DIM
    kern = functools.partial(_diff_kernel, tq=tq, n_ctx=n_ctx, out_scale=1.0 - lam_init)
    return pl.pallas_call(
        kern,
        grid=(b, DIFF_HEADS, t // tq),
        in_specs=[pl.BlockSpec(memory_space=pltpu.SMEM),
                  pl.BlockSpec((1, tq, w), lambda bi, hi, ti: (bi, ti, COL_DQ // w + hi)),
                  pl.BlockSpec((1, t, w), lambda bi, hi, ti: (bi, 0, COL_DK // w + hi)),
                  pl.BlockSpec((1, t, w), lambda bi, hi, ti: (bi, 0, COL_DV // w + hi)),
                  pl.BlockSpec((1, w), lambda bi, hi, ti: (0, 0))],
        out_specs=pl.BlockSpec((1, tq, w), lambda bi, hi, ti: (bi, ti, hi)),
        out_shape=jax.ShapeDtypeStruct((b, t, GROUP_WIDTH), F32),
        compiler_params=_params(("arbitrary", "arbitrary", "arbitrary")),
        name="diff_attn",
    )(lam.reshape(1), p_all, p_all, p_all, subln.reshape(1, w))


def _gelu_tanh(x):
    return 0.5 * x * (1.0 + jnp.tanh(math.sqrt(2.0 / math.pi) * (x + 0.044715 * (x * x * x))))


def _sgu_kernel(z_ref, g_ref, b_ref, w_ref, bs_ref, o_ref, *, tm):
    z = _gelu_tanh(z_ref[0])
    u = z[:, :GROUP_WIDTH]
    v = z[:, GROUP_WIDTH:]
    mu = jnp.mean(v, axis=-1, keepdims=True)
    var = jnp.mean(jnp.square(v - mu), axis=-1, keepdims=True)
    vn = ((v - mu) * lax.rsqrt(var + EPS) * g_ref[...] + b_ref[...]).astype(BF16)
    for g in range(SGU_GROUPS):
        wg = w_ref[g].astype(BF16)
        bias = bs_ref[:, g:g + 1]
        cs = slice(g * LANES, (g + 1) * LANES)
        for c in range(tm // SGU_CHUNK):
            rs = slice(c * SGU_CHUNK, (c + 1) * SGU_CHUNK)
            sv = jnp.dot(wg, vn[rs, cs], preferred_element_type=F32) + bias
            o_ref[0, rs, cs] = u[rs, cs] * sv


def _sgu(p_all, ln_g, ln_b, w_s, b_s, tm=768):
    b, t, _ = p_all.shape
    zw = 2 * GROUP_WIDTH
    kern = functools.partial(_sgu_kernel, tm=tm)
    return pl.pallas_call(
        kern,
        grid=(b, t // tm),
        in_specs=[pl.BlockSpec((1, tm, zw), lambda bi, ti: (bi, ti, COL_Z // zw)),
                  pl.BlockSpec((1, GROUP_WIDTH), lambda bi, ti: (0, 0)),
                  pl.BlockSpec((1, GROUP_WIDTH), lambda bi, ti: (0, 0)),
                  pl.BlockSpec(w_s.shape, lambda bi, ti: (0, 0, 0)),
                  pl.BlockSpec((SGU_CHUNK, SGU_GROUPS), lambda bi, ti: (0, 0))],
        out_specs=pl.BlockSpec((1, tm, GROUP_WIDTH), lambda bi, ti: (bi, ti, 0)),
        out_shape=jax.ShapeDtypeStruct((b, t, GROUP_WIDTH), F32),
        compiler_params=_params(("arbitrary", "arbitrary")),
        name="sgu",
    )(p_all, ln_g.reshape(1, GROUP_WIDTH), ln_b.reshape(1, GROUP_WIDTH), w_s, b_s.T)


def _win_kernel(sink_ref, q_ref, k_ref, v_ref, o_ref, kk_ref, vv_ref, *, n_ctx):
    g = pl.program_id(1)
    n = pl.program_id(2)
    t = k_ref.shape[1]
    blk = WIN_BLOCK
    band = 3 * blk
    lane = lax.broadcasted_iota(jnp.int32, (1, LANES), 1)
    low = lane < HEAD_DIM

    @pl.when(n == 0)
    def _():
        mine = (lane >= g * HEAD_DIM) & (lane < (g + 1) * HEAD_DIM)
        km = jnp.where(mine, k_ref[0], 0.0)
        vm = jnp.where(mine, v_ref[0], 0.0)
        kk_ref[...] = (km + pltpu.roll(km, HEAD_DIM, 1)).astype(BF16)
        vv_ref[...] = (vm + pltpu.roll(vm, HEAD_DIM, 1)).astype(BF16)

    start = pl.multiple_of(jnp.clip((n - 1) * blk, 0, t - band), blk)
    k_ctx = kk_ref[0:n_ctx, :]
    v_ctx = vv_ref[0:n_ctx, :]
    k_band = kk_ref[pl.ds(start, band), :]
    v_band = vv_ref[pl.ds(start, band), :]
    qpos = n * blk + lax.broadcasted_iota(jnp.int32, (blk, 1), 0)
    kpos = start + lax.broadcasted_iota(jnp.int32, (1, band), 1)
    valid = (kpos >= n_ctx) & (qpos >= n_ctx) & (jnp.abs(kpos - qpos) <= WINDOW)

    q = q_ref[0] * (HEAD_DIM ** -0.5)
    for pair in range(WIN_REP // 2):
        qp = q[:, pair * LANES:(pair + 1) * LANES]
        halves = []
        for hf in range(2):
            qm = jnp.where(low if hf == 0 else jnp.logical_not(low), qp, 0.0).astype(BF16)
            s_c = lax.dot_general(qm, k_ctx, NT_DIMS, preferred_element_type=F32)
            s_b = lax.dot_general(qm, k_band, NT_DIMS, preferred_element_type=F32)
            s_b = jnp.where(valid, s_b, NEG_INF)
            sink = sink_ref[g * WIN_REP + pair * 2 + hf]
            m = jnp.maximum(jnp.maximum(jnp.max(s_c, axis=-1, keepdims=True),
                                        jnp.max(s_b, axis=-1, keepdims=True)), sink)
            e_c = jnp.exp(s_c - m)
            e_b = jnp.exp(s_b - m)
            den = (jnp.sum(e_c, axis=-1, keepdims=True) + jnp.sum(e_b, axis=-1, keepdims=True)
                   + jnp.exp(sink - m))
            o = (jnp.dot(e_c.astype(BF16), v_ctx, preferred_element_type=F32)
                 + jnp.dot(e_b.astype(BF16), v_band, preferred_element_type=F32))
            halves.append(o / den)
        o_ref[0, :, pair * LANES:(pair + 1) * LANES] = jnp.where(low, halves[0], halves[1])


def _win(p_all, sink, n_ctx):
    b, t, _ = p_all.shape
    qw = WIN_REP * HEAD_DIM
    kern = functools.partial(_win_kernel, n_ctx=n_ctx)
    return pl.pallas_call(
        kern,
        grid=(b, WIN_KV_HEADS, t // WIN_BLOCK),
        in_specs=[pl.BlockSpec(memory_space=pltpu.SMEM),
                  pl.BlockSpec((1, WIN_BLOCK, qw), lambda bi, gi, ni: (bi, ni, COL_WQ // qw + gi)),
                  pl.BlockSpec((1, t, LANES), lambda bi, gi, ni: (bi, 0, COL_WK // LANES)),
                  pl.BlockSpec((1, t, LANES), lambda bi, gi, ni: (bi, 0, COL_WV // LANES))],
        out_specs=pl.BlockSpec((1, WIN_BLOCK, qw), lambda bi, gi, ni: (bi, ni, gi)),
        out_shape=jax.ShapeDtypeStruct((b, t, GROUP_WIDTH), F32),
        scratch_shapes=[pltpu.VMEM((t, LANES), BF16), pltpu.VMEM((t, LANES), BF16)],
        compiler_params=_params(("arbitrary", "arbitrary", "arbitrary")),
        name="win_attn",
    )(sink, p_all, p_all, p_all)


def _outproj_kernel(a_ref, b_ref, c_ref, d_ref, w_ref, h_ref, m_ref, o_ref, mix_ref, *, tm, n_ctx):
    ti = pl.program_id(1)

    @pl.when(pl.program_id(2) == 0)
    def _():
        for i, r in enumerate((a_ref, b_ref, c_ref, d_ref)):
            mix_ref[:, i * GROUP_WIDTH:(i + 1) * GROUP_WIDTH] = r[0].astype(BF16)

    y = jnp.dot(mix_ref[...], w_ref[...].astype(BF16), preferred_element_type=F32)
    gate = _mod_row(m_ref, 2, _row_is_ctx(ti, tm, n_ctx))
    o_ref[0] = h_ref[0] + gate * y


def _outproj(mixes, w, layer, h, modsel, n_ctx, tm=768, tn=512):
    b, t, d = h.shape
    kern = functools.partial(_outproj_kernel, tm=tm, n_ctx=n_ctx)
    mix_spec = pl.BlockSpec((1, tm, GROUP_WIDTH), lambda bi, ti, j: (bi, ti, 0))
    return pl.pallas_call(
        kern,
        grid=(b, t // tm, d // tn),
        in_specs=[mix_spec, mix_spec, mix_spec, mix_spec,
                  pl.BlockSpec((None, 4 * GROUP_WIDTH, tn), lambda bi, ti, j: (layer, 0, j)),
                  pl.BlockSpec((1, tm, tn), lambda bi, ti, j: (bi, ti, j)),
                  pl.BlockSpec((None, 2, 6, tn), lambda bi, ti, j: (bi, 0, 0, j))],
        out_specs=pl.BlockSpec((1, tm, tn), lambda bi, ti, j: (bi, ti, j)),
        out_shape=jax.ShapeDtypeStruct((b, t, d), F32),
        scratch_shapes=[pltpu.VMEM((tm, 4 * GROUP_WIDTH), BF16)],
        compiler_params=_params(("arbitrary", "arbitrary", "arbitrary")),
        name="outproj",
    )(*mixes, w, h, modsel)


def _ffn_in_kernel(x_ref, m_ref, g_ref, o_ref, *, tm, n_ctx, row0):
    is_ctx = _row_is_ctx(pl.program_id(1), tm, n_ctx - row0)
    o_ref[0] = _norm_modulate(x_ref[0], g_ref[...], m_ref, 3, is_ctx)


def _route_kernel(x_ref, m_ref, g_ref, r_ref, o_ref, gate_ref, sel_ref, *, tm, n_ctx, row0):
    is_ctx = _row_is_ctx(pl.program_id(1), tm, n_ctx - row0)
    f = _norm_modulate(x_ref[0], g_ref[...], m_ref, 3, is_ctx)
    o_ref[0] = f
    logits = jnp.dot(f, r_ref[...], preferred_element_type=F32, precision=lax.Precision.HIGHEST)
    lane = lax.broadcasted_iota(jnp.int32, (1, LANES), 1)
    logits = jnp.where(lane < N_EXPERTS, logits, -jnp.inf)
    m1 = jnp.max(logits, axis=-1, keepdims=True)
    i1 = jnp.min(jnp.where(logits == m1, lane, LANES), axis=-1, keepdims=True)
    rest = jnp.where(lane == i1, -jnp.inf, logits)
    m2 = jnp.max(rest, axis=-1, keepdims=True)
    i2 = jnp.min(jnp.where(rest == m2, lane, LANES), axis=-1, keepdims=True)
    e2 = jnp.exp(m2 - m1)
    w1 = 1.0 / (1.0 + e2)
    w2 = e2 / (1.0 + e2)
    gate_ref[0] = jnp.where(lane == i1, w1, 0.0) + jnp.where(lane == i2, w2, 0.0)
    sel_ref[0] = ((lane == i1) | (lane == i2)).astype(jnp.int32)


def _ffn_in(h, modsel, g, n_ctx, row0, router=None, tm=256):
    b, t, d = h.shape
    nt = (t - row0) // tm
    off = row0 // tm
    in_specs = [pl.BlockSpec((1, tm, d), lambda bi, ti: (bi, ti + off, 0)),
                pl.BlockSpec((None, 2, 6, d), lambda bi, ti: (bi, 0, 0, 0)),
                pl.BlockSpec((1, d), lambda bi, ti: (0, 0))]
    f_spec = pl.BlockSpec((1, tm, d), lambda bi, ti: (bi, ti, 0))
    f_shape = jax.ShapeDtypeStruct((b, t - row0, d), F32)
    if router is None:
        kern = functools.partial(_ffn_in_kernel, tm=tm, n_ctx=n_ctx, row0=row0)
        return pl.pallas_call(
            kern, grid=(b, nt), in_specs=in_specs, out_specs=f_spec, out_shape=f_shape,
            compiler_params=_params(("arbitrary", "arbitrary")), name="ffn_in",
        )(h, modsel, g.reshape(1, d))
    r_pad = jnp.zeros((d, LANES), F32).at[:, :N_EXPERTS].set(router)
    e_spec = pl.BlockSpec((1, tm, LANES), lambda bi, ti: (bi, ti, 0))
    kern = functools.partial(_route_kernel, tm=tm, n_ctx=n_ctx, row0=row0)
    return pl.pallas_call(
        kern, grid=(b, nt),
        in_specs=in_specs + [pl.BlockSpec((d, LANES), lambda bi, ti: (0, 0))],
        out_specs=[f_spec, e_spec, e_spec],
        out_shape=[f_shape, jax.ShapeDtypeStruct((b, t - row0, LANES), F32),
                   jax.ShapeDtypeStruct((b, t - row0, LANES), jnp.int32)],
        compiler_params=_params(("arbitrary", "arbitrary")), name="ffn_in_route",
    )(h, modsel, g.reshape(1, d), r_pad)


FFN_TM = 1024
FFN_SUB = 256


def _ffn_kernel(te_ref, tr_ref, x_ref, w1_ref, w3_ref, w2_ref, o_ref, xs_ref):
    i = pl.program_id(0)
    j = pl.program_id(1)
    n_blocks = (tr_ref[i] + FFN_SUB - 1) // FFN_SUB

    @pl.when(j == 0)
    def _():
        o_ref[...] = jnp.zeros_like(o_ref)
        xs_ref[...] = x_ref[...].astype(BF16)

    for nb in range(1, FFN_TM // FFN_SUB + 1):
        @pl.when(n_blocks == nb)
        def _(nb=nb):
            m = nb * FFN_SUB
            xs = xs_ref[0:m, :]
            h1 = jnp.dot(xs, w1_ref[0].astype(BF16), preferred_element_type=F32)
            h3 = jnp.dot(xs, w3_ref[0].astype(BF16), preferred_element_type=F32)
            act = (h1 * _sigmoid(h1) * h3).astype(BF16)
            o_ref[0:m, :] += jnp.dot(act, w2_ref[0].astype(BF16), preferred_element_type=F32)


def _ffn(x_rows, w1, w3, w2, tile_expert, tile_rows, tf=256):
    nr, d = x_rows.shape
    n_exp, _, f = w1.shape
    nj = f // tf

    def wcol(i, j, te, tr):
        return (te[i], 0, jnp.where(tr[i] > 0, j, nj - 1))

    def wrow(i, j, te, tr):
        return (te[i], jnp.where(tr[i] > 0, j, nj - 1), 0)

    return pl.pallas_call(
        _ffn_kernel,
        grid_spec=pltpu.PrefetchScalarGridSpec(
            num_scalar_prefetch=2,
            grid=(nr // FFN_TM, nj),
            in_specs=[pl.BlockSpec((FFN_TM, d), lambda i, j, te, tr: (i, 0)),
                      pl.BlockSpec((1, d, tf), wcol),
                      pl.BlockSpec((1, d, tf), wcol),
                      pl.BlockSpec((1, tf, d), wrow)],
            out_specs=pl.BlockSpec((FFN_TM, d), lambda i, j, te, tr: (i, 0)),
            scratch_shapes=[pltpu.VMEM((FFN_TM, d), BF16)]),
        out_shape=jax.ShapeDtypeStruct((nr, d), F32),
        compiler_params=_params(("arbitrary", "arbitrary")),
        name="ffn",
    )(tile_expert, tile_rows, x_rows, w1, w3, w2)


def _residual_kernel(h_ref, y_ref, m_ref, o_ref, *, tm, n_ctx):
    gate = _mod_row(m_ref, 5, _row_is_ctx(pl.program_id(1), tm, n_ctx))
    o_ref[0] = h_ref[0] + gate * y_ref[0]


def _residual(h, y, modsel, n_ctx, tm=256):
    b, t, d = h.shape
    spec = pl.BlockSpec((1, tm, d), lambda bi, ti: (bi, ti, 0))
    kern = functools.partial(_residual_kernel, tm=tm, n_ctx=n_ctx)
    return pl.pallas_call(
        kern, grid=(b, t // tm),
        in_specs=[spec, spec, pl.BlockSpec((None, 2, 6, d), lambda bi, ti: (bi, 0, 0, 0))],
        out_specs=spec, out_shape=jax.ShapeDtypeStruct((b, t, d), F32),
        compiler_params=_params(("arbitrary", "arbitrary")), name="residual",
    )(h, y, modsel)


def _moe_final_kernel(h_ref, y0_ref, y1_ref, w0_ref, w1_ref, m_ref, g_ref, o_ref):
    y = w0_ref[0] * y0_ref[0] + w1_ref[0] * y1_ref[0]
    hn = h_ref[0] + m_ref[1, 5:6, :] * y
    ms = jnp.mean(hn * hn, axis=-1, keepdims=True)
    o_ref[0] = hn * lax.rsqrt(ms + EPS) * g_ref[...]


def _moe_final(h, y0, y1, w0, w1, modsel, g, row0, tm=256):
    b, t, d = h.shape
    n_lat = t - row0
    off = row0 // tm
    spec = pl.BlockSpec((1, tm, d), lambda bi, ti: (bi, ti, 0))
    wspec = pl.BlockSpec((1, tm, 1), lambda bi, ti: (bi, ti, 0))
    return pl.pallas_call(
        _moe_final_kernel, grid=(b, n_lat // tm),
        in_specs=[pl.BlockSpec((1, tm, d), lambda bi, ti: (bi, ti + off, 0)), spec, spec, wspec, wspec,
                  pl.BlockSpec((None, 2, 6, d), lambda bi, ti: (bi, 0, 0, 0)),
                  pl.BlockSpec((1, d), lambda bi, ti: (0, 0))],
        out_specs=spec, out_shape=jax.ShapeDtypeStruct((b, n_lat, d), F32),
        compiler_params=_params(("arbitrary", "arbitrary")), name="moe_final",
    )(h, y0, y1, w0, w1, modsel, g.reshape(1, d))


def _routing_tables(sel, gates, n_tiles):
    n, n_exp = sel.shape
    seli = sel.astype(jnp.int32)
    counts = jnp.sum(seli, axis=0)
    rank = jnp.cumsum(seli, axis=0) - seli
    tiles_e = (counts + FFN_TM - 1) // FFN_TM
    tile_end = jnp.cumsum(tiles_e)
    tile_start = tile_end - tiles_e
    pos = tile_start[None, :] * FFN_TM + rank
    n_rows = n_tiles * FFN_TM
    expert = jnp.arange(n_exp, dtype=jnp.int32)[None, :]
    e_lo = jnp.min(jnp.where(sel, expert, n_exp), axis=1, keepdims=True)
    e_hi = jnp.max(jnp.where(sel, expert, -1), axis=1, keepdims=True)
    pick = lambda a, e: jnp.sum(jnp.where(expert == e, a, 0), axis=1)
    pos0, pos1 = pick(pos, e_lo), pick(pos, e_hi)
    w0, w1 = pick(gates, e_lo), pick(gates, e_hi)
    tok = jnp.arange(n, dtype=jnp.int32)
    src = jnp.zeros((n_rows,), jnp.int32).at[jnp.concatenate([pos0, pos1])].set(
        jnp.concatenate([tok, tok]), mode="promise_in_bounds", unique_indices=True)

    tile = jnp.arange(n_tiles, dtype=jnp.int32)
    te = jnp.minimum(jnp.sum((tile_end[None, :] <= tile[:, None]).astype(jnp.int32), axis=1), n_exp - 1)
    tr = jnp.clip(counts[te] - (tile - tile_start[te]) * FFN_TM, 0, FFN_TM)
    tr = jnp.where(tile < tile_end[-1], tr, 0)
    return src, pos0, pos1, w0, w1, te.astype(jnp.int32), tr.astype(jnp.int32)


def kernel(x, c, ctx, c_ctx, w_mod, b_mod, norm_mix, norm_ffn, w_in, w_out, pool_w, pool_scale, diff_lam,
           diff_subln, sgu_ln_g, sgu_ln_b, sgu_w, sgu_b, win_sink, ffn_w1, ffn_w3, ffn_w2, moe_router,
           moe_w1, moe_w3, moe_w2, norm_final):
    b, n_lat, d = x.shape
    n_ctx = ctx.shape[1]
    t = n_ctx + n_lat
    depth = w_mod.shape[0]
    segments = ((0, n_ctx), (n_ctx, n_lat))

    c_rows = jnp.zeros((8, d), F32).at[:b].set(c).at[b].set(c_ctx)
    mod = _adaln(c_rows, w_mod, b_mod).reshape(depth, 8, 6, d)
    cos_t, sin_t = _rope_tables(n_ctx, n_lat)
    h = jnp.concatenate([ctx, x], axis=1)

    out = None
    for l in range(depth):
        last = l == depth - 1
        lat_m = mod[l, :b]
        ctx_m = jnp.broadcast_to(mod[l, b:b + 1], lat_m.shape)
        modsel = jnp.stack([ctx_m, lat_m], axis=1)

        p_all = _inproj(h, modsel, norm_mix[l], w_in, l, cos_t, sin_t, n_ctx)

        lam_init = 0.8 - 0.6 * math.exp(-0.3 * l)
        lq1, lk1, lq2, lk2 = diff_lam[l].astype(F32)
        lam = jnp.exp(jnp.sum(lq1 * lk1)) - jnp.exp(jnp.sum(lq2 * lk2)) + lam_init

        mixes = (_pool(p_all, pool_w[l], pool_scale[l], segments),
                 _diff(p_all, lam, diff_subln[l], n_ctx, lam_init),
                 _sgu(p_all, sgu_ln_g[l], sgu_ln_b[l], sgu_w[l], sgu_b[l]),
                 _win(p_all, win_sink[l], n_ctx))
        h = _outproj(mixes, w_out, l, h, modsel, n_ctx)

        i = l // 2
        row0 = n_ctx if last else 0
        rows = b * (t - row0)
        if l % 2 == 0:
            f = _ffn_in(h, modsel, norm_ffn[l], n_ctx, row0).reshape(rows, d)
            n_tiles = rows // FFN_TM
            te = jnp.zeros((n_tiles,), jnp.int32)
            tr = jnp.full((n_tiles,), FFN_TM, jnp.int32)
            y = _ffn(f, ffn_w1[i:i + 1], ffn_w3[i:i + 1], ffn_w2[i:i + 1], te, tr)
            assert not last
            h = _residual(h, y.reshape(b, t - row0, d), modsel, n_ctx)
        else:
            assert last
            f, gates, sel = _ffn_in(h, modsel, norm_ffn[l], n_ctx, row0, router=moe_router[i])
            f = f.reshape(rows, d)
            n_exp = moe_w1.shape[1]
            n_tiles = (2 * rows) // FFN_TM + n_exp
            src, pos0, pos1, w0, w1, te, tr = _routing_tables(
                sel.reshape(rows, LANES)[:, :n_exp] > 0, gates.reshape(rows, LANES)[:, :n_exp], n_tiles)
            rows_of = lambda a, idx: a.at[idx].get(mode="promise_in_bounds")
            ys = _ffn(rows_of(f, src), moe_w1[i], moe_w3[i], moe_w2[i], te, tr)
            shp = (b, t - row0, d)
            out = _moe_final(h, rows_of(ys, pos0).reshape(shp), rows_of(ys, pos1).reshape(shp),
                             w0.reshape(b, t - row0, 1), w1.reshape(b, t - row0, 1), modsel, norm_final, row0)
    return out
```

```python
import functools
import math

import jax
import jax.numpy as jnp
from jax import lax
from jax.experimental import pallas as pl
from jax.experimental.pallas import tpu as pltpu

F32 = jnp.float32
BF16 = jnp.bfloat16

GRID_W = 64
GROUP_WIDTH = 512
HEAD_DIM = 64
ROPE_BASE = 10000.0
EPS = 1e-6
NEG_INF = -1e30
LANES = 128

POOL_WINDOWS = (2, 4, 8, 16)
POOL_PAD = 16
DIFF_HEADS = GROUP_WIDTH // (2 * HEAD_DIM)
SGU_CHUNK = 128
SGU_GROUPS = 4
WIN_HEADS = GROUP_WIDTH // HEAD_DIM
WIN_KV_HEADS = 2
WIN_REP = WIN_HEADS // WIN_KV_HEADS
WINDOW = 128
WIN_BLOCK = 128
N_EXPERTS = 8

COL_POOL = 0
COL_DQ = GROUP_WIDTH
COL_DK = 2 * GROUP_WIDTH
COL_DV = 3 * GROUP_WIDTH
COL_Z = 4 * GROUP_WIDTH
COL_WQ = 6 * GROUP_WIDTH
COL_WK = COL_WQ + WIN_HEADS * HEAD_DIM
COL_WV = COL_WK + WIN_KV_HEADS * HEAD_DIM
IN_COLS = COL_WV + WIN_KV_HEADS * HEAD_DIM
ROPE_RANGES = ((COL_DQ, COL_DV), (COL_WQ, COL_WV))

VMEM_LIMIT = 56 * 1024 * 1024
NORM_ROWS = 16
ROPE_ROWS = 64


def _params(sem, vmem=VMEM_LIMIT):
    return pltpu.CompilerParams(dimension_semantics=sem, vmem_limit_bytes=vmem)


def _sigmoid(x):
    return 1.0 / (1.0 + jnp.exp(-x))


def _row_is_ctx(tile_idx, tm, n_ctx):
    rows = tile_idx * tm + lax.broadcasted_iota(jnp.int32, (tm, 1), 0)
    return rows < n_ctx


def _mod_row(m_ref, k, is_ctx):
    return jnp.where(is_ctx, m_ref[0, k:k + 1, :], m_ref[1, k:k + 1, :])


def _norm_modulate(x, g, m_ref, k_shift, is_ctx):
    ms = jnp.mean(x * x, axis=-1, keepdims=True)
    y = x * lax.rsqrt(ms + EPS) * g
    return y * (1.0 + _mod_row(m_ref, k_shift + 1, is_ctx)) + _mod_row(m_ref, k_shift, is_ctx)


def _adaln_kernel(c_ref, w_ref, b_ref, o_ref):
    cv = c_ref[...]
    s = (cv * _sigmoid(cv)).astype(BF16)
    o_ref[0] = jnp.dot(s, w_ref[0].astype(BF16), preferred_element_type=F32) + b_ref[0]


def _adaln(c_rows, w_mod, b_mod, tn=1024):
    depth, d, n = w_mod.shape
    r = c_rows.shape[0]
    return pl.pallas_call(
        _adaln_kernel,
        grid=(depth, n // tn),
        in_specs=[pl.BlockSpec((r, d), lambda l, j: (0, 0)),
                  pl.BlockSpec((1, d, tn), lambda l, j: (l, 0, j)),
                  pl.BlockSpec((1, 1, tn), lambda l, j: (l, 0, j))],
        out_specs=pl.BlockSpec((1, r, tn), lambda l, j: (l, 0, j)),
        out_shape=jax.ShapeDtypeStruct((depth, r, n), F32),
        compiler_params=_params(("arbitrary", "arbitrary")),
        name="adaln",
    )(c_rows, w_mod, b_mod.reshape(depth, 1, n))


def _is_rope_col(col):
    return any(lo <= col < hi for lo, hi in ROPE_RANGES)


def _inproj_kernel(x_ref, m_ref, g_ref, w_ref, cos_ref, sin_ref, o_ref, xn_ref, *, tm, tn, n_ctx):
    ti = pl.program_id(1)
    j = pl.program_id(2)

    @pl.when(j == 0)
    def _():
        def norm_rows(r, carry):
            rs = pl.ds(pl.multiple_of(r * NORM_ROWS, NORM_ROWS), NORM_ROWS)
            rows = ti * tm + r * NORM_ROWS + lax.broadcasted_iota(jnp.int32, (NORM_ROWS, 1), 0)
            xn_ref[rs, :] = _norm_modulate(x_ref[0, rs, :], g_ref[...], m_ref, 0, rows < n_ctx).astype(BF16)
            return carry
        lax.fori_loop(0, tm // NORM_ROWS, norm_rows, 0)

    o_ref[0] = jnp.dot(xn_ref[...], w_ref[...].astype(BF16), preferred_element_type=F32)
    lane = lax.broadcasted_iota(jnp.int32, (1, LANES), 1)
    first_half = (lane % HEAD_DIM) < (HEAD_DIM // 2)

    for jj in range(IN_COLS // tn):
        chunks = [cc for cc in range(tn // LANES) if _is_rope_col(jj * tn + cc * LANES)]
        if not chunks:
            continue

        @pl.when(j == jj)
        def _(chunks=chunks):
            def rope_rows(r, carry):
                rs = pl.ds(pl.multiple_of(r * ROPE_ROWS, ROPE_ROWS), ROPE_ROWS)
                cos = cos_ref[rs, :]
                sin = sin_ref[rs, :]
                for cc in chunks:
                    cs = slice(cc * LANES, (cc + 1) * LANES)
                    a = o_ref[0, rs, cs]
                    partner = jnp.where(first_half,
                                        pltpu.roll(a, LANES - HEAD_DIM // 2, 1),
                                        pltpu.roll(a, HEAD_DIM // 2, 1))
                    o_ref[0, rs, cs] = a * cos + partner * sin
                return carry
            lax.fori_loop(0, tm // ROPE_ROWS, rope_rows, 0)


def _inproj(h, modsel, g, w, layer, cos_t, sin_t, n_ctx, tm=768, tn=768):
    b, t, d = h.shape
    kern = functools.partial(_inproj_kernel, tm=tm, tn=tn, n_ctx=n_ctx)
    return pl.pallas_call(
        kern,
        grid=(b, t // tm, IN_COLS // tn),
        in_specs=[pl.BlockSpec((1, tm, d), lambda bi, ti, j: (bi, ti, 0)),
                  pl.BlockSpec((None, 2, 6, d), lambda bi, ti, j: (bi, 0, 0, 0)),
                  pl.BlockSpec((1, d), lambda bi, ti, j: (0, 0)),
                  pl.BlockSpec((None, d, tn), lambda bi, ti, j: (layer, 0, j)),
                  pl.BlockSpec((tm, LANES), lambda bi, ti, j: (ti, 0)),
                  pl.BlockSpec((tm, LANES), lambda bi, ti, j: (ti, 0))],
        out_specs=pl.BlockSpec((1, tm, tn), lambda bi, ti, j: (bi, ti, j)),
        out_shape=jax.ShapeDtypeStruct((b, t, IN_COLS), F32),
        scratch_shapes=[pltpu.VMEM((tm, d), BF16)],
        compiler_params=_params(("arbitrary", "arbitrary", "arbitrary")),
        name="inproj",
    )(h, modsel, g.reshape(1, d), w, cos_t, sin_t)


def _rope_tables(n_ctx, n_lat):
    n_rows = n_lat // GRID_W
    rows = jnp.repeat(jnp.arange(n_rows, dtype=F32), GRID_W)
    cols = jnp.tile(jnp.arange(GRID_W, dtype=F32), n_rows)
    n_freq = HEAD_DIM // 4
    inv_freq = ROPE_BASE ** (-jnp.arange(n_freq, dtype=F32) / n_freq)
    ang = jnp.concatenate([rows[:, None] * inv_freq, cols[:, None] * inv_freq], axis=-1)
    cos, sin = jnp.cos(ang), jnp.sin(ang)
    cos_h = jnp.concatenate([cos, cos], axis=-1)
    sin_h = jnp.concatenate([-sin, sin], axis=-1)
    cos_t = jnp.concatenate([jnp.ones((n_ctx, HEAD_DIM), F32), cos_h], axis=0)
    sin_t = jnp.concatenate([jnp.zeros((n_ctx, HEAD_DIM), F32), sin_h], axis=0)
    rep = LANES // HEAD_DIM
    return jnp.tile(cos_t, (1, rep)), jnp.tile(sin_t, (1, rep))


def _pool_kernel(a_ref, w_ref, s_ref, o_ref, pad_ref, *, segments):
    for g, win in enumerate(POOL_WINDOWS):
        half = win // 2
        cs = slice(g * LANES, (g + 1) * LANES)
        wg = w_ref[g].astype(BF16)
        for s0, n in segments:
            a = a_ref[0, s0:s0 + n, cs]
            pad_ref[0:POOL_PAD, :] = jnp.zeros((POOL_PAD, LANES), F32)
            pad_ref[POOL_PAD:POOL_PAD + n, :] = a
            pad_ref[POOL_PAD + n:2 * POOL_PAD + n, :] = jnp.zeros((POOL_PAD, LANES), F32)
            tot = pad_ref[POOL_PAD - half:POOL_PAD - half + n, :]
            for k in range(-half + 1, half):
                tot = tot + pad_ref[POOL_PAD + k:POOL_PAD + k + n, :]
            t = lax.broadcasted_iota(jnp.int32, (n, 1), 0)
            cnt = (jnp.minimum(t + half, n) - jnp.maximum(t - half, 0)).astype(F32)
            dlt = (tot / cnt - a).astype(BF16)
            y = jnp.dot(dlt, wg, preferred_element_type=F32)
            o_ref[0, s0:s0 + n, cs] = y * s_ref[:, cs]


def _pool(p_all, w, scale, segments):
    b, t, _ = p_all.shape
    max_n = max(n for _, n in segments)
    kern = functools.partial(_pool_kernel, segments=segments)
    return pl.pallas_call(
        kern,
        grid=(b,),
        in_specs=[pl.BlockSpec((1, t, GROUP_WIDTH), lambda bi: (bi, 0, COL_POOL // GROUP_WIDTH)),
                  pl.BlockSpec(w.shape, lambda bi: (0, 0, 0)),
                  pl.BlockSpec((1, GROUP_WIDTH), lambda bi: (0, 0))],
        out_specs=pl.BlockSpec((1, t, GROUP_WIDTH), lambda bi: (bi, 0, 0)),
        out_shape=jax.ShapeDtypeStruct((b, t, GROUP_WIDTH), F32),
        scratch_shapes=[pltpu.VMEM((max_n + 2 * POOL_PAD, LANES), F32)],
        compiler_params=_params(("arbitrary",)),
        name="pool",
    )(p_all, w, scale.reshape(1, GROUP_WIDTH))


def _diff_kernel(lam_ref, q_ref, k_ref, v_ref, g_ref, o_ref, kt_ref, vb_ref, *, tq, n_ctx, out_scale):
    ti = pl.program_id(2)
    lam = lam_ref[0]

    @pl.when(ti == 0)
    def _():
        kt_ref[...] = k_ref[0].T.astype(BF16)
        vb_ref[...] = v_ref[0].astype(BF16)

    lane = lax.broadcasted_iota(jnp.int32, (1, 2 * HEAD_DIM), 1)
    q = q_ref[0] * (HEAD_DIM ** -0.5)
    q1 = jnp.where(lane < HEAD_DIM, q, 0.0).astype(BF16)
    q2 = jnp.where(lane >= HEAD_DIM, q, 0.0).astype(BF16)

    def attend(n_keys):
        kt = kt_ref[:, 0:n_keys]
        v = vb_ref[0:n_keys, :]
        s1 = jnp.dot(q1, kt, preferred_element_type=F32)
        s2 = jnp.dot(q2, kt, preferred_element_type=F32)
        e1 = jnp.exp(s1 - jnp.max(s1, axis=-1, keepdims=True))
        e2 = jnp.exp(s2 - jnp.max(s2, axis=-1, keepdims=True))
        r1 = 1.0 / jnp.sum(e1, axis=-1, keepdims=True)
        r2 = lam / jnp.sum(e2, axis=-1, keepdims=True)
        a = (e1 * r1 - e2 * r2).astype(BF16)
        o = jnp.dot(a, v, preferred_element_type=F32)
        ms = jnp.mean(o * o, axis=-1, keepdims=True)
        o_ref[0] = o * lax.rsqrt(ms + EPS) * g_ref[...] * out_scale

    @pl.when(ti * tq < n_ctx)
    def _():
        attend(n_ctx)

    @pl.when(ti * tq >= n_ctx)
    def _():
        attend(k_ref.shape[1])


def _diff(p_all, lam, subln, n_ctx, lam_init, tq=256):
    b, t, _ = p_all.shape
    assert n_ctx % tq == 0 and t % tq == 0
    w = 2 * HEAD_DIM
    kern = functools.partial(_diff_kernel, tq=tq, n_ctx=n_ctx, out_scale=1.0 - lam_init)
    return pl.pallas_call(
        kern,
        grid=(b, DIFF_HEADS, t // tq),
        in_specs=[pl.BlockSpec(memory_space=pltpu.SMEM),
                  pl.BlockSpec((1, tq, w), lambda bi, hi, ti: (bi, ti, COL_DQ // w + hi)),
                  pl.BlockSpec((1, t, w), lambda bi, hi, ti: (bi, 0, COL_DK // w + hi)),
                  pl.BlockSpec((1, t, w), lambda bi, hi, ti: (bi, 0, COL_DV // w + hi)),
                  pl.BlockSpec((1, w), lambda bi, hi, ti: (0, 0))],
        out_specs=pl.BlockSpec((1, tq, w), lambda bi, hi, ti: (bi, ti, hi)),
        out_shape=jax.ShapeDtypeStruct((b, t, GROUP_WIDTH), F32),
        scratch_shapes=[pltpu.VMEM((w, t), BF16), pltpu.VMEM((t, w), BF16)],
        compiler_params=_params(("arbitrary", "arbitrary", "arbitrary")),
        name="diff_attn",
    )(lam.reshape(1), p_all, p_all, p_all, subln.reshape(1, w))


def _gelu_tanh(x):
    return 0.5 * x * (1.0 + jnp.tanh(math.sqrt(2.0 / math.pi) * (x + 0.044715 * (x * x * x))))


def _sgu_kernel(z_ref, g_ref, b_ref, w_ref, bs_ref, o_ref, *, tm):
    z = _gelu_tanh(z_ref[0])
    u = z[:, :GROUP_WIDTH]
    v = z[:, GROUP_WIDTH:]
    mu = jnp.mean(v, axis=-1, keepdims=True)
    var = jnp.mean(jnp.square(v - mu), axis=-1, keepdims=True)
    vn = ((v - mu) * lax.rsqrt(var + EPS) * g_ref[...] + b_ref[...]).astype(BF16)
    for g in range(SGU_GROUPS):
        wg = w_ref[g].astype(BF16)
        bias = bs_ref[:, g:g + 1]
        cs = slice(g * LANES, (g + 1) * LANES)
        for c in range(tm // SGU_CHUNK):
            rs = slice(c * SGU_CHUNK, (c + 1) * SGU_CHUNK)
            sv = jnp.dot(wg, vn[rs, cs], preferred_element_type=F32) + bias
            o_ref[0, rs, cs] = u[rs, cs] * sv


def _sgu(p_all, ln_g, ln_b, w_s, b_s, tm=768):
    b, t, _ = p_all.shape
    zw = 2 * GROUP_WIDTH
    kern = functools.partial(_sgu_kernel, tm=tm)
    return pl.pallas_call(
        kern,
        grid=(b, t // tm),
        in_specs=[pl.BlockSpec((1, tm, zw), lambda bi, ti: (bi, ti, COL_Z // zw)),
                  pl.BlockSpec((1, GROUP_WIDTH), lambda bi, ti: (0, 0)),
                  pl.BlockSpec((1, GROUP_WIDTH), lambda bi, ti: (0, 0)),
                  pl.BlockSpec(w_s.shape, lambda bi, ti: (0, 0, 0)),
                  pl.BlockSpec((SGU_CHUNK, SGU_GROUPS), lambda bi, ti: (0, 0))],
        out_specs=pl.BlockSpec((1, tm, GROUP_WIDTH), lambda bi, ti: (bi, ti, 0)),
        out_shape=jax.ShapeDtypeStruct((b, t, GROUP_WIDTH), F32),
        compiler_params=_params(("arbitrary", "arbitrary")),
        name="sgu",
    )(p_all, ln_g.reshape(1, GROUP_WIDTH), ln_b.reshape(1, GROUP_WIDTH), w_s, b_s.T)


def _win_kernel(sink_ref, q_ref, k_ref, v_ref, o_ref, kt_ref, vv_ref, *, n_ctx):
    g = pl.program_id(1)
    n = pl.program_id(2)
    t = k_ref.shape[1]
    blk = WIN_BLOCK
    band = 3 * blk
    lane = lax.broadcasted_iota(jnp.int32, (1, LANES), 1)
    low = lane < HEAD_DIM

    @pl.when(n == 0)
    def _():
        mine = (lane >= g * HEAD_DIM) & (lane < (g + 1) * HEAD_DIM)
        km = jnp.where(mine, k_ref[0], 0.0)
        vm = jnp.where(mine, v_ref[0], 0.0)
        kt_ref[...] = (km + pltpu.roll(km, HEAD_DIM, 1)).T.astype(BF16)
        vv_ref[...] = (vm + pltpu.roll(vm, HEAD_DIM, 1)).astype(BF16)

    rows = WIN_REP * blk
    row = lax.broadcasted_iota(jnp.int32, (rows, 1), 0)
    head = row // blk
    start = pl.multiple_of(jnp.clip((n - 1) * blk, 0, t - band), blk)
    qpos = n * blk + row % blk
    kpos = start + lax.broadcasted_iota(jnp.int32, (1, band), 1)
    valid = (kpos >= n_ctx) & (qpos >= n_ctx) & (jnp.abs(kpos - qpos) <= WINDOW)

    q = q_ref[0] * (HEAD_DIM ** -0.5)
    pieces = []
    for pair in range(WIN_REP // 2):
        qp = q[:, pair * LANES:(pair + 1) * LANES]
        pieces += [jnp.where(low, qp, 0.0), jnp.where(low, 0.0, qp)]
    qs = jnp.concatenate(pieces, axis=0).astype(BF16)
    s_c = jnp.dot(qs, kt_ref[:, 0:n_ctx], preferred_element_type=F32)
    s_b = jnp.dot(qs, kt_ref[:, pl.ds(start, band)], preferred_element_type=F32)
    s_b = jnp.where(valid, s_b, NEG_INF)
    sink = jnp.full((rows, 1), sink_ref[g * WIN_REP], F32)
    for r in range(1, WIN_REP):
        sink = jnp.where(head == r, sink_ref[g * WIN_REP + r], sink)
    m = jnp.maximum(jnp.maximum(jnp.max(s_c, axis=-1, keepdims=True),
                                jnp.max(s_b, axis=-1, keepdims=True)), sink)
    e_c = jnp.exp(s_c - m)
    e_b = jnp.exp(s_b - m)
    den = jnp.sum(e_c, axis=-1, keepdims=True) + jnp.sum(e_b, axis=-1, keepdims=True) + jnp.exp(sink - m)
    o = (jnp.dot(e_c.astype(BF16), vv_ref[0:n_ctx, :], preferred_element_type=F32)
         + jnp.dot(e_b.astype(BF16), vv_ref[pl.ds(start, band), :], preferred_element_type=F32)) / den
    for pair in range(WIN_REP // 2):
        lo = o[(2 * pair) * blk:(2 * pair + 1) * blk]
        hi = o[(2 * pair + 1) * blk:(2 * pair + 2) * blk]
        o_ref[0, :, pair * LANES:(pair + 1) * LANES] = jnp.where(low, lo, hi)


def _win(p_all, sink, n_ctx):
    b, t, _ = p_all.shape
    qw = WIN_REP * HEAD_DIM
    kern = functools.partial(_win_kernel, n_ctx=n_ctx)
    return pl.pallas_call(
        kern,
        grid=(b, WIN_KV_HEADS, t // WIN_BLOCK),
        in_specs=[pl.BlockSpec(memory_space=pltpu.SMEM),
                  pl.BlockSpec((1, WIN_BLOCK, qw), lambda bi, gi, ni: (bi, ni, COL_WQ // qw + gi)),
                  pl.BlockSpec((1, t, LANES), lambda bi, gi, ni: (bi, 0, COL_WK // LANES)),
                  pl.BlockSpec((1, t, LANES), lambda bi, gi, ni: (bi, 0, COL_WV // LANES))],
        out_specs=pl.BlockSpec((1, WIN_BLOCK, qw), lambda bi, gi, ni: (bi, ni, gi)),
        out_shape=jax.ShapeDtypeStruct((b, t, GROUP_WIDTH), F32),
        scratch_shapes=[pltpu.VMEM((LANES, t), BF16), pltpu.VMEM((t, LANES), BF16)],
        compiler_params=_params(("arbitrary", "arbitrary", "arbitrary")),
        name="win_attn",
    )(sink, p_all, p_all, p_all)


def _outproj_kernel(a_ref, b_ref, c_ref, d_ref, w_ref, h_ref, m_ref, o_ref, mix_ref, *, tm, n_ctx):
    ti = pl.program_id(1)

    @pl.when(pl.program_id(2) == 0)
    def _():
        for i, r in enumerate((a_ref, b_ref, c_ref, d_ref)):
            mix_ref[:, i * GROUP_WIDTH:(i + 1) * GROUP_WIDTH] = r[0].astype(BF16)

    y = jnp.dot(mix_ref[...], w_ref[...].astype(BF16), preferred_element_type=F32)
    gate = _mod_row(m_ref, 2, _row_is_ctx(ti, tm, n_ctx))
    o_ref[0] = h_ref[0] + gate * y


def _outproj(mixes, w, layer, h, modsel, n_ctx, tm=768, tn=512):
    b, t, d = h.shape
    kern = functools.partial(_outproj_kernel, tm=tm, n_ctx=n_ctx)
    mix_spec = pl.BlockSpec((1, tm, GROUP_WIDTH), lambda bi, ti, j: (bi, ti, 0))
    return pl.pallas_call(
        kern,
        grid=(b, t // tm, d // tn),
        in_specs=[mix_spec, mix_spec, mix_spec, mix_spec,
                  pl.BlockSpec((None, 4 * GROUP_WIDTH, tn), lambda bi, ti, j: (layer, 0, j)),
                  pl.BlockSpec((1, tm, tn), lambda bi, ti, j: (bi, ti, j)),
                  pl.BlockSpec((None, 2, 6, tn), lambda bi, ti, j: (bi, 0, 0, j))],
        out_specs=pl.BlockSpec((1, tm, tn), lambda bi, ti, j: (bi, ti, j)),
        out_shape=jax.ShapeDtypeStruct((b, t, d), F32),
        scratch_shapes=[pltpu.VMEM((tm, 4 * GROUP_WIDTH), BF16)],
        compiler_params=_params(("arbitrary", "arbitrary", "arbitrary")),
        name="outproj",
    )(*mixes, w, h, modsel)


def _ffn_in_kernel(x_ref, m_ref, g_ref, o_ref, *, tm, n_ctx, row0):
    first_row = row0 + pl.program_id(1) * tm

    def norm_rows(r, carry):
        rs = pl.ds(pl.multiple_of(r * NORM_ROWS, NORM_ROWS), NORM_ROWS)
        rows = first_row + r * NORM_ROWS + lax.broadcasted_iota(jnp.int32, (NORM_ROWS, 1), 0)
        o_ref[0, rs, :] = _norm_modulate(x_ref[0, rs, :], g_ref[...], m_ref, 3, rows < n_ctx)
        return carry
    lax.fori_loop(0, tm // NORM_ROWS, norm_rows, 0)


def _route_kernel(x_ref, m_ref, g_ref, r_ref, o_ref, gate_ref, sel_ref, *, tm, n_ctx, row0):
    _ffn_in_kernel(x_ref, m_ref, g_ref, o_ref, tm=tm, n_ctx=n_ctx, row0=row0)
    logits = jnp.dot(o_ref[0], r_ref[...], preferred_element_type=F32, precision=lax.Precision.HIGHEST)
    lane = lax.broadcasted_iota(jnp.int32, (1, LANES), 1)
    logits = jnp.where(lane < N_EXPERTS, logits, -jnp.inf)
    m1 = jnp.max(logits, axis=-1, keepdims=True)
    i1 = jnp.min(jnp.where(logits == m1, lane, LANES), axis=-1, keepdims=True)
    rest = jnp.where(lane == i1, -jnp.inf, logits)
    m2 = jnp.max(rest, axis=-1, keepdims=True)
    i2 = jnp.min(jnp.where(rest == m2, lane, LANES), axis=-1, keepdims=True)
    e2 = jnp.exp(m2 - m1)
    w1 = 1.0 / (1.0 + e2)
    w2 = e2 / (1.0 + e2)
    gate_ref[0] = jnp.where(lane == i1, w1, 0.0) + jnp.where(lane == i2, w2, 0.0)
    sel_ref[0] = ((lane == i1) | (lane == i2)).astype(jnp.int32)


def _ffn_in(h, modsel, g, n_ctx, row0, router=None, tm=256):
    b, t, d = h.shape
    nt = (t - row0) // tm
    off = row0 // tm
    in_specs = [pl.BlockSpec((1, tm, d), lambda bi, ti: (bi, ti + off, 0)),
                pl.BlockSpec((None, 2, 6, d), lambda bi, ti: (bi, 0, 0, 0)),
                pl.BlockSpec((1, d), lambda bi, ti: (0, 0))]
    f_spec = pl.BlockSpec((1, tm, d), lambda bi, ti: (bi, ti, 0))
    f_shape = jax.ShapeDtypeStruct((b, t - row0, d), F32)
    if router is None:
        kern = functools.partial(_ffn_in_kernel, tm=tm, n_ctx=n_ctx, row0=row0)
        return pl.pallas_call(
            kern, grid=(b, nt), in_specs=in_specs, out_specs=f_spec, out_shape=f_shape,
            compiler_params=_params(("arbitrary", "arbitrary")), name="ffn_in",
        )(h, modsel, g.reshape(1, d))
    r_pad = jnp.zeros((d, LANES), F32).at[:, :N_EXPERTS].set(router)
    e_spec = pl.BlockSpec((1, tm, LANES), lambda bi, ti: (bi, ti, 0))
    kern = functools.partial(_route_kernel, tm=tm, n_ctx=n_ctx, row0=row0)
    return pl.pallas_call(
        kern, grid=(b, nt),
        in_specs=in_specs + [pl.BlockSpec((d, LANES), lambda bi, ti: (0, 0))],
        out_specs=[f_spec, e_spec, e_spec],
        out_shape=[f_shape, jax.ShapeDtypeStruct((b, t - row0, LANES), F32),
                   jax.ShapeDtypeStruct((b, t - row0, LANES), jnp.int32)],
        compiler_params=_params(("arbitrary", "arbitrary")), name="ffn_in_route",
    )(h, modsel, g.reshape(1, d), r_pad)


FFN_TM = 1024
FFN_SUB = 256


def _ffn_kernel(te_ref, tr_ref, x_ref, w1_ref, w3_ref, w2_ref, o_ref, xs_ref):
    i = pl.program_id(0)
    j = pl.program_id(1)
    n_blocks = (tr_ref[i] + FFN_SUB - 1) // FFN_SUB

    @pl.when(j == 0)
    def _():
        o_ref[...] = jnp.zeros_like(o_ref)
        xs_ref[...] = x_ref[...].astype(BF16)

    for nb in range(1, FFN_TM // FFN_SUB + 1):
        @pl.when(n_blocks == nb)
        def _(nb=nb):
            m = nb * FFN_SUB
            xs = xs_ref[0:m, :]
            h1 = jnp.dot(xs, w1_ref[0].astype(BF16), preferred_element_type=F32)
            h3 = jnp.dot(xs, w3_ref[0].astype(BF16), preferred_element_type=F32)
            act = (h1 * _sigmoid(h1) * h3).astype(BF16)
            o_ref[0:m, :] += jnp.dot(act, w2_ref[0].astype(BF16), preferred_element_type=F32)


def _ffn(x_rows, w1, w3, w2, tile_expert, tile_rows, tf=256):
    nr, d = x_rows.shape
    n_exp, _, f = w1.shape
    nj = f // tf

    def wcol(i, j, te, tr):
        return (te[i], 0, jnp.where(tr[i] > 0, j, nj - 1))

    def wrow(i, j, te, tr):
        return (te[i], jnp.where(tr[i] > 0, j, nj - 1), 0)

    return pl.pallas_call(
        _ffn_kernel,
        grid_spec=pltpu.PrefetchScalarGridSpec(
            num_scalar_prefetch=2,
            grid=(nr // FFN_TM, nj),
            in_specs=[pl.BlockSpec((FFN_TM, d), lambda i, j, te, tr: (i, 0)),
                      pl.BlockSpec((1, d, tf), wcol),
                      pl.BlockSpec((1, d, tf), wcol),
                      pl.BlockSpec((1, tf, d), wrow)],
            out_specs=pl.BlockSpec((FFN_TM, d), lambda i, j, te, tr: (i, 0)),
            scratch_shapes=[pltpu.VMEM((FFN_TM, d), BF16)]),
        out_shape=jax.ShapeDtypeStruct((nr, d), F32),
        compiler_params=_params(("arbitrary", "arbitrary")),
        name="ffn",
    )(tile_expert, tile_rows, x_rows, w1, w3, w2)


def _residual_kernel(h_ref, y_ref, m_ref, o_ref, *, tm, n_ctx):
    gate = _mod_row(m_ref, 5, _row_is_ctx(pl.program_id(1), tm, n_ctx))
    o_ref[0] = h_ref[0] + gate * y_ref[0]


def _residual(h, y, modsel, n_ctx, tm=256):
    b, t, d = h.shape
    spec = pl.BlockSpec((1, tm, d), lambda bi, ti: (bi, ti, 0))
    kern = functools.partial(_residual_kernel, tm=tm, n_ctx=n_ctx)
    return pl.pallas_call(
        kern, grid=(b, t // tm),
        in_specs=[spec, spec, pl.BlockSpec((None, 2, 6, d), lambda bi, ti: (bi, 0, 0, 0))],
        out_specs=spec, out_shape=jax.ShapeDtypeStruct((b, t, d), F32),
        compiler_params=_params(("arbitrary", "arbitrary")), name="residual",
    )(h, y, modsel)


def _moe_final_kernel(h_ref, y0_ref, y1_ref, w0_ref, w1_ref, m_ref, g_ref, o_ref):
    y = w0_ref[0] * y0_ref[0] + w1_ref[0] * y1_ref[0]
    hn = h_ref[0] + m_ref[1, 5:6, :] * y
    ms = jnp.mean(hn * hn, axis=-1, keepdims=True)
    o_ref[0] = hn * lax.rsqrt(ms + EPS) * g_ref[...]


def _moe_final(h, y0, y1, w0, w1, modsel, g, row0, tm=256):
    b, t, d = h.shape
    n_lat = t - row0
    off = row0 // tm
    spec = pl.BlockSpec((1, tm, d), lambda bi, ti: (bi, ti, 0))
    wspec = pl.BlockSpec((1, tm, 1), lambda bi, ti: (bi, ti, 0))
    return pl.pallas_call(
        _moe_final_kernel, grid=(b, n_lat // tm),
        in_specs=[pl.BlockSpec((1, tm, d), lambda bi, ti: (bi, ti + off, 0)), spec, spec, wspec, wspec,
                  pl.BlockSpec((None, 2, 6, d), lambda bi, ti: (bi, 0, 0, 0)),
                  pl.BlockSpec((1, d), lambda bi, ti: (0, 0))],
        out_specs=spec, out_shape=jax.ShapeDtypeStruct((b, n_lat, d), F32),
        compiler_params=_params(("arbitrary", "arbitrary")), name="moe_final",
    )(h, y0, y1, w0, w1, modsel, g.reshape(1, d))


def _routing_tables(sel, gates, n_tiles):
    n, n_exp = sel.shape
    seli = sel.astype(jnp.int32)
    counts = jnp.sum(seli, axis=0)
    rank = jnp.cumsum(seli, axis=0) - seli
    tiles_e = (counts + FFN_TM - 1) // FFN_TM
    tile_end = jnp.cumsum(tiles_e)
    tile_start = tile_end - tiles_e
    pos = tile_start[None, :] * FFN_TM + rank
    n_rows = n_tiles * FFN_TM
    expert = jnp.arange(n_exp, dtype=jnp.int32)[None, :]
    e_lo = jnp.min(jnp.where(sel, expert, n_exp), axis=1, keepdims=True)
    e_hi = jnp.max(jnp.where(sel, expert, -1), axis=1, keepdims=True)
    pick = lambda a, e: jnp.sum(jnp.where(expert == e, a, 0), axis=1)
    pos0, pos1 = pick(pos, e_lo), pick(pos, e_hi)
    w0, w1 = pick(gates, e_lo), pick(gates, e_hi)
    tok = jnp.arange(n, dtype=jnp.int32)
    src = (jnp.arange(n_rows, dtype=jnp.int32) % n).at[jnp.concatenate([pos0, pos1])].set(
        jnp.concatenate([tok, tok]), mode="promise_in_bounds", unique_indices=True)

    tile = jnp.arange(n_tiles, dtype=jnp.int32)
    te = jnp.minimum(jnp.sum((tile_end[None, :] <= tile[:, None]).astype(jnp.int32), axis=1), n_exp - 1)
    tr = jnp.clip(counts[te] - (tile - tile_start[te]) * FFN_TM, 0, FFN_TM)
    tr = jnp.where(tile < tile_end[-1], tr, 0)
    return src, pos0, pos1, w0, w1, te.astype(jnp.int32), tr.astype(jnp.int32)


def kernel(x, c, ctx, c_ctx, w_mod, b_mod, norm_mix, norm_ffn, w_in, w_out, pool_w, pool_scale, diff_lam,
           diff_subln, sgu_ln_g, sgu_ln_b, sgu_w, sgu_b, win_sink, ffn_w1, ffn_w3, ffn_w2, moe_router,
           moe_w1, moe_w3, moe_w2, norm_final):
    b, n_lat, d = x.shape
    n_ctx = ctx.shape[1]
    t = n_ctx + n_lat
    depth = w_mod.shape[0]
    segments = ((0, n_ctx), (n_ctx, n_lat))

    c_rows = jnp.zeros((8, d), F32).at[:b].set(c).at[b].set(c_ctx)
    mod = _adaln(c_rows, w_mod, b_mod).reshape(depth, 8, 6, d)
    cos_t, sin_t = _rope_tables(n_ctx, n_lat)
    h = jnp.concatenate([ctx, x], axis=1)

    out = None
    for l in range(depth):
        last = l == depth - 1
        lat_m = mod[l, :b]
        ctx_m = jnp.broadcast_to(mod[l, b:b + 1], lat_m.shape)
        modsel = jnp.stack([ctx_m, lat_m], axis=1)

        p_all = _inproj(h, modsel, norm_mix[l], w_in, l, cos_t, sin_t, n_ctx)

        lam_init = 0.8 - 0.6 * math.exp(-0.3 * l)
        lq1, lk1, lq2, lk2 = diff_lam[l].astype(F32)
        lam = jnp.exp(jnp.sum(lq1 * lk1)) - jnp.exp(jnp.sum(lq2 * lk2)) + lam_init

        mixes = (_pool(p_all, pool_w[l], pool_scale[l], segments),
                 _diff(p_all, lam, diff_subln[l], n_ctx, lam_init),
                 _sgu(p_all, sgu_ln_g[l], sgu_ln_b[l], sgu_w[l], sgu_b[l]),
                 _win(p_all, win_sink[l], n_ctx))
        h = _outproj(mixes, w_out, l, h, modsel, n_ctx)

        i = l // 2
        row0 = n_ctx if last else 0
        rows = b * (t - row0)
        if l % 2 == 0:
            f = _ffn_in(h, modsel, norm_ffn[l], n_ctx, row0).reshape(rows, d)
            n_tiles = rows // FFN_TM
            te = jnp.zeros((n_tiles,), jnp.int32)
            tr = jnp.full((n_tiles,), FFN_TM, jnp.int32)
            y = _ffn(f, ffn_w1[i:i + 1], ffn_w3[i:i + 1], ffn_w2[i:i + 1], te, tr)
            assert not last
            h = _residual(h, y.reshape(b, t - row0, d), modsel, n_ctx)
        else:
            assert last
            f, gates, sel = _ffn_in(h, modsel, norm_ffn[l], n_ctx, row0, router=moe_router[i])
            f = f.reshape(rows, d)
            n_exp = moe_w1.shape[1]
            n_tiles = (2 * rows) // FFN_TM + n_exp
            src, pos0, pos1, w0, w1, te, tr = _routing_tables(
                sel.reshape(rows, LANES)[:, :n_exp] > 0, gates.reshape(rows, LANES)[:, :n_exp], n_tiles)
            rows_of = lambda a, idx: a.at[idx].get(mode="promise_in_bounds")
            ys = _ffn(rows_of(f, src), moe_w1[i], moe_w3[i], moe_w2[i], te, tr)
            shp = (b, t - row0, d)
            out = _moe_final(h, rows_of(ys, pos0).reshape(shp), rows_of(ys, pos1).reshape(shp),
                             w0.reshape(b, t - row0, 1), w1.reshape(b, t - row0, 1), modsel, norm_final, row0)
    return out
```

```python
import functools
import math

import jax
import jax.numpy as jnp
from jax import lax
from jax.experimental import pallas as pl
from jax.experimental.pallas import tpu as pltpu

F32 = jnp.float32
BF16 = jnp.bfloat16

GRID_W = 64
GROUP_WIDTH = 512
HEAD_DIM = 64
ROPE_BASE = 10000.0
EPS = 1e-6
NEG_INF = -1e30
LANES = 128

POOL_WINDOWS = (2, 4, 8, 16)
POOL_PAD = 16
DIFF_HEADS = GROUP_WIDTH // (2 * HEAD_DIM)
SGU_CHUNK = 128
SGU_GROUPS = 4
WIN_HEADS = GROUP_WIDTH // HEAD_DIM
WIN_KV_HEADS = 2
WIN_REP = WIN_HEADS // WIN_KV_HEADS
WINDOW = 128
WIN_BLOCK = 128
N_EXPERTS = 8

COL_POOL = 0
COL_DQ = GROUP_WIDTH
COL_DK = 2 * GROUP_WIDTH
COL_DV = 3 * GROUP_WIDTH
COL_Z = 4 * GROUP_WIDTH
COL_WQ = 6 * GROUP_WIDTH
COL_WK = COL_WQ + WIN_HEADS * HEAD_DIM
COL_WV = COL_WK + WIN_KV_HEADS * HEAD_DIM
IN_COLS = COL_WV + WIN_KV_HEADS * HEAD_DIM
ROPE_RANGES = ((COL_DQ, COL_DV), (COL_WQ, COL_WV))

VMEM_LIMIT = 56 * 1024 * 1024
NORM_ROWS = 32
ROPE_ROWS = 64


def _params(sem, vmem=VMEM_LIMIT):
    return pltpu.CompilerParams(dimension_semantics=sem, vmem_limit_bytes=vmem)


def _sigmoid(x):
    return 1.0 / (1.0 + jnp.exp(-x))


def _row_is_ctx(tile_idx, tm, n_ctx):
    rows = tile_idx * tm + lax.broadcasted_iota(jnp.int32, (tm, 1), 0)
    return rows < n_ctx


def _mod_row(m_ref, k, is_ctx):
    return jnp.where(is_ctx, m_ref[0, k:k + 1, :], m_ref[1, k:k + 1, :])


def _norm_modulate_rows(x_ref, g_ref, m_ref, k_shift, first_row, tm, n_ctx, store):
    def body(r, carry):
        r0 = pl.multiple_of(r * NORM_ROWS, NORM_ROWS)
        which = ((first_row + r0) >= n_ctx).astype(jnp.int32)
        rs = pl.ds(r0, NORM_ROWS)
        x = x_ref[0, rs, :]
        ms = jnp.mean(x * x, axis=-1, keepdims=True)
        a = g_ref[...] * (1.0 + m_ref[which, k_shift + 1:k_shift + 2, :])
        store(rs, x * lax.rsqrt(ms + EPS) * a + m_ref[which, k_shift:k_shift + 1, :])
        return carry
    lax.fori_loop(0, tm // NORM_ROWS, body, 0, unroll=2)


def _adaln_kernel(c_ref, w_ref, b_ref, o_ref):
    cv = c_ref[...]
    s = (cv * _sigmoid(cv)).astype(BF16)
    o_ref[0] = jnp.dot(s, w_ref[0].astype(BF16), preferred_element_type=F32) + b_ref[0]


def _adaln(c_rows, w_mod, b_mod, tn=1024):
    depth, d, n = w_mod.shape
    r = c_rows.shape[0]
    return pl.pallas_call(
        _adaln_kernel,
        grid=(depth, n // tn),
        in_specs=[pl.BlockSpec((r, d), lambda l, j: (0, 0)),
                  pl.BlockSpec((1, d, tn), lambda l, j: (l, 0, j)),
                  pl.BlockSpec((1, 1, tn), lambda l, j: (l, 0, j))],
        out_specs=pl.BlockSpec((1, r, tn), lambda l, j: (l, 0, j)),
        out_shape=jax.ShapeDtypeStruct((depth, r, n), F32),
        compiler_params=_params(("arbitrary", "arbitrary")),
        name="adaln",
    )(c_rows, w_mod, b_mod.reshape(depth, 1, n))


def _is_rope_col(col):
    return any(lo <= col < hi for lo, hi in ROPE_RANGES)


def _inproj_kernel(x_ref, m_ref, g_ref, w_ref, cos_ref, sin_ref, o_ref, xn_ref, *, tm, tn, n_ctx):
    ti = pl.program_id(1)
    j = pl.program_id(2)

    @pl.when(j == 0)
    def _():
        def store(rs, y):
            xn_ref[rs, :] = y.astype(BF16)
        _norm_modulate_rows(x_ref, g_ref, m_ref, 0, ti * tm, tm, n_ctx, store)

    o_ref[0] = jnp.dot(xn_ref[...], w_ref[...].astype(BF16), preferred_element_type=F32)
    lane = lax.broadcasted_iota(jnp.int32, (1, LANES), 1)
    first_half = (lane % HEAD_DIM) < (HEAD_DIM // 2)

    for jj in range(IN_COLS // tn):
        chunks = [cc for cc in range(tn // LANES) if _is_rope_col(jj * tn + cc * LANES)]
        if not chunks:
            continue

        @pl.when(j == jj)
        def _(chunks=chunks):
            def rope_rows(r, carry):
                rs = pl.ds(pl.multiple_of(r * ROPE_ROWS, ROPE_ROWS), ROPE_ROWS)
                cos = cos_ref[rs, :]
                sin = sin_ref[rs, :]
                for cc in chunks:
                    cs = slice(cc * LANES, (cc + 1) * LANES)
                    a = o_ref[0, rs, cs]
                    partner = jnp.where(first_half,
                                        pltpu.roll(a, LANES - HEAD_DIM // 2, 1),
                                        pltpu.roll(a, HEAD_DIM // 2, 1))
                    o_ref[0, rs, cs] = a * cos + partner * sin
                return carry
            lax.fori_loop(0, tm // ROPE_ROWS, rope_rows, 0, unroll=2)


def _inproj(h, modsel, g, w, layer, cos_t, sin_t, n_ctx, tm=768, tn=768):
    b, t, d = h.shape
    kern = functools.partial(_inproj_kernel, tm=tm, tn=tn, n_ctx=n_ctx)
    return pl.pallas_call(
        kern,
        grid=(b, t // tm, IN_COLS // tn),
        in_specs=[pl.BlockSpec((1, tm, d), lambda bi, ti, j: (bi, ti, 0)),
                  pl.BlockSpec((None, 2, 6, d), lambda bi, ti, j: (bi, 0, 0, 0)),
                  pl.BlockSpec((1, d), lambda bi, ti, j: (0, 0)),
                  pl.BlockSpec((None, d, tn), lambda bi, ti, j: (layer, 0, j)),
                  pl.BlockSpec((tm, LANES), lambda bi, ti, j: (ti, 0)),
                  pl.BlockSpec((tm, LANES), lambda bi, ti, j: (ti, 0))],
        out_specs=pl.BlockSpec((1, tm, tn), lambda bi, ti, j: (bi, ti, j)),
        out_shape=jax.ShapeDtypeStruct((b, t, IN_COLS), F32),
        scratch_shapes=[pltpu.VMEM((tm, d), BF16)],
        compiler_params=_params(("arbitrary", "arbitrary", "arbitrary")),
        name="inproj",
    )(h, modsel, g.reshape(1, d), w, cos_t, sin_t)


def _rope_tables(n_ctx, n_lat):
    n_rows = n_lat // GRID_W
    rows = jnp.repeat(jnp.arange(n_rows, dtype=F32), GRID_W)
    cols = jnp.tile(jnp.arange(GRID_W, dtype=F32), n_rows)
    n_freq = HEAD_DIM // 4
    inv_freq = ROPE_BASE ** (-jnp.arange(n_freq, dtype=F32) / n_freq)
    ang = jnp.concatenate([rows[:, None] * inv_freq, cols[:, None] * inv_freq], axis=-1)
    cos, sin = jnp.cos(ang), jnp.sin(ang)
    cos_h = jnp.concatenate([cos, cos], axis=-1)
    sin_h = jnp.concatenate([-sin, sin], axis=-1)
    cos_t = jnp.concatenate([jnp.ones((n_ctx, HEAD_DIM), F32), cos_h], axis=0)
    sin_t = jnp.concatenate([jnp.zeros((n_ctx, HEAD_DIM), F32), sin_h], axis=0)
    rep = LANES // HEAD_DIM
    return jnp.tile(cos_t, (1, rep)), jnp.tile(sin_t, (1, rep))


def _pool_kernel(a_ref, w_ref, s_ref, o_ref, pad_ref, *, segments):
    for g, win in enumerate(POOL_WINDOWS):
        half = win // 2
        cs = slice(g * LANES, (g + 1) * LANES)
        wg = w_ref[g].astype(BF16)
        for s0, n in segments:
            a = a_ref[0, s0:s0 + n, cs]
            pad_ref[0:POOL_PAD, :] = jnp.zeros((POOL_PAD, LANES), F32)
            pad_ref[POOL_PAD:POOL_PAD + n, :] = a
            pad_ref[POOL_PAD + n:2 * POOL_PAD + n, :] = jnp.zeros((POOL_PAD, LANES), F32)
            tot = pad_ref[POOL_PAD - half:POOL_PAD - half + n, :]
            for k in range(-half + 1, half):
                tot = tot + pad_ref[POOL_PAD + k:POOL_PAD + k + n, :]
            t = lax.broadcasted_iota(jnp.int32, (n, 1), 0)
            cnt = (jnp.minimum(t + half, n) - jnp.maximum(t - half, 0)).astype(F32)
            dlt = (tot / cnt - a).astype(BF16)
            y = jnp.dot(dlt, wg, preferred_element_type=F32)
            o_ref[0, s0:s0 + n, cs] = (y * s_ref[:, cs]).astype(BF16)


def _pool(p_all, w, scale, segments):
    b, t, _ = p_all.shape
    max_n = max(n for _, n in segments)
    kern = functools.partial(_pool_kernel, segments=segments)
    return pl.pallas_call(
        kern,
        grid=(b,),
        in_specs=[pl.BlockSpec((1, t, GROUP_WIDTH), lambda bi: (bi, 0, COL_POOL // GROUP_WIDTH)),
                  pl.BlockSpec(w.shape, lambda bi: (0, 0, 0)),
                  pl.BlockSpec((1, GROUP_WIDTH), lambda bi: (0, 0))],
        out_specs=pl.BlockSpec((1, t, GROUP_WIDTH), lambda bi: (bi, 0, 0)),
        out_shape=jax.ShapeDtypeStruct((b, t, GROUP_WIDTH), BF16),
        scratch_shapes=[pltpu.VMEM((max_n + 2 * POOL_PAD, LANES), F32)],
        compiler_params=_params(("arbitrary",)),
        name="pool",
    )(p_all, w, scale.reshape(1, GROUP_WIDTH))


def _diff_kernel(lam_ref, q_ref, k_ref, v_ref, g_ref, o_ref, kt_ref, vb_ref, *, tq, n_ctx, out_scale):
    ti = pl.program_id(2)
    lam = lam_ref[0]

    @pl.when(ti == 0)
    def _():
        kt_ref[...] = k_ref[0].T.astype(BF16)
        vb_ref[...] = v_ref[0].astype(BF16)

    lane = lax.broadcasted_iota(jnp.int32, (1, 2 * HEAD_DIM), 1)
    q = q_ref[0] * (HEAD_DIM ** -0.5)
    q1 = jnp.where(lane < HEAD_DIM, q, 0.0).astype(BF16)
    q2 = jnp.where(lane >= HEAD_DIM, q, 0.0).astype(BF16)

    def attend(n_keys):
        kt = kt_ref[:, 0:n_keys]
        v = vb_ref[0:n_keys, :]
        s1 = jnp.dot(q1, kt, preferred_element_type=F32)
        s2 = jnp.dot(q2, kt, preferred_element_type=F32)
        e1 = jnp.exp(s1 - jnp.max(s1, axis=-1, keepdims=True))
        e2 = jnp.exp(s2 - jnp.max(s2, axis=-1, keepdims=True))
        r1 = 1.0 / jnp.sum(e1, axis=-1, keepdims=True)
        r2 = lam / jnp.sum(e2, axis=-1, keepdims=True)
        a = (e1 * r1 - e2 * r2).astype(BF16)
        o = jnp.dot(a, v, preferred_element_type=F32)
        ms = jnp.mean(o * o, axis=-1, keepdims=True)
        o_ref[0] = (o * lax.rsqrt(ms + EPS) * g_ref[...] * out_scale).astype(BF16)

    @pl.when(ti * tq < n_ctx)
    def _():
        attend(n_ctx)

    @pl.when(ti * tq >= n_ctx)
    def _():
        attend(k_ref.shape[1])


def _diff(p_all, lam, subln, n_ctx, lam_init, tq=256):
    b, t, _ = p_all.shape
    assert n_ctx % tq == 0 and t % tq == 0
    w = 2 * HEAD_DIM
    kern = functools.partial(_diff_kernel, tq=tq, n_ctx=n_ctx, out_scale=1.0 - lam_init)
    return pl.pallas_call(
        kern,
        grid=(b, DIFF_HEADS, t // tq),
        in_specs=[pl.BlockSpec(memory_space=pltpu.SMEM),
                  pl.BlockSpec((1, tq, w), lambda bi, hi, ti: (bi, ti, COL_DQ // w + hi)),
                  pl.BlockSpec((1, t, w), lambda bi, hi, ti: (bi, 0, COL_DK // w + hi)),
                  pl.BlockSpec((1, t, w), lambda bi, hi, ti: (bi, 0, COL_DV // w + hi)),
                  pl.BlockSpec((1, w), lambda bi, hi, ti: (0, 0))],
        out_specs=pl.BlockSpec((1, tq, w), lambda bi, hi, ti: (bi, ti, hi)),
        out_shape=jax.ShapeDtypeStruct((b, t, GROUP_WIDTH), BF16),
        scratch_shapes=[pltpu.VMEM((w, t), BF16), pltpu.VMEM((t, w), BF16)],
        compiler_params=_params(("arbitrary", "arbitrary", "arbitrary")),
        name="diff_attn",
    )(lam.reshape(1), p_all, p_all, p_all, subln.reshape(1, w))


def _gelu_tanh(x):
    return 0.5 * x * (1.0 + jnp.tanh(math.sqrt(2.0 / math.pi) * (x + 0.044715 * (x * x * x))))


def _sgu_kernel(z_ref, g_ref, b_ref, w_ref, bs_ref, o_ref, *, tm):
    z = _gelu_tanh(z_ref[0])
    u = z[:, :GROUP_WIDTH]
    v = z[:, GROUP_WIDTH:]
    mu = jnp.mean(v, axis=-1, keepdims=True)
    var = jnp.mean(jnp.square(v - mu), axis=-1, keepdims=True)
    vn = ((v - mu) * lax.rsqrt(var + EPS) * g_ref[...] + b_ref[...]).astype(BF16)
    for g in range(SGU_GROUPS):
        wg = w_ref[g].astype(BF16)
        bias = bs_ref[:, g:g + 1]
        cs = slice(g * LANES, (g + 1) * LANES)
        for c in range(tm // SGU_CHUNK):
            rs = slice(c * SGU_CHUNK, (c + 1) * SGU_CHUNK)
            sv = jnp.dot(wg, vn[rs, cs], preferred_element_type=F32) + bias
            o_ref[0, rs, cs] = (u[rs, cs] * sv).astype(BF16)


def _sgu(p_all, ln_g, ln_b, w_s, b_s, tm=768):
    b, t, _ = p_all.shape
    zw = 2 * GROUP_WIDTH
    kern = functools.partial(_sgu_kernel, tm=tm)
    return pl.pallas_call(
        kern,
        grid=(b, t // tm),
        in_specs=[pl.BlockSpec((1, tm, zw), lambda bi, ti: (bi, ti, COL_Z // zw)),
                  pl.BlockSpec((1, GROUP_WIDTH), lambda bi, ti: (0, 0)),
                  pl.BlockSpec((1, GROUP_WIDTH), lambda bi, ti: (0, 0)),
                  pl.BlockSpec(w_s.shape, lambda bi, ti: (0, 0, 0)),
                  pl.BlockSpec((SGU_CHUNK, SGU_GROUPS), lambda bi, ti: (0, 0))],
        out_specs=pl.BlockSpec((1, tm, GROUP_WIDTH), lambda bi, ti: (bi, ti, 0)),
        out_shape=jax.ShapeDtypeStruct((b, t, GROUP_WIDTH), BF16),
        compiler_params=_params(("arbitrary", "arbitrary")),
        name="sgu",
    )(p_all, ln_g.reshape(1, GROUP_WIDTH), ln_b.reshape(1, GROUP_WIDTH), w_s, b_s.T)


def _win_kernel(sink_ref, q_ref, k_ref, v_ref, o_ref, kt_ref, vv_ref, *, n_ctx):
    g = pl.program_id(1)
    n = pl.program_id(2)
    t = k_ref.shape[1]
    blk = WIN_BLOCK
    band = 3 * blk
    lane = lax.broadcasted_iota(jnp.int32, (1, LANES), 1)
    low = lane < HEAD_DIM

    @pl.when(n == 0)
    def _():
        mine = (lane >= g * HEAD_DIM) & (lane < (g + 1) * HEAD_DIM)
        km = jnp.where(mine, k_ref[0], 0.0)
        vm = jnp.where(mine, v_ref[0], 0.0)
        kt_ref[...] = (km + pltpu.roll(km, HEAD_DIM, 1)).T.astype(BF16)
        vv_ref[...] = (vm + pltpu.roll(vm, HEAD_DIM, 1)).astype(BF16)

    rows = WIN_REP * blk
    row = lax.broadcasted_iota(jnp.int32, (rows, 1), 0)
    head = row // blk
    start = pl.multiple_of(jnp.clip((n - 1) * blk, 0, t - band), blk)
    qpos = n * blk + row % blk
    kpos = start + lax.broadcasted_iota(jnp.int32, (1, band), 1)
    valid = (kpos >= n_ctx) & (qpos >= n_ctx) & (jnp.abs(kpos - qpos) <= WINDOW)

    q = q_ref[0] * (HEAD_DIM ** -0.5)
    pieces = []
    for pair in range(WIN_REP // 2):
        qp = q[:, pair * LANES:(pair + 1) * LANES]
        pieces += [jnp.where(low, qp, 0.0), jnp.where(low, 0.0, qp)]
    qs = jnp.concatenate(pieces, axis=0).astype(BF16)
    s_c = jnp.dot(qs, kt_ref[:, 0:n_ctx], preferred_element_type=F32)
    s_b = jnp.dot(qs, kt_ref[:, pl.ds(start, band)], preferred_element_type=F32)
    s_b = jnp.where(valid, s_b, NEG_INF)
    sink = jnp.full((rows, 1), sink_ref[g * WIN_REP], F32)
    for r in range(1, WIN_REP):
        sink = jnp.where(head == r, sink_ref[g * WIN_REP + r], sink)
    m = jnp.maximum(jnp.maximum(jnp.max(s_c, axis=-1, keepdims=True),
                                jnp.max(s_b, axis=-1, keepdims=True)), sink)
    e_c = jnp.exp(s_c - m)
    e_b = jnp.exp(s_b - m)
    den = jnp.sum(e_c, axis=-1, keepdims=True) + jnp.sum(e_b, axis=-1, keepdims=True) + jnp.exp(sink - m)
    o = (jnp.dot(e_c.astype(BF16), vv_ref[0:n_ctx, :], preferred_element_type=F32)
         + jnp.dot(e_b.astype(BF16), vv_ref[pl.ds(start, band), :], preferred_element_type=F32)) / den
    for pair in range(WIN_REP // 2):
        lo = o[(2 * pair) * blk:(2 * pair + 1) * blk]
        hi = o[(2 * pair + 1) * blk:(2 * pair + 2) * blk]
        o_ref[0, :, pair * LANES:(pair + 1) * LANES] = jnp.where(low, lo, hi).astype(BF16)


def _win(p_all, sink, n_ctx):
    b, t, _ = p_all.shape
    qw = WIN_REP * HEAD_DIM
    kern = functools.partial(_win_kernel, n_ctx=n_ctx)
    return pl.pallas_call(
        kern,
        grid=(b, WIN_KV_HEADS, t // WIN_BLOCK),
        in_specs=[pl.BlockSpec(memory_space=pltpu.SMEM),
                  pl.BlockSpec((1, WIN_BLOCK, qw), lambda bi, gi, ni: (bi, ni, COL_WQ // qw + gi)),
                  pl.BlockSpec((1, t, LANES), lambda bi, gi, ni: (bi, 0, COL_WK // LANES)),
                  pl.BlockSpec((1, t, LANES), lambda bi, gi, ni: (bi, 0, COL_WV // LANES))],
        out_specs=pl.BlockSpec((1, WIN_BLOCK, qw), lambda bi, gi, ni: (bi, ni, gi)),
        out_shape=jax.ShapeDtypeStruct((b, t, GROUP_WIDTH), BF16),
        scratch_shapes=[pltpu.VMEM((LANES, t), BF16), pltpu.VMEM((t, LANES), BF16)],
        compiler_params=_params(("arbitrary", "arbitrary", "arbitrary")),
        name="win_attn",
    )(sink, p_all, p_all, p_all)


def _outproj_kernel(a_ref, b_ref, c_ref, d_ref, w_ref, h_ref, m_ref, o_ref, wb_ref, *, tm, n_ctx):
    ti = pl.program_id(2)

    @pl.when((pl.program_id(1) == 0) & (ti == 0))
    def _():
        wb_ref[...] = w_ref[...].astype(BF16)

    y = None
    for i, r in enumerate((a_ref, b_ref, c_ref, d_ref)):
        part = jnp.dot(r[0], wb_ref[i * GROUP_WIDTH:(i + 1) * GROUP_WIDTH, :], preferred_element_type=F32)
        y = part if y is None else y + part
    gate = _mod_row(m_ref, 2, _row_is_ctx(ti, tm, n_ctx))
    o_ref[0] = h_ref[0] + gate * y


def _outproj(mixes, w, layer, h, modsel, n_ctx, tm=768, tn=1024):
    b, t, d = h.shape
    tn = min(tn, d)
    assert d % tn == 0 and t % tm == 0
    kern = functools.partial(_outproj_kernel, tm=tm, n_ctx=n_ctx)
    mix_spec = pl.BlockSpec((1, tm, GROUP_WIDTH), lambda j, bi, ti: (bi, ti, 0))
    return pl.pallas_call(
        kern,
        grid=(d // tn, b, t // tm),
        in_specs=[mix_spec, mix_spec, mix_spec, mix_spec,
                  pl.BlockSpec((None, 4 * GROUP_WIDTH, tn), lambda j, bi, ti: (layer, 0, j)),
                  pl.BlockSpec((1, tm, tn), lambda j, bi, ti: (bi, ti, j)),
                  pl.BlockSpec((None, 2, 6, tn), lambda j, bi, ti: (bi, 0, 0, j))],
        out_specs=pl.BlockSpec((1, tm, tn), lambda j, bi, ti: (bi, ti, j)),
        out_shape=jax.ShapeDtypeStruct((b, t, d), F32),
        scratch_shapes=[pltpu.VMEM((4 * GROUP_WIDTH, tn), BF16)],
        compiler_params=_params(("arbitrary", "arbitrary", "arbitrary")),
        name="outproj",
    )(*mixes, w, h, modsel)


def _ffn_in_kernel(x_ref, m_ref, g_ref, o_ref, *, tm, n_ctx, row0):
    def store(rs, y):
        o_ref[0, rs, :] = y
    _norm_modulate_rows(x_ref, g_ref, m_ref, 3, row0 + pl.program_id(1) * tm, tm, n_ctx, store)


def _route_kernel(x_ref, m_ref, g_ref, r_ref, o_ref, gate_ref, sel_ref, *, tm, n_ctx, row0):
    _ffn_in_kernel(x_ref, m_ref, g_ref, o_ref, tm=tm, n_ctx=n_ctx, row0=row0)
    logits = jnp.dot(o_ref[0], r_ref[...], preferred_element_type=F32, precision=lax.Precision.HIGHEST)
    lane = lax.broadcasted_iota(jnp.int32, (1, LANES), 1)
    logits = jnp.where(lane < N_EXPERTS, logits, -jnp.inf)
    m1 = jnp.max(logits, axis=-1, keepdims=True)
    i1 = jnp.min(jnp.where(logits == m1, lane, LANES), axis=-1, keepdims=True)
    rest = jnp.where(lane == i1, -jnp.inf, logits)
    m2 = jnp.max(rest, axis=-1, keepdims=True)
    i2 = jnp.min(jnp.where(rest == m2, lane, LANES), axis=-1, keepdims=True)
    e2 = jnp.exp(m2 - m1)
    w1 = 1.0 / (1.0 + e2)
    w2 = e2 / (1.0 + e2)
    gate_ref[0] = jnp.where(lane == i1, w1, 0.0) + jnp.where(lane == i2, w2, 0.0)
    sel_ref[0] = ((lane == i1) | (lane == i2)).astype(jnp.int32)


def _ffn_in(h, modsel, g, n_ctx, row0, router=None, tm=256):
    b, t, d = h.shape
    nt = (t - row0) // tm
    off = row0 // tm
    in_specs = [pl.BlockSpec((1, tm, d), lambda bi, ti: (bi, ti + off, 0)),
                pl.BlockSpec((None, 2, 6, d), lambda bi, ti: (bi, 0, 0, 0)),
                pl.BlockSpec((1, d), lambda bi, ti: (0, 0))]
    f_spec = pl.BlockSpec((1, tm, d), lambda bi, ti: (bi, ti, 0))
    f_shape = jax.ShapeDtypeStruct((b, t - row0, d), F32)
    if router is None:
        kern = functools.partial(_ffn_in_kernel, tm=tm, n_ctx=n_ctx, row0=row0)
        return pl.pallas_call(
            kern, grid=(b, nt), in_specs=in_specs, out_specs=f_spec, out_shape=f_shape,
            compiler_params=_params(("arbitrary", "arbitrary")), name="ffn_in",
        )(h, modsel, g.reshape(1, d))
    r_pad = jnp.zeros((d, LANES), F32).at[:, :N_EXPERTS].set(router)
    e_spec = pl.BlockSpec((1, tm, LANES), lambda bi, ti: (bi, ti, 0))
    kern = functools.partial(_route_kernel, tm=tm, n_ctx=n_ctx, row0=row0)
    return pl.pallas_call(
        kern, grid=(b, nt),
        in_specs=in_specs + [pl.BlockSpec((d, LANES), lambda bi, ti: (0, 0))],
        out_specs=[f_spec, e_spec, e_spec],
        out_shape=[f_shape, jax.ShapeDtypeStruct((b, t - row0, LANES), F32),
                   jax.ShapeDtypeStruct((b, t - row0, LANES), jnp.int32)],
        compiler_params=_params(("arbitrary", "arbitrary")), name="ffn_in_route",
    )(h, modsel, g.reshape(1, d), r_pad)


FFN_TM = 1024
FFN_SUB = 256


def _ffn_kernel(te_ref, tr_ref, x_ref, w1_ref, w3_ref, w2_ref, o_ref, xs_ref):
    i = pl.program_id(0)
    j = pl.program_id(1)
    n_blocks = (tr_ref[i] + FFN_SUB - 1) // FFN_SUB

    @pl.when(j == 0)
    def _():
        o_ref[...] = jnp.zeros_like(o_ref)
        xs_ref[...] = x_ref[...].astype(BF16)

    for nb in range(1, FFN_TM // FFN_SUB + 1):
        @pl.when(n_blocks == nb)
        def _(nb=nb):
            m = nb * FFN_SUB
            xs = xs_ref[0:m, :]
            h1 = jnp.dot(xs, w1_ref[0].astype(BF16), preferred_element_type=F32)
            h3 = jnp.dot(xs, w3_ref[0].astype(BF16), preferred_element_type=F32)
            act = (h1 * _sigmoid(h1) * h3).astype(BF16)
            o_ref[0:m, :] += jnp.dot(act, w2_ref[0].astype(BF16), preferred_element_type=F32)


def _ffn(x_rows, w1, w3, w2, tile_expert, tile_rows, tf=256):
    nr, d = x_rows.shape
    n_exp, _, f = w1.shape
    nj = f // tf

    def wcol(i, j, te, tr):
        return (te[i], 0, jnp.where(tr[i] > 0, j, nj - 1))

    def wrow(i, j, te, tr):
        return (te[i], jnp.where(tr[i] > 0, j, nj - 1), 0)

    return pl.pallas_call(
        _ffn_kernel,
        grid_spec=pltpu.PrefetchScalarGridSpec(
            num_scalar_prefetch=2,
            grid=(nr // FFN_TM, nj),
            in_specs=[pl.BlockSpec((FFN_TM, d), lambda i, j, te, tr: (i, 0)),
                      pl.BlockSpec((1, d, tf), wcol),
                      pl.BlockSpec((1, d, tf), wcol),
                      pl.BlockSpec((1, tf, d), wrow)],
            out_specs=pl.BlockSpec((FFN_TM, d), lambda i, j, te, tr: (i, 0)),
            scratch_shapes=[pltpu.VMEM((FFN_TM, d), BF16)]),
        out_shape=jax.ShapeDtypeStruct((nr, d), F32),
        compiler_params=_params(("arbitrary", "arbitrary")),
        name="ffn",
    )(tile_expert, tile_rows, x_rows, w1, w3, w2)


def _residual_kernel(h_ref, y_ref, m_ref, o_ref, *, tm, n_ctx):
    gate = _mod_row(m_ref, 5, _row_is_ctx(pl.program_id(1), tm, n_ctx))
    o_ref[0] = h_ref[0] + gate * y_ref[0]


def _residual(h, y, modsel, n_ctx, tm=256):
    b, t, d = h.shape
    spec = pl.BlockSpec((1, tm, d), lambda bi, ti: (bi, ti, 0))
    kern = functools.partial(_residual_kernel, tm=tm, n_ctx=n_ctx)
    return pl.pallas_call(
        kern, grid=(b, t // tm),
        in_specs=[spec, spec, pl.BlockSpec((None, 2, 6, d), lambda bi, ti: (bi, 0, 0, 0))],
        out_specs=spec, out_shape=jax.ShapeDtypeStruct((b, t, d), F32),
        compiler_params=_params(("arbitrary", "arbitrary")), name="residual",
    )(h, y, modsel)


def _moe_final_kernel(h_ref, y0_ref, y1_ref, w0_ref, w1_ref, m_ref, g_ref, o_ref):
    y = w0_ref[0] * y0_ref[0] + w1_ref[0] * y1_ref[0]
    hn = h_ref[0] + m_ref[1, 5:6, :] * y
    ms = jnp.mean(hn * hn, axis=-1, keepdims=True)
    o_ref[0] = hn * lax.rsqrt(ms + EPS) * g_ref[...]


def _moe_final(h, y0, y1, w0, w1, modsel, g, row0, tm=256):
    b, t, d = h.shape
    n_lat = t - row0
    off = row0 // tm
    spec = pl.BlockSpec((1, tm, d), lambda bi, ti: (bi, ti, 0))
    wspec = pl.BlockSpec((1, tm, 1), lambda bi, ti: (bi, ti, 0))
    return pl.pallas_call(
        _moe_final_kernel, grid=(b, n_lat // tm),
        in_specs=[pl.BlockSpec((1, tm, d), lambda bi, ti: (bi, ti + off, 0)), spec, spec, wspec, wspec,
                  pl.BlockSpec((None, 2, 6, d), lambda bi, ti: (bi, 0, 0, 0)),
                  pl.BlockSpec((1, d), lambda bi, ti: (0, 0))],
        out_specs=spec, out_shape=jax.ShapeDtypeStruct((b, n_lat, d), F32),
        compiler_params=_params(("arbitrary", "arbitrary")), name="moe_final",
    )(h, y0, y1, w0, w1, modsel, g.reshape(1, d))


def _routing_tables(sel, gates, n_tiles):
    n, n_exp = sel.shape
    seli = sel.astype(jnp.int32)
    counts = jnp.sum(seli, axis=0)
    rank = jnp.cumsum(seli, axis=0) - seli
    tiles_e = (counts + FFN_TM - 1) // FFN_TM
    tile_end = jnp.cumsum(tiles_e)
    tile_start = tile_end - tiles_e
    pos = tile_start[None, :] * FFN_TM + rank
    n_rows = n_tiles * FFN_TM
    expert = jnp.arange(n_exp, dtype=jnp.int32)[None, :]
    e_lo = jnp.min(jnp.where(sel, expert, n_exp), axis=1, keepdims=True)
    e_hi = jnp.max(jnp.where(sel, expert, -1), axis=1, keepdims=True)
    pick = lambda a, e: jnp.sum(jnp.where(expert == e, a, 0), axis=1)
    pos0, pos1 = pick(pos, e_lo), pick(pos, e_hi)
    w0, w1 = pick(gates, e_lo), pick(gates, e_hi)
    tok = jnp.arange(n, dtype=jnp.int32)
    src = (jnp.arange(n_rows, dtype=jnp.int32) % n).at[jnp.concatenate([pos0, pos1])].set(
        jnp.concatenate([tok, tok]), mode="promise_in_bounds", unique_indices=True)

    tile = jnp.arange(n_tiles, dtype=jnp.int32)
    te = jnp.minimum(jnp.sum((tile_end[None, :] <= tile[:, None]).astype(jnp.int32), axis=1), n_exp - 1)
    tr = jnp.clip(counts[te] - (tile - tile_start[te]) * FFN_TM, 0, FFN_TM)
    tr = jnp.where(tile < tile_end[-1], tr, 0)
    return src, pos0, pos1, w0, w1, te.astype(jnp.int32), tr.astype(jnp.int32)


def kernel(x, c, ctx, c_ctx, w_mod, b_mod, norm_mix, norm_ffn, w_in, w_out, pool_w, pool_scale, diff_lam,
           diff_subln, sgu_ln_g, sgu_ln_b, sgu_w, sgu_b, win_sink, ffn_w1, ffn_w3, ffn_w2, moe_router,
           moe_w1, moe_w3, moe_w2, norm_final):
    b, n_lat, d = x.shape
    n_ctx = ctx.shape[1]
    t = n_ctx + n_lat
    depth = w_mod.shape[0]
    segments = ((0, n_ctx), (n_ctx, n_lat))

    c_rows = jnp.zeros((8, d), F32).at[:b].set(c).at[b].set(c_ctx)
    mod = _adaln(c_rows, w_mod, b_mod).reshape(depth, 8, 6, d)
    cos_t, sin_t = _rope_tables(n_ctx, n_lat)
    h = jnp.concatenate([ctx, x], axis=1)

    out = None
    for l in range(depth):
        last = l == depth - 1
        lat_m = mod[l, :b]
        ctx_m = jnp.broadcast_to(mod[l, b:b + 1], lat_m.shape)
        modsel = jnp.stack([ctx_m, lat_m], axis=1)

        p_all = _inproj(h, modsel, norm_mix[l], w_in, l, cos_t, sin_t, n_ctx)

        lam_init = 0.8 - 0.6 * math.exp(-0.3 * l)
        lq1, lk1, lq2, lk2 = diff_lam[l].astype(F32)
        lam = jnp.exp(jnp.sum(lq1 * lk1)) - jnp.exp(jnp.sum(lq2 * lk2)) + lam_init

        mixes = (_pool(p_all, pool_w[l], pool_scale[l], segments),
                 _diff(p_all, lam, diff_subln[l], n_ctx, lam_init),
                 _sgu(p_all, sgu_ln_g[l], sgu_ln_b[l], sgu_w[l], sgu_b[l]),
                 _win(p_all, win_sink[l], n_ctx))
        h = _outproj(mixes, w_out, l, h, modsel, n_ctx)

        i = l // 2
        row0 = n_ctx if last else 0
        rows = b * (t - row0)
        if l % 2 == 0:
            f = _ffn_in(h, modsel, norm_ffn[l], n_ctx, row0).reshape(rows, d)
            n_tiles = rows // FFN_TM
            te = jnp.zeros((n_tiles,), jnp.int32)
            tr = jnp.full((n_tiles,), FFN_TM, jnp.int32)
            y = _ffn(f, ffn_w1[i:i + 1], ffn_w3[i:i + 1], ffn_w2[i:i + 1], te, tr)
            assert not last
            h = _residual(h, y.reshape(b, t - row0, d), modsel, n_ctx)
        else:
            assert last
            f, gates, sel = _ffn_in(h, modsel, norm_ffn[l], n_ctx, row0, router=moe_router[i])
            f = f.reshape(rows, d)
            n_exp = moe_w1.shape[1]
            n_tiles = (2 * rows) // FFN_TM + n_exp
            src, pos0, pos1, w0, w1, te, tr = _routing_tables(
                sel.reshape(rows, LANES)[:, :n_exp] > 0, gates.reshape(rows, LANES)[:, :n_exp], n_tiles)
            rows_of = lambda a, idx: a.at[idx].get(mode="promise_in_bounds")
            ys = _ffn(rows_of(f, src), moe_w1[i], moe_w3[i], moe_w2[i], te, tr)
            shp = (b, t - row0, d)
            out = _moe_final(h, rows_of(ys, pos0).reshape(shp), rows_of(ys, pos1).reshape(shp),
                             w0.reshape(b, t - row0, 1), w1.reshape(b, t - row0, 1), modsel, norm_final, row0)
    return out
```

```python
import functools
import math

import jax
import jax.numpy as jnp
from jax import lax
from jax.experimental import pallas as pl
from jax.experimental.pallas import tpu as pltpu

F32 = jnp.float32
BF16 = jnp.bfloat16

GRID_W = 64
GROUP_WIDTH = 512
HEAD_DIM = 64
ROPE_BASE = 10000.0
EPS = 1e-6
NEG_INF = -1e30
LANES = 128

POOL_WINDOWS = (2, 4, 8, 16)
POOL_PAD = 16
DIFF_HEADS = GROUP_WIDTH // (2 * HEAD_DIM)
SGU_CHUNK = 128
SGU_GROUPS = 4
WIN_HEADS = GROUP_WIDTH // HEAD_DIM
WIN_KV_HEADS = 2
WIN_REP = WIN_HEADS // WIN_KV_HEADS
WINDOW = 128
WIN_BLOCK = 128
WIN_QB = 2 * WIN_BLOCK
WIN_BAND = WIN_QB + 2 * WIN_BLOCK
N_EXPERTS = 8

COL_POOL = 0
COL_DQ = GROUP_WIDTH
COL_DK = 2 * GROUP_WIDTH
COL_DV = 3 * GROUP_WIDTH
COL_Z = 4 * GROUP_WIDTH
COL_WQ = 6 * GROUP_WIDTH
COL_WK = COL_WQ + WIN_HEADS * HEAD_DIM
COL_WV = COL_WK + WIN_KV_HEADS * HEAD_DIM
IN_COLS = COL_WV + WIN_KV_HEADS * HEAD_DIM
ROPE_RANGES = ((COL_DQ, COL_DV), (COL_WQ, COL_WV))

VMEM_LIMIT = 56 * 1024 * 1024
NORM_ROWS = 32
ROPE_ROWS = 64


def _params(sem, vmem=VMEM_LIMIT):
    return pltpu.CompilerParams(dimension_semantics=sem, vmem_limit_bytes=vmem)


def _sigmoid(x):
    return 1.0 / (1.0 + jnp.exp(-x))


def _row_is_ctx(tile_idx, tm, n_ctx):
    rows = tile_idx * tm + lax.broadcasted_iota(jnp.int32, (tm, 1), 0)
    return rows < n_ctx


def _mod_row(m_ref, k, is_ctx):
    return jnp.where(is_ctx, m_ref[0, k:k + 1, :], m_ref[1, k:k + 1, :])


def _norm_modulate_rows(x_ref, g_ref, m_ref, k_shift, first_row, tm, n_ctx, store):
    def body(r, carry):
        r0 = pl.multiple_of(r * NORM_ROWS, NORM_ROWS)
        which = ((first_row + r0) >= n_ctx).astype(jnp.int32)
        rs = pl.ds(r0, NORM_ROWS)
        x = x_ref[0, rs, :]
        ms = jnp.mean(x * x, axis=-1, keepdims=True)
        a = g_ref[...] * (1.0 + m_ref[which, k_shift + 1:k_shift + 2, :])
        store(rs, x * lax.rsqrt(ms + EPS) * a + m_ref[which, k_shift:k_shift + 1, :])
        return carry
    lax.fori_loop(0, tm // NORM_ROWS, body, 0, unroll=2)


def _adaln_kernel(c_ref, w_ref, b_ref, o_ref):
    cv = c_ref[...]
    s = (cv * _sigmoid(cv)).astype(BF16)
    o_ref[0] = jnp.dot(s, w_ref[0].astype(BF16), preferred_element_type=F32) + b_ref[0]


def _adaln(c_rows, w_mod, b_mod, tn=1024):
    depth, d, n = w_mod.shape
    r = c_rows.shape[0]
    return pl.pallas_call(
        _adaln_kernel,
        grid=(depth, n // tn),
        in_specs=[pl.BlockSpec((r, d), lambda l, j: (0, 0)),
                  pl.BlockSpec((1, d, tn), lambda l, j: (l, 0, j)),
                  pl.BlockSpec((1, 1, tn), lambda l, j: (l, 0, j))],
        out_specs=pl.BlockSpec((1, r, tn), lambda l, j: (l, 0, j)),
        out_shape=jax.ShapeDtypeStruct((depth, r, n), F32),
        compiler_params=_params(("arbitrary", "arbitrary")),
        name="adaln",
    )(c_rows, w_mod, b_mod.reshape(depth, 1, n))


def _is_rope_col(col):
    return any(lo <= col < hi for lo, hi in ROPE_RANGES)


def _inproj_kernel(x_ref, m_ref, g_ref, w_ref, cos_ref, sin_ref, o_ref, xn_ref, *, tm, tn, n_ctx):
    ti = pl.program_id(1)
    j = pl.program_id(2)

    @pl.when(j == 0)
    def _():
        def store(rs, y):
            xn_ref[rs, :] = y.astype(BF16)
        _norm_modulate_rows(x_ref, g_ref, m_ref, 0, ti * tm, tm, n_ctx, store)

    o_ref[0] = jnp.dot(xn_ref[...], w_ref[...].astype(BF16), preferred_element_type=F32)
    lane = lax.broadcasted_iota(jnp.int32, (1, LANES), 1)
    first_half = (lane % HEAD_DIM) < (HEAD_DIM // 2)

    for jj in range(IN_COLS // tn):
        chunks = [cc for cc in range(tn // LANES) if _is_rope_col(jj * tn + cc * LANES)]
        if not chunks:
            continue

        @pl.when(j == jj)
        def _(chunks=chunks):
            def rope_rows(r, carry):
                rs = pl.ds(pl.multiple_of(r * ROPE_ROWS, ROPE_ROWS), ROPE_ROWS)
                cos = cos_ref[rs, :]
                sin = sin_ref[rs, :]
                for cc in chunks:
                    cs = slice(cc * LANES, (cc + 1) * LANES)
                    a = o_ref[0, rs, cs]
                    partner = jnp.where(first_half,
                                        pltpu.roll(a, LANES - HEAD_DIM // 2, 1),
                                        pltpu.roll(a, HEAD_DIM // 2, 1))
                    o_ref[0, rs, cs] = a * cos + partner * sin
                return carry
            lax.fori_loop(0, tm // ROPE_ROWS, rope_rows, 0, unroll=2)


def _inproj(h, modsel, g, w, layer, cos_t, sin_t, n_ctx, tm=768, tn=768):
    b, t, d = h.shape
    kern = functools.partial(_inproj_kernel, tm=tm, tn=tn, n_ctx=n_ctx)
    return pl.pallas_call(
        kern,
        grid=(b, t // tm, IN_COLS // tn),
        in_specs=[pl.BlockSpec((1, tm, d), lambda bi, ti, j: (bi, ti, 0)),
                  pl.BlockSpec((None, 2, 6, d), lambda bi, ti, j: (bi, 0, 0, 0)),
                  pl.BlockSpec((1, d), lambda bi, ti, j: (0, 0)),
                  pl.BlockSpec((None, d, tn), lambda bi, ti, j: (layer, 0, j)),
                  pl.BlockSpec((tm, LANES), lambda bi, ti, j: (ti, 0)),
                  pl.BlockSpec((tm, LANES), lambda bi, ti, j: (ti, 0))],
        out_specs=pl.BlockSpec((1, tm, tn), lambda bi, ti, j: (bi, ti, j)),
        out_shape=jax.ShapeDtypeStruct((b, t, IN_COLS), F32),
        scratch_shapes=[pltpu.VMEM((tm, d), BF16)],
        compiler_params=_params(("arbitrary", "arbitrary", "arbitrary")),
        name="inproj",
    )(h, modsel, g.reshape(1, d), w, cos_t, sin_t)


def _rope_tables(n_ctx, n_lat):
    n_rows = n_lat // GRID_W
    rows = jnp.repeat(jnp.arange(n_rows, dtype=F32), GRID_W)
    cols = jnp.tile(jnp.arange(GRID_W, dtype=F32), n_rows)
    n_freq = HEAD_DIM // 4
    inv_freq = ROPE_BASE ** (-jnp.arange(n_freq, dtype=F32) / n_freq)
    ang = jnp.concatenate([rows[:, None] * inv_freq, cols[:, None] * inv_freq], axis=-1)
    cos, sin = jnp.cos(ang), jnp.sin(ang)
    cos_h = jnp.concatenate([cos, cos], axis=-1)
    sin_h = jnp.concatenate([-sin, sin], axis=-1)
    cos_t = jnp.concatenate([jnp.ones((n_ctx, HEAD_DIM), F32), cos_h], axis=0)
    sin_t = jnp.concatenate([jnp.zeros((n_ctx, HEAD_DIM), F32), sin_h], axis=0)
    rep = LANES // HEAD_DIM
    return jnp.tile(cos_t, (1, rep)), jnp.tile(sin_t, (1, rep))


def _pool_kernel(a_ref, w_ref, s_ref, o_ref, pad_ref, *, segments):
    for g, win in enumerate(POOL_WINDOWS):
        half = win // 2
        cs = slice(g * LANES, (g + 1) * LANES)
        wg = w_ref[g].astype(BF16)
        for s0, n in segments:
            a = a_ref[0, s0:s0 + n, cs]
            pad_ref[0:POOL_PAD, :] = jnp.zeros((POOL_PAD, LANES), F32)
            pad_ref[POOL_PAD:POOL_PAD + n, :] = a
            pad_ref[POOL_PAD + n:2 * POOL_PAD + n, :] = jnp.zeros((POOL_PAD, LANES), F32)
            tot = pad_ref[POOL_PAD - half:POOL_PAD - half + n, :]
            for k in range(-half + 1, half):
                tot = tot + pad_ref[POOL_PAD + k:POOL_PAD + k + n, :]
            t = lax.broadcasted_iota(jnp.int32, (n, 1), 0)
            cnt = (jnp.minimum(t + half, n) - jnp.maximum(t - half, 0)).astype(F32)
            dlt = (tot / cnt - a).astype(BF16)
            y = jnp.dot(dlt, wg, preferred_element_type=F32)
            o_ref[0, s0:s0 + n, cs] = (y * s_ref[:, cs]).astype(BF16)


def _pool(p_all, w, scale, segments):
    b, t, _ = p_all.shape
    max_n = max(n for _, n in segments)
    kern = functools.partial(_pool_kernel, segments=segments)
    return pl.pallas_call(
        kern,
        grid=(b,),
        in_specs=[pl.BlockSpec((1, t, GROUP_WIDTH), lambda bi: (bi, 0, COL_POOL // GROUP_WIDTH)),
                  pl.BlockSpec(w.shape, lambda bi: (0, 0, 0)),
                  pl.BlockSpec((1, GROUP_WIDTH), lambda bi: (0, 0))],
        out_specs=pl.BlockSpec((1, t, GROUP_WIDTH), lambda bi: (bi, 0, 0)),
        out_shape=jax.ShapeDtypeStruct((b, t, GROUP_WIDTH), BF16),
        scratch_shapes=[pltpu.VMEM((max_n + 2 * POOL_PAD, LANES), F32)],
        compiler_params=_params(("arbitrary",)),
        name="pool",
    )(p_all, w, scale.reshape(1, GROUP_WIDTH))


def _diff_kernel(lam_ref, q_ref, k_ref, v_ref, g_ref, o_ref, kt_ref, vb_ref, *, tq, n_ctx, out_scale):
    ti = pl.program_id(2)
    lam = lam_ref[0]

    @pl.when(ti == 0)
    def _():
        kt_ref[...] = k_ref[0].T.astype(BF16)
        vb_ref[...] = v_ref[0].astype(BF16)

    lane = lax.broadcasted_iota(jnp.int32, (1, 2 * HEAD_DIM), 1)
    q = q_ref[0] * (HEAD_DIM ** -0.5)
    q1 = jnp.where(lane < HEAD_DIM, q, 0.0).astype(BF16)
    q2 = jnp.where(lane >= HEAD_DIM, q, 0.0).astype(BF16)

    def attend(n_keys):
        kt = kt_ref[:, 0:n_keys]
        v = vb_ref[0:n_keys, :]
        s1 = jnp.dot(q1, kt, preferred_element_type=F32)
        s2 = jnp.dot(q2, kt, preferred_element_type=F32)
        e1 = jnp.exp(s1 - jnp.max(s1, axis=-1, keepdims=True))
        e2 = jnp.exp(s2 - jnp.max(s2, axis=-1, keepdims=True))
        r1 = 1.0 / jnp.sum(e1, axis=-1, keepdims=True)
        r2 = lam / jnp.sum(e2, axis=-1, keepdims=True)
        o = (jnp.dot(e1.astype(BF16), v, preferred_element_type=F32) * r1
             - jnp.dot(e2.astype(BF16), v, preferred_element_type=F32) * r2)
        ms = jnp.mean(o * o, axis=-1, keepdims=True)
        o_ref[0] = (o * lax.rsqrt(ms + EPS) * g_ref[...] * out_scale).astype(BF16)

    @pl.when(ti * tq < n_ctx)
    def _():
        attend(n_ctx)

    @pl.when(ti * tq >= n_ctx)
    def _():
        attend(k_ref.shape[1])


def _diff(p_all, lam, subln, n_ctx, lam_init, tq=256):
    b, t, _ = p_all.shape
    assert n_ctx % tq == 0 and t % tq == 0
    w = 2 * HEAD_DIM
    kern = functools.partial(_diff_kernel, tq=tq, n_ctx=n_ctx, out_scale=1.0 - lam_init)
    return pl.pallas_call(
        kern,
        grid=(b, DIFF_HEADS, t // tq),
        in_specs=[pl.BlockSpec(memory_space=pltpu.SMEM),
                  pl.BlockSpec((1, tq, w), lambda bi, hi, ti: (bi, ti, COL_DQ // w + hi)),
                  pl.BlockSpec((1, t, w), lambda bi, hi, ti: (bi, 0, COL_DK // w + hi)),
                  pl.BlockSpec((1, t, w), lambda bi, hi, ti: (bi, 0, COL_DV // w + hi)),
                  pl.BlockSpec((1, w), lambda bi, hi, ti: (0, 0))],
        out_specs=pl.BlockSpec((1, tq, w), lambda bi, hi, ti: (bi, ti, hi)),
        out_shape=jax.ShapeDtypeStruct((b, t, GROUP_WIDTH), BF16),
        scratch_shapes=[pltpu.VMEM((w, t), BF16), pltpu.VMEM((t, w), BF16)],
        compiler_params=_params(("arbitrary", "arbitrary", "arbitrary")),
        name="diff_attn",
    )(lam.reshape(1), p_all, p_all, p_all, subln.reshape(1, w))


def _gelu_tanh(x):
    return 0.5 * x * (1.0 + jnp.tanh(math.sqrt(2.0 / math.pi) * (x + 0.044715 * (x * x * x))))


def _sgu_kernel(z_ref, g_ref, b_ref, w_ref, bs_ref, o_ref, *, tm):
    z = _gelu_tanh(z_ref[0])
    u = z[:, :GROUP_WIDTH]
    v = z[:, GROUP_WIDTH:]
    mu = jnp.mean(v, axis=-1, keepdims=True)
    var = jnp.mean(jnp.square(v - mu), axis=-1, keepdims=True)
    vn = ((v - mu) * lax.rsqrt(var + EPS) * g_ref[...] + b_ref[...]).astype(BF16)
    for g in range(SGU_GROUPS):
        wg = w_ref[g].astype(BF16)
        bias = bs_ref[:, g:g + 1]
        cs = slice(g * LANES, (g + 1) * LANES)
        for c in range(tm // SGU_CHUNK):
            rs = slice(c * SGU_CHUNK, (c + 1) * SGU_CHUNK)
            sv = jnp.dot(wg, vn[rs, cs], preferred_element_type=F32) + bias
            o_ref[0, rs, cs] = (u[rs, cs] * sv).astype(BF16)


def _sgu(p_all, ln_g, ln_b, w_s, b_s, tm=768):
    b, t, _ = p_all.shape
    zw = 2 * GROUP_WIDTH
    kern = functools.partial(_sgu_kernel, tm=tm)
    return pl.pallas_call(
        kern,
        grid=(b, t // tm),
        in_specs=[pl.BlockSpec((1, tm, zw), lambda bi, ti: (bi, ti, COL_Z // zw)),
                  pl.BlockSpec((1, GROUP_WIDTH), lambda bi, ti: (0, 0)),
                  pl.BlockSpec((1, GROUP_WIDTH), lambda bi, ti: (0, 0)),
                  pl.BlockSpec(w_s.shape, lambda bi, ti: (0, 0, 0)),
                  pl.BlockSpec((SGU_CHUNK, SGU_GROUPS), lambda bi, ti: (0, 0))],
        out_specs=pl.BlockSpec((1, tm, GROUP_WIDTH), lambda bi, ti: (bi, ti, 0)),
        out_shape=jax.ShapeDtypeStruct((b, t, GROUP_WIDTH), BF16),
        compiler_params=_params(("arbitrary", "arbitrary")),
        name="sgu",
    )(p_all, ln_g.reshape(1, GROUP_WIDTH), ln_b.reshape(1, GROUP_WIDTH), w_s, b_s.T)


def _win_band_start(step, t):
    return jnp.clip(step * WIN_QB - WIN_BLOCK, 0, t - WIN_BAND)


def _win_bias(n_ctx, t):
    step = jnp.arange(t // WIN_QB, dtype=jnp.int32)[:, None, None]
    qpos = step * WIN_QB + jnp.arange(WIN_QB, dtype=jnp.int32)[None, :, None]
    kpos = _win_band_start(step, t) + jnp.arange(WIN_BAND, dtype=jnp.int32)[None, None, :]
    valid = (kpos >= n_ctx) & (qpos >= n_ctx) & (jnp.abs(kpos - qpos) <= WINDOW)
    return jnp.where(valid, 0.0, NEG_INF).astype(F32)


def _win_kernel(sink_ref, q_ref, k_ref, v_ref, bias_ref, o_ref, kt_ref, vv_ref, *, n_ctx):
    g = pl.program_id(1)
    n = pl.program_id(2)
    t = k_ref.shape[1]
    blk = WIN_QB
    band = WIN_BAND
    lane = lax.broadcasted_iota(jnp.int32, (1, LANES), 1)
    low = lane < HEAD_DIM

    @pl.when(n == 0)
    def _():
        mine = (lane >= g * HEAD_DIM) & (lane < (g + 1) * HEAD_DIM)
        km = jnp.where(mine, k_ref[0], 0.0)
        vm = jnp.where(mine, v_ref[0], 0.0)
        kt_ref[...] = (km + pltpu.roll(km, HEAD_DIM, 1)).T.astype(BF16)
        vv_ref[...] = (vm + pltpu.roll(vm, HEAD_DIM, 1)).astype(BF16)

    rows = WIN_REP * blk
    head = lax.broadcasted_iota(jnp.int32, (rows, 1), 0) // blk
    start = pl.multiple_of(_win_band_start(n, t), WIN_BLOCK)

    q = q_ref[0] * (HEAD_DIM ** -0.5)
    pieces = []
    for pair in range(WIN_REP // 2):
        qp = q[:, pair * LANES:(pair + 1) * LANES]
        pieces += [jnp.where(low, qp, 0.0), jnp.where(low, 0.0, qp)]
    qs = jnp.concatenate(pieces, axis=0).astype(BF16)
    s_c = jnp.dot(qs, kt_ref[:, 0:n_ctx], preferred_element_type=F32)
    s_b = jnp.dot(qs, kt_ref[:, pl.ds(start, band)], preferred_element_type=F32)
    s_b = (s_b.reshape(WIN_REP, blk, band) + bias_ref[...]).reshape(rows, band)
    sink = jnp.full((rows, 1), sink_ref[g * WIN_REP], F32)
    for r in range(1, WIN_REP):
        sink = jnp.where(head == r, sink_ref[g * WIN_REP + r], sink)
    m = jnp.maximum(jnp.maximum(jnp.max(s_c, axis=-1, keepdims=True),
                                jnp.max(s_b, axis=-1, keepdims=True)), sink)
    e_c = jnp.exp(s_c - m)
    e_b = jnp.exp(s_b - m)
    den = jnp.sum(e_c, axis=-1, keepdims=True) + jnp.sum(e_b, axis=-1, keepdims=True) + jnp.exp(sink - m)
    o = (jnp.dot(e_c.astype(BF16), vv_ref[0:n_ctx, :], preferred_element_type=F32)
         + jnp.dot(e_b.astype(BF16), vv_ref[pl.ds(start, band), :], preferred_element_type=F32)) / den
    for pair in range(WIN_REP // 2):
        lo = o[(2 * pair) * blk:(2 * pair + 1) * blk]
        hi = o[(2 * pair + 1) * blk:(2 * pair + 2) * blk]
        o_ref[0, :, pair * LANES:(pair + 1) * LANES] = jnp.where(low, lo, hi).astype(BF16)


def _win(p_all, sink, n_ctx):
    b, t, _ = p_all.shape
    qw = WIN_REP * HEAD_DIM
    assert t % WIN_QB == 0 and t >= WIN_BAND
    lo = -(-(n_ctx + WIN_BLOCK) // WIN_QB)
    hi = (t - WIN_BAND + WIN_BLOCK) // WIN_QB

    def bias_block(bi, gi, ni):
        return (jnp.where((ni >= lo) & (ni <= hi), lo, ni), 0, 0)

    kern = functools.partial(_win_kernel, n_ctx=n_ctx)
    return pl.pallas_call(
        kern,
        grid=(b, WIN_KV_HEADS, t // WIN_QB),
        in_specs=[pl.BlockSpec(memory_space=pltpu.SMEM),
                  pl.BlockSpec((1, WIN_QB, qw), lambda bi, gi, ni: (bi, ni, COL_WQ // qw + gi)),
                  pl.BlockSpec((1, t, LANES), lambda bi, gi, ni: (bi, 0, COL_WK // LANES)),
                  pl.BlockSpec((1, t, LANES), lambda bi, gi, ni: (bi, 0, COL_WV // LANES)),
                  pl.BlockSpec((1, WIN_QB, WIN_BAND), bias_block)],
        out_specs=pl.BlockSpec((1, WIN_QB, qw), lambda bi, gi, ni: (bi, ni, gi)),
        out_shape=jax.ShapeDtypeStruct((b, t, GROUP_WIDTH), BF16),
        scratch_shapes=[pltpu.VMEM((LANES, t), BF16), pltpu.VMEM((t, LANES), BF16)],
        compiler_params=_params(("arbitrary", "arbitrary", "arbitrary")),
        name="win_attn",
    )(sink, p_all, p_all, p_all, _win_bias(n_ctx, t))


def _outproj_kernel(a_ref, b_ref, c_ref, d_ref, w_ref, h_ref, m_ref, o_ref, wb_ref, *, tm, n_ctx):
    ti = pl.program_id(2)

    @pl.when((pl.program_id(1) == 0) & (ti == 0))
    def _():
        wb_ref[...] = w_ref[...].astype(BF16)

    y = None
    for i, r in enumerate((a_ref, b_ref, c_ref, d_ref)):
        part = jnp.dot(r[0], wb_ref[i * GROUP_WIDTH:(i + 1) * GROUP_WIDTH, :], preferred_element_type=F32)
        y = part if y is None else y + part
    gate = _mod_row(m_ref, 2, _row_is_ctx(ti, tm, n_ctx))
    o_ref[0] = h_ref[0] + gate * y


def _outproj(mixes, w, layer, h, modsel, n_ctx, tm=768, tn=1024):
    b, t, d = h.shape
    tn = min(tn, d)
    assert d % tn == 0 and t % tm == 0
    kern = functools.partial(_outproj_kernel, tm=tm, n_ctx=n_ctx)
    mix_spec = pl.BlockSpec((1, tm, GROUP_WIDTH), lambda j, bi, ti: (bi, ti, 0))
    return pl.pallas_call(
        kern,
        grid=(d // tn, b, t // tm),
        in_specs=[mix_spec, mix_spec, mix_spec, mix_spec,
                  pl.BlockSpec((None, 4 * GROUP_WIDTH, tn), lambda j, bi, ti: (layer, 0, j)),
                  pl.BlockSpec((1, tm, tn), lambda j, bi, ti: (bi, ti, j)),
                  pl.BlockSpec((None, 2, 6, tn), lambda j, bi, ti: (bi, 0, 0, j))],
        out_specs=pl.BlockSpec((1, tm, tn), lambda j, bi, ti: (bi, ti, j)),
        out_shape=jax.ShapeDtypeStruct((b, t, d), F32),
        scratch_shapes=[pltpu.VMEM((4 * GROUP_WIDTH, tn), BF16)],
        compiler_params=_params(("arbitrary", "arbitrary", "arbitrary")),
        name="outproj",
    )(*mixes, w, h, modsel)


def _ffn_in_kernel(x_ref, m_ref, g_ref, o_ref, *, tm, n_ctx, row0):
    def store(rs, y):
        o_ref[0, rs, :] = y
    _norm_modulate_rows(x_ref, g_ref, m_ref, 3, row0 + pl.program_id(1) * tm, tm, n_ctx, store)


def _route_kernel(x_ref, m_ref, g_ref, r_ref, o_ref, gate_ref, sel_ref, *, tm, n_ctx, row0):
    _ffn_in_kernel(x_ref, m_ref, g_ref, o_ref, tm=tm, n_ctx=n_ctx, row0=row0)
    logits = jnp.dot(o_ref[0], r_ref[...], preferred_element_type=F32, precision=lax.Precision.HIGHEST)
    lane = lax.broadcasted_iota(jnp.int32, (1, LANES), 1)
    logits = jnp.where(lane < N_EXPERTS, logits, -jnp.inf)
    m1 = jnp.max(logits, axis=-1, keepdims=True)
    i1 = jnp.min(jnp.where(logits == m1, lane, LANES), axis=-1, keepdims=True)
    rest = jnp.where(lane == i1, -jnp.inf, logits)
    m2 = jnp.max(rest, axis=-1, keepdims=True)
    i2 = jnp.min(jnp.where(rest == m2, lane, LANES), axis=-1, keepdims=True)
    e2 = jnp.exp(m2 - m1)
    w1 = 1.0 / (1.0 + e2)
    w2 = e2 / (1.0 + e2)
    gate_ref[0] = jnp.where(lane == i1, w1, 0.0) + jnp.where(lane == i2, w2, 0.0)
    sel_ref[0] = ((lane == i1) | (lane == i2)).astype(jnp.int32)


def _ffn_in_route(h, modsel, g, n_ctx, row0, router, tm=256):
    b, t, d = h.shape
    nt = (t - row0) // tm
    off = row0 // tm
    in_specs = [pl.BlockSpec((1, tm, d), lambda bi, ti: (bi, ti + off, 0)),
                pl.BlockSpec((None, 2, 6, d), lambda bi, ti: (bi, 0, 0, 0)),
                pl.BlockSpec((1, d), lambda bi, ti: (0, 0))]
    f_spec = pl.BlockSpec((1, tm, d), lambda bi, ti: (bi, ti, 0))
    f_shape = jax.ShapeDtypeStruct((b, t - row0, d), F32)
    r_pad = jnp.zeros((d, LANES), F32).at[:, :N_EXPERTS].set(router)
    e_spec = pl.BlockSpec((1, tm, LANES), lambda bi, ti: (bi, ti, 0))
    kern = functools.partial(_route_kernel, tm=tm, n_ctx=n_ctx, row0=row0)
    return pl.pallas_call(
        kern, grid=(b, nt),
        in_specs=in_specs + [pl.BlockSpec((d, LANES), lambda bi, ti: (0, 0))],
        out_specs=[f_spec, e_spec, e_spec],
        out_shape=[f_shape, jax.ShapeDtypeStruct((b, t - row0, LANES), F32),
                   jax.ShapeDtypeStruct((b, t - row0, LANES), jnp.int32)],
        compiler_params=_params(("arbitrary", "arbitrary")), name="ffn_in_route",
    )(h, modsel, g.reshape(1, d), r_pad)


FFN_TM = 1024
FFN_SUB = 256


def _swiglu_slice(xs, w1, w3, w2):
    h1 = jnp.dot(xs, w1.astype(BF16), preferred_element_type=F32)
    h3 = jnp.dot(xs, w3.astype(BF16), preferred_element_type=F32)
    act = (h1 * _sigmoid(h1) * h3).astype(BF16)
    return jnp.dot(act, w2.astype(BF16), preferred_element_type=F32)


def _ffn_kernel(te_ref, tr_ref, x_ref, w1_ref, w3_ref, w2_ref, o_ref, xs_ref):
    i = pl.program_id(0)
    j = pl.program_id(1)
    n_blocks = (tr_ref[i] + FFN_SUB - 1) // FFN_SUB

    @pl.when(j == 0)
    def _():
        o_ref[...] = jnp.zeros_like(o_ref)
        xs_ref[...] = x_ref[...].astype(BF16)

    for nb in range(1, FFN_TM // FFN_SUB + 1):
        @pl.when(n_blocks == nb)
        def _(nb=nb):
            m = nb * FFN_SUB
            o_ref[0:m, :] += _swiglu_slice(xs_ref[0:m, :], w1_ref[0], w3_ref[0], w2_ref[0])


def _ffn(x_rows, w1, w3, w2, tile_expert, tile_rows, tf=256):
    nr, d = x_rows.shape
    n_exp, _, f = w1.shape
    nj = f // tf

    def wcol(i, j, te, tr):
        return (te[i], 0, jnp.where(tr[i] > 0, j, nj - 1))

    def wrow(i, j, te, tr):
        return (te[i], jnp.where(tr[i] > 0, j, nj - 1), 0)

    return pl.pallas_call(
        _ffn_kernel,
        grid_spec=pltpu.PrefetchScalarGridSpec(
            num_scalar_prefetch=2,
            grid=(nr // FFN_TM, nj),
            in_specs=[pl.BlockSpec((FFN_TM, d), lambda i, j, te, tr: (i, 0)),
                      pl.BlockSpec((1, d, tf), wcol),
                      pl.BlockSpec((1, d, tf), wcol),
                      pl.BlockSpec((1, tf, d), wrow)],
            out_specs=pl.BlockSpec((FFN_TM, d), lambda i, j, te, tr: (i, 0)),
            scratch_shapes=[pltpu.VMEM((FFN_TM, d), BF16)]),
        out_shape=jax.ShapeDtypeStruct((nr, d), F32),
        compiler_params=_params(("arbitrary", "arbitrary")),
        name="ffn",
    )(tile_expert, tile_rows, x_rows, w1, w3, w2)


def _ffn_dense_kernel(h_ref, m_ref, g_ref, w1_ref, w3_ref, w2_ref, o_ref, xs_ref, *, tm, n_ctx):
    ti = pl.program_id(1)
    j = pl.program_id(2)

    @pl.when(j == 0)
    def _():
        def store(rs, y):
            xs_ref[rs, :] = y.astype(BF16)
        _norm_modulate_rows(h_ref, g_ref, m_ref, 3, ti * tm, tm, n_ctx, store)
        o_ref[...] = jnp.zeros_like(o_ref)

    o_ref[0] += _swiglu_slice(xs_ref[...], w1_ref[...], w3_ref[...], w2_ref[...])

    @pl.when(j == pl.num_programs(2) - 1)
    def _():
        gate = _mod_row(m_ref, 5, _row_is_ctx(ti, tm, n_ctx))
        o_ref[0] = h_ref[0] + gate * o_ref[0]


def _ffn_dense(h, modsel, g, w1, w3, w2, layer, n_ctx, tf=256):
    b, t, d = h.shape
    f = w1.shape[2]
    tm = t // 2
    assert t % tm == 0 and f % tf == 0 and tm % NORM_ROWS == 0 and n_ctx % NORM_ROWS == 0
    kern = functools.partial(_ffn_dense_kernel, tm=tm, n_ctx=n_ctx)
    return pl.pallas_call(
        kern, grid=(b, t // tm, f // tf),
        in_specs=[pl.BlockSpec((1, tm, d), lambda bi, ti, j: (bi, ti, 0)),
                  pl.BlockSpec((None, 2, 6, d), lambda bi, ti, j: (bi, 0, 0, 0)),
                  pl.BlockSpec((1, d), lambda bi, ti, j: (0, 0)),
                  pl.BlockSpec((None, d, tf), lambda bi, ti, j: (layer, 0, j)),
                  pl.BlockSpec((None, d, tf), lambda bi, ti, j: (layer, 0, j)),
                  pl.BlockSpec((None, tf, d), lambda bi, ti, j: (layer, j, 0))],
        out_specs=pl.BlockSpec((1, tm, d), lambda bi, ti, j: (bi, ti, 0), pipeline_mode=pl.Buffered(1)),
        out_shape=jax.ShapeDtypeStruct((b, t, d), F32),
        scratch_shapes=[pltpu.VMEM((tm, d), BF16)],
        compiler_params=_params(("arbitrary", "arbitrary", "arbitrary")), name="ffn_dense",
    )(h, modsel, g.reshape(1, d), w1, w3, w2)


def _moe_final_kernel(h_ref, y0_ref, y1_ref, w0_ref, w1_ref, m_ref, g_ref, o_ref):
    y = w0_ref[0] * y0_ref[0] + w1_ref[0] * y1_ref[0]
    hn = h_ref[0] + m_ref[1, 5:6, :] * y
    ms = jnp.mean(hn * hn, axis=-1, keepdims=True)
    o_ref[0] = hn * lax.rsqrt(ms + EPS) * g_ref[...]


def _moe_final(h, y0, y1, w0, w1, modsel, g, row0, tm=256):
    b, t, d = h.shape
    n_lat = t - row0
    off = row0 // tm
    spec = pl.BlockSpec((1, tm, d), lambda bi, ti: (bi, ti, 0))
    wspec = pl.BlockSpec((1, tm, 1), lambda bi, ti: (bi, ti, 0))
    return pl.pallas_call(
        _moe_final_kernel, grid=(b, n_lat // tm),
        in_specs=[pl.BlockSpec((1, tm, d), lambda bi, ti: (bi, ti + off, 0)), spec, spec, wspec, wspec,
                  pl.BlockSpec((None, 2, 6, d), lambda bi, ti: (bi, 0, 0, 0)),
                  pl.BlockSpec((1, d), lambda bi, ti: (0, 0))],
        out_specs=spec, out_shape=jax.ShapeDtypeStruct((b, n_lat, d), F32),
        compiler_params=_params(("arbitrary", "arbitrary")), name="moe_final",
    )(h, y0, y1, w0, w1, modsel, g.reshape(1, d))


def _routing_tables(sel, gates, n_tiles):
    n, n_exp = sel.shape
    seli = sel.astype(jnp.int32)
    counts = jnp.sum(seli, axis=0)
    rank = jnp.cumsum(seli, axis=0) - seli
    tiles_e = (counts + FFN_TM - 1) // FFN_TM
    tile_end = jnp.cumsum(tiles_e)
    tile_start = tile_end - tiles_e
    pos = tile_start[None, :] * FFN_TM + rank
    n_rows = n_tiles * FFN_TM
    expert = jnp.arange(n_exp, dtype=jnp.int32)[None, :]
    e_lo = jnp.min(jnp.where(sel, expert, n_exp), axis=1, keepdims=True)
    e_hi = jnp.max(jnp.where(sel, expert, -1), axis=1, keepdims=True)
    pick = lambda a, e: jnp.sum(jnp.where(expert == e, a, 0), axis=1)
    pos0, pos1 = pick(pos, e_lo), pick(pos, e_hi)
    w0, w1 = pick(gates, e_lo), pick(gates, e_hi)
    tok = jnp.arange(n, dtype=jnp.int32)
    src = (jnp.arange(n_rows, dtype=jnp.int32) % n).at[jnp.concatenate([pos0, pos1])].set(
        jnp.concatenate([tok, tok]), mode="promise_in_bounds", unique_indices=True)

    tile = jnp.arange(n_tiles, dtype=jnp.int32)
    te = jnp.minimum(jnp.sum((tile_end[None, :] <= tile[:, None]).astype(jnp.int32), axis=1), n_exp - 1)
    tr = jnp.clip(counts[te] - (tile - tile_start[te]) * FFN_TM, 0, FFN_TM)
    tr = jnp.where(tile < tile_end[-1], tr, 0)
    return src, pos0, pos1, w0, w1, te.astype(jnp.int32), tr.astype(jnp.int32)


def kernel(x, c, ctx, c_ctx, w_mod, b_mod, norm_mix, norm_ffn, w_in, w_out, pool_w, pool_scale, diff_lam,
           diff_subln, sgu_ln_g, sgu_ln_b, sgu_w, sgu_b, win_sink, ffn_w1, ffn_w3, ffn_w2, moe_router,
           moe_w1, moe_w3, moe_w2, norm_final):
    b, n_lat, d = x.shape
    n_ctx = ctx.shape[1]
    t = n_ctx + n_lat
    depth = w_mod.shape[0]
    segments = ((0, n_ctx), (n_ctx, n_lat))

    c_rows = jnp.zeros((8, d), F32).at[:b].set(c).at[b].set(c_ctx)
    mod = _adaln(c_rows, w_mod, b_mod).reshape(depth, 8, 6, d)
    cos_t, sin_t = _rope_tables(n_ctx, n_lat)
    h = jnp.concatenate([ctx, x], axis=1)

    out = None
    for l in range(depth):
        last = l == depth - 1
        lat_m = mod[l, :b]
        ctx_m = jnp.broadcast_to(mod[l, b:b + 1], lat_m.shape)
        modsel = jnp.stack([ctx_m, lat_m], axis=1)

        p_all = _inproj(h, modsel, norm_mix[l], w_in, l, cos_t, sin_t, n_ctx)

        lam_init = 0.8 - 0.6 * math.exp(-0.3 * l)
        lq1, lk1, lq2, lk2 = diff_lam[l].astype(F32)
        lam = jnp.exp(jnp.sum(lq1 * lk1)) - jnp.exp(jnp.sum(lq2 * lk2)) + lam_init

        mixes = (_pool(p_all, pool_w[l], pool_scale[l], segments),
                 _diff(p_all, lam, diff_subln[l], n_ctx, lam_init),
                 _sgu(p_all, sgu_ln_g[l], sgu_ln_b[l], sgu_w[l], sgu_b[l]),
                 _win(p_all, win_sink[l], n_ctx))
        h = _outproj(mixes, w_out, l, h, modsel, n_ctx)

        i = l // 2
        row0 = n_ctx if last else 0
        rows = b * (t - row0)
        if l % 2 == 0:
            assert not last
            h = _ffn_dense(h, modsel, norm_ffn[l], ffn_w1, ffn_w3, ffn_w2, i, n_ctx)
        else:
            assert last
            f, gates, sel = _ffn_in_route(h, modsel, norm_ffn[l], n_ctx, row0, moe_router[i])
            f = f.reshape(rows, d)
            n_exp = moe_w1.shape[1]
            n_tiles = (2 * rows) // FFN_TM + n_exp
            src, pos0, pos1, w0, w1, te, tr = _routing_tables(
                sel.reshape(rows, LANES)[:, :n_exp] > 0, gates.reshape(rows, LANES)[:, :n_exp], n_tiles)
            rows_of = lambda a, idx: a.at[idx].get(mode="promise_in_bounds")
            ys = _ffn(rows_of(f, src), moe_w1[i], moe_w3[i], moe_w2[i], te, tr)
            shp = (b, t - row0, d)
            out = _moe_final(h, rows_of(ys, pos0).reshape(shp), rows_of(ys, pos1).reshape(shp),
                             w0.reshape(b, t - row0, 1), w1.reshape(b, t - row0, 1), modsel, norm_final, row0)
    return out
```

```python
import functools
import math

import jax
import jax.numpy as jnp
from jax import lax
from jax.experimental import pallas as pl
from jax.experimental.pallas import tpu as pltpu

F32 = jnp.float32
BF16 = jnp.bfloat16

GRID_W = 64
GROUP_WIDTH = 512
HEAD_DIM = 64
ROPE_BASE = 10000.0
EPS = 1e-6
NEG_INF = -1e30
LANES = 128

POOL_WINDOWS = (2, 4, 8, 16)
POOL_PAD = 16
DIFF_HEADS = GROUP_WIDTH // (2 * HEAD_DIM)
SGU_CHUNK = 128
SGU_GROUPS = 4
WIN_HEADS = GROUP_WIDTH // HEAD_DIM
WIN_KV_HEADS = 2
WIN_REP = WIN_HEADS // WIN_KV_HEADS
WINDOW = 128
WIN_BLOCK = 128
WIN_QB = 2 * WIN_BLOCK
WIN_BAND = WIN_QB + 2 * WIN_BLOCK
N_EXPERTS = 8

COL_POOL = 0
COL_DQ = GROUP_WIDTH
COL_DK = 2 * GROUP_WIDTH
COL_DV = 3 * GROUP_WIDTH
COL_Z = 4 * GROUP_WIDTH
COL_WQ = 6 * GROUP_WIDTH
COL_WK = COL_WQ + WIN_HEADS * HEAD_DIM
COL_WV = COL_WK + WIN_KV_HEADS * HEAD_DIM
IN_COLS = COL_WV + WIN_KV_HEADS * HEAD_DIM
ROPE_RANGES = ((COL_DQ, COL_DV), (COL_WQ, COL_WV))

VMEM_LIMIT = 56 * 1024 * 1024
NORM_ROWS = 32
ROPE_ROWS = 64


def _params(sem, vmem=VMEM_LIMIT):
    return pltpu.CompilerParams(dimension_semantics=sem, vmem_limit_bytes=vmem)


def _sigmoid(x):
    return 1.0 / (1.0 + jnp.exp(-x))


def _row_is_ctx(tile_idx, tm, n_ctx):
    rows = tile_idx * tm + lax.broadcasted_iota(jnp.int32, (tm, 1), 0)
    return rows < n_ctx


def _mod_row(m_ref, k, is_ctx):
    return jnp.where(is_ctx, m_ref[0, k:k + 1, :], m_ref[1, k:k + 1, :])


def _norm_modulate_rows(x_ref, g_ref, m_ref, k_shift, first_row, tm, n_ctx, store):
    def body(r, carry):
        r0 = pl.multiple_of(r * NORM_ROWS, NORM_ROWS)
        which = ((first_row + r0) >= n_ctx).astype(jnp.int32)
        rs = pl.ds(r0, NORM_ROWS)
        x = x_ref[0, rs, :]
        ms = jnp.mean(x * x, axis=-1, keepdims=True)
        a = g_ref[...] * (1.0 + m_ref[which, k_shift + 1:k_shift + 2, :])
        store(rs, x * lax.rsqrt(ms + EPS) * a + m_ref[which, k_shift:k_shift + 1, :])
        return carry
    lax.fori_loop(0, tm // NORM_ROWS, body, 0, unroll=2)


def _adaln_kernel(c_ref, w_ref, b_ref, o_ref):
    cv = c_ref[...]
    s = (cv * _sigmoid(cv)).astype(BF16)
    o_ref[0] = jnp.dot(s, w_ref[0].astype(BF16), preferred_element_type=F32) + b_ref[0]


def _adaln(c_rows, w_mod, b_mod, tn=1024):
    depth, d, n = w_mod.shape
    r = c_rows.shape[0]
    return pl.pallas_call(
        _adaln_kernel,
        grid=(depth, n // tn),
        in_specs=[pl.BlockSpec((r, d), lambda l, j: (0, 0)),
                  pl.BlockSpec((1, d, tn), lambda l, j: (l, 0, j)),
                  pl.BlockSpec((1, 1, tn), lambda l, j: (l, 0, j))],
        out_specs=pl.BlockSpec((1, r, tn), lambda l, j: (l, 0, j)),
        out_shape=jax.ShapeDtypeStruct((depth, r, n), F32),
        compiler_params=_params(("arbitrary", "arbitrary")),
        name="adaln",
    )(c_rows, w_mod, b_mod.reshape(depth, 1, n))


def _is_rope_col(col):
    return any(lo <= col < hi for lo, hi in ROPE_RANGES)


def _inproj_kernel(x_ref, m_ref, g_ref, w_ref, cos_ref, sin_ref, o_ref, xn_ref, *, tm, tn, n_ctx):
    ti = pl.program_id(1)
    j = pl.program_id(2)

    @pl.when(j == 0)
    def _():
        def store(rs, y):
            xn_ref[rs, :] = y.astype(BF16)
        _norm_modulate_rows(x_ref, g_ref, m_ref, 0, ti * tm, tm, n_ctx, store)

    o_ref[0] = jnp.dot(xn_ref[...], w_ref[...].astype(BF16), preferred_element_type=F32)
    lane = lax.broadcasted_iota(jnp.int32, (1, LANES), 1)
    first_half = (lane % HEAD_DIM) < (HEAD_DIM // 2)

    for jj in range(IN_COLS // tn):
        chunks = [cc for cc in range(tn // LANES) if _is_rope_col(jj * tn + cc * LANES)]
        if not chunks:
            continue

        @pl.when(j == jj)
        def _(chunks=chunks):
            def rope_rows(r, carry):
                rs = pl.ds(pl.multiple_of(r * ROPE_ROWS, ROPE_ROWS), ROPE_ROWS)
                cos = cos_ref[rs, :]
                sin = sin_ref[rs, :]
                for cc in chunks:
                    cs = slice(cc * LANES, (cc + 1) * LANES)
                    a = o_ref[0, rs, cs]
                    partner = jnp.where(first_half,
                                        pltpu.roll(a, LANES - HEAD_DIM // 2, 1),
                                        pltpu.roll(a, HEAD_DIM // 2, 1))
                    o_ref[0, rs, cs] = a * cos + partner * sin
                return carry
            lax.fori_loop(0, tm // ROPE_ROWS, rope_rows, 0, unroll=2)


def _inproj(h, modsel, g, w, layer, cos_t, sin_t, n_ctx, tm=768, tn=768):
    b, t, d = h.shape
    kern = functools.partial(_inproj_kernel, tm=tm, tn=tn, n_ctx=n_ctx)
    return pl.pallas_call(
        kern,
        grid=(b, t // tm, IN_COLS // tn),
        in_specs=[pl.BlockSpec((1, tm, d), lambda bi, ti, j: (bi, ti, 0)),
                  pl.BlockSpec((None, 2, 6, d), lambda bi, ti, j: (bi, 0, 0, 0)),
                  pl.BlockSpec((1, d), lambda bi, ti, j: (0, 0)),
                  pl.BlockSpec((None, d, tn), lambda bi, ti, j: (layer, 0, j)),
                  pl.BlockSpec((tm, LANES), lambda bi, ti, j: (ti, 0)),
                  pl.BlockSpec((tm, LANES), lambda bi, ti, j: (ti, 0))],
        out_specs=pl.BlockSpec((1, tm, tn), lambda bi, ti, j: (bi, ti, j)),
        out_shape=jax.ShapeDtypeStruct((b, t, IN_COLS), F32),
        scratch_shapes=[pltpu.VMEM((tm, d), BF16)],
        compiler_params=_params(("arbitrary", "arbitrary", "arbitrary")),
        name="inproj",
    )(h, modsel, g.reshape(1, d), w, cos_t, sin_t)


def _rope_tables(n_ctx, n_lat):
    n_rows = n_lat // GRID_W
    rows = jnp.repeat(jnp.arange(n_rows, dtype=F32), GRID_W)
    cols = jnp.tile(jnp.arange(GRID_W, dtype=F32), n_rows)
    n_freq = HEAD_DIM // 4
    inv_freq = ROPE_BASE ** (-jnp.arange(n_freq, dtype=F32) / n_freq)
    ang = jnp.concatenate([rows[:, None] * inv_freq, cols[:, None] * inv_freq], axis=-1)
    cos, sin = jnp.cos(ang), jnp.sin(ang)
    cos_h = jnp.concatenate([cos, cos], axis=-1)
    sin_h = jnp.concatenate([-sin, sin], axis=-1)
    cos_t = jnp.concatenate([jnp.ones((n_ctx, HEAD_DIM), F32), cos_h], axis=0)
    sin_t = jnp.concatenate([jnp.zeros((n_ctx, HEAD_DIM), F32), sin_h], axis=0)
    rep = LANES // HEAD_DIM
    return jnp.tile(cos_t, (1, rep)), jnp.tile(sin_t, (1, rep))


def _pool_kernel(a_ref, w_ref, s_ref, o_ref, pad_ref, *, segments):
    for g, win in enumerate(POOL_WINDOWS):
        half = win // 2
        cs = slice(g * LANES, (g + 1) * LANES)
        wg = w_ref[g].astype(BF16)
        for s0, n in segments:
            a = a_ref[0, s0:s0 + n, cs]
            pad_ref[0:POOL_PAD, :] = jnp.zeros((POOL_PAD, LANES), F32)
            pad_ref[POOL_PAD:POOL_PAD + n, :] = a
            pad_ref[POOL_PAD + n:2 * POOL_PAD + n, :] = jnp.zeros((POOL_PAD, LANES), F32)
            tot = pad_ref[POOL_PAD - half:POOL_PAD - half + n, :]
            for k in range(-half + 1, half):
                tot = tot + pad_ref[POOL_PAD + k:POOL_PAD + k + n, :]
            t = lax.broadcasted_iota(jnp.int32, (n, 1), 0)
            cnt = (jnp.minimum(t + half, n) - jnp.maximum(t - half, 0)).astype(F32)
            dlt = (tot / cnt - a).astype(BF16)
            y = jnp.dot(dlt, wg, preferred_element_type=F32)
            o_ref[0, s0:s0 + n, cs] = (y * s_ref[:, cs]).astype(BF16)


def _pool(p_all, w, scale, segments):
    b, t, _ = p_all.shape
    max_n = max(n for _, n in segments)
    kern = functools.partial(_pool_kernel, segments=segments)
    return pl.pallas_call(
        kern,
        grid=(b,),
        in_specs=[pl.BlockSpec((1, t, GROUP_WIDTH), lambda bi: (bi, 0, COL_POOL // GROUP_WIDTH)),
                  pl.BlockSpec(w.shape, lambda bi: (0, 0, 0)),
                  pl.BlockSpec((1, GROUP_WIDTH), lambda bi: (0, 0))],
        out_specs=pl.BlockSpec((1, t, GROUP_WIDTH), lambda bi: (bi, 0, 0)),
        out_shape=jax.ShapeDtypeStruct((b, t, GROUP_WIDTH), BF16),
        scratch_shapes=[pltpu.VMEM((max_n + 2 * POOL_PAD, LANES), F32)],
        compiler_params=_params(("arbitrary",)),
        name="pool",
    )(p_all, w, scale.reshape(1, GROUP_WIDTH))


def _diff_kernel(lam_ref, q_ref, k_ref, v_ref, g_ref, o_ref, kt_ref, vb_ref, *, tq, n_ctx, out_scale):
    ti = pl.program_id(2)
    lam = lam_ref[0]

    @pl.when(ti == 0)
    def _():
        kt_ref[...] = k_ref[0].T.astype(BF16)
        vb_ref[...] = v_ref[0].astype(BF16)

    lane = lax.broadcasted_iota(jnp.int32, (1, 2 * HEAD_DIM), 1)
    q = q_ref[0] * (HEAD_DIM ** -0.5)
    q1 = jnp.where(lane < HEAD_DIM, q, 0.0).astype(BF16)
    q2 = jnp.where(lane >= HEAD_DIM, q, 0.0).astype(BF16)

    def attend(n_keys):
        kt = kt_ref[:, 0:n_keys]
        v = vb_ref[0:n_keys, :]
        s1 = jnp.dot(q1, kt, preferred_element_type=F32)
        s2 = jnp.dot(q2, kt, preferred_element_type=F32)
        e1 = jnp.exp(s1 - jnp.max(s1, axis=-1, keepdims=True))
        e2 = jnp.exp(s2 - jnp.max(s2, axis=-1, keepdims=True))
        r1 = 1.0 / jnp.sum(e1, axis=-1, keepdims=True)
        r2 = lam / jnp.sum(e2, axis=-1, keepdims=True)
        o = (jnp.dot(e1.astype(BF16), v, preferred_element_type=F32) * r1
             - jnp.dot(e2.astype(BF16), v, preferred_element_type=F32) * r2)
        ms = jnp.mean(o * o, axis=-1, keepdims=True)
        o_ref[0] = (o * lax.rsqrt(ms + EPS) * g_ref[...] * out_scale).astype(BF16)

    @pl.when(ti * tq < n_ctx)
    def _():
        attend(n_ctx)

    @pl.when(ti * tq >= n_ctx)
    def _():
        attend(k_ref.shape[1])


def _diff(p_all, lam, subln, n_ctx, lam_init, tq=256):
    b, t, _ = p_all.shape
    assert n_ctx % tq == 0 and t % tq == 0
    w = 2 * HEAD_DIM
    kern = functools.partial(_diff_kernel, tq=tq, n_ctx=n_ctx, out_scale=1.0 - lam_init)
    return pl.pallas_call(
        kern,
        grid=(b, DIFF_HEADS, t // tq),
        in_specs=[pl.BlockSpec(memory_space=pltpu.SMEM),
                  pl.BlockSpec((1, tq, w), lambda bi, hi, ti: (bi, ti, COL_DQ // w + hi)),
                  pl.BlockSpec((1, t, w), lambda bi, hi, ti: (bi, 0, COL_DK // w + hi)),
                  pl.BlockSpec((1, t, w), lambda bi, hi, ti: (bi, 0, COL_DV // w + hi)),
                  pl.BlockSpec((1, w), lambda bi, hi, ti: (0, 0))],
        out_specs=pl.BlockSpec((1, tq, w), lambda bi, hi, ti: (bi, ti, hi)),
        out_shape=jax.ShapeDtypeStruct((b, t, GROUP_WIDTH), BF16),
        scratch_shapes=[pltpu.VMEM((w, t), BF16), pltpu.VMEM((t, w), BF16)],
        compiler_params=_params(("arbitrary", "arbitrary", "arbitrary")),
        name="diff_attn",
    )(lam.reshape(1), p_all, p_all, p_all, subln.reshape(1, w))


def _gelu_tanh(x):
    return 0.5 * x * (1.0 + jnp.tanh(math.sqrt(2.0 / math.pi) * (x + 0.044715 * (x * x * x))))


def _sgu_kernel(z_ref, g_ref, b_ref, w_ref, bs_ref, o_ref, *, tm):
    z = _gelu_tanh(z_ref[0])
    u = z[:, :GROUP_WIDTH]
    v = z[:, GROUP_WIDTH:]
    mu = jnp.mean(v, axis=-1, keepdims=True)
    var = jnp.mean(jnp.square(v - mu), axis=-1, keepdims=True)
    vn = ((v - mu) * lax.rsqrt(var + EPS) * g_ref[...] + b_ref[...]).astype(BF16)
    for g in range(SGU_GROUPS):
        wg = w_ref[g].astype(BF16)
        bias = bs_ref[:, g:g + 1]
        cs = slice(g * LANES, (g + 1) * LANES)
        for c in range(tm // SGU_CHUNK):
            rs = slice(c * SGU_CHUNK, (c + 1) * SGU_CHUNK)
            sv = jnp.dot(wg, vn[rs, cs], preferred_element_type=F32) + bias
            o_ref[0, rs, cs] = (u[rs, cs] * sv).astype(BF16)


def _sgu(p_all, ln_g, ln_b, w_s, b_s, tm=768):
    b, t, _ = p_all.shape
    zw = 2 * GROUP_WIDTH
    kern = functools.partial(_sgu_kernel, tm=tm)
    return pl.pallas_call(
        kern,
        grid=(b, t // tm),
        in_specs=[pl.BlockSpec((1, tm, zw), lambda bi, ti: (bi, ti, COL_Z // zw)),
                  pl.BlockSpec((1, GROUP_WIDTH), lambda bi, ti: (0, 0)),
                  pl.BlockSpec((1, GROUP_WIDTH), lambda bi, ti: (0, 0)),
                  pl.BlockSpec(w_s.shape, lambda bi, ti: (0, 0, 0)),
                  pl.BlockSpec((SGU_CHUNK, SGU_GROUPS), lambda bi, ti: (0, 0))],
        out_specs=pl.BlockSpec((1, tm, GROUP_WIDTH), lambda bi, ti: (bi, ti, 0)),
        out_shape=jax.ShapeDtypeStruct((b, t, GROUP_WIDTH), BF16),
        compiler_params=_params(("arbitrary", "arbitrary")),
        name="sgu",
    )(p_all, ln_g.reshape(1, GROUP_WIDTH), ln_b.reshape(1, GROUP_WIDTH), w_s, b_s.T)


def _win_band_start(step, t):
    return jnp.clip(step * WIN_QB - WIN_BLOCK, 0, t - WIN_BAND)


def _win_bias(n_ctx, t):
    step = jnp.arange(t // WIN_QB, dtype=jnp.int32)[:, None, None]
    qpos = step * WIN_QB + jnp.arange(WIN_QB, dtype=jnp.int32)[None, :, None]
    kpos = _win_band_start(step, t) + jnp.arange(WIN_BAND, dtype=jnp.int32)[None, None, :]
    valid = (kpos >= n_ctx) & (qpos >= n_ctx) & (jnp.abs(kpos - qpos) <= WINDOW)
    return jnp.where(valid, 0.0, NEG_INF).astype(F32)


def _win_kernel(sink_ref, q_ref, k_ref, v_ref, bias_ref, o_ref, kt_ref, vv_ref, *, n_ctx):
    g = pl.program_id(1)
    n = pl.program_id(2)
    t = k_ref.shape[1]
    blk = WIN_QB
    band = WIN_BAND
    lane = lax.broadcasted_iota(jnp.int32, (1, LANES), 1)
    low = lane < HEAD_DIM

    @pl.when(n == 0)
    def _():
        mine = (lane >= g * HEAD_DIM) & (lane < (g + 1) * HEAD_DIM)
        km = jnp.where(mine, k_ref[0], 0.0)
        vm = jnp.where(mine, v_ref[0], 0.0)
        kt_ref[...] = (km + pltpu.roll(km, HEAD_DIM, 1)).T.astype(BF16)
        vv_ref[...] = (vm + pltpu.roll(vm, HEAD_DIM, 1)).astype(BF16)

    rows = WIN_REP * blk
    head = lax.broadcasted_iota(jnp.int32, (rows, 1), 0) // blk
    start = pl.multiple_of(_win_band_start(n, t), WIN_BLOCK)

    q = q_ref[0] * (HEAD_DIM ** -0.5)
    pieces = []
    for pair in range(WIN_REP // 2):
        qp = q[:, pair * LANES:(pair + 1) * LANES]
        pieces += [jnp.where(low, qp, 0.0), jnp.where(low, 0.0, qp)]
    qs = jnp.concatenate(pieces, axis=0).astype(BF16)
    s_c = jnp.dot(qs, kt_ref[:, 0:n_ctx], preferred_element_type=F32)
    s_b = jnp.dot(qs, kt_ref[:, pl.ds(start, band)], preferred_element_type=F32)
    s_b = (s_b.reshape(WIN_REP, blk, band) + bias_ref[...]).reshape(rows, band)
    sink = jnp.full((rows, 1), sink_ref[g * WIN_REP], F32)
    for r in range(1, WIN_REP):
        sink = jnp.where(head == r, sink_ref[g * WIN_REP + r], sink)
    m = jnp.maximum(jnp.maximum(jnp.max(s_c, axis=-1, keepdims=True),
                                jnp.max(s_b, axis=-1, keepdims=True)), sink)
    e_c = jnp.exp(s_c - m)
    e_b = jnp.exp(s_b - m)
    den = jnp.sum(e_c, axis=-1, keepdims=True) + jnp.sum(e_b, axis=-1, keepdims=True) + jnp.exp(sink - m)
    o = (jnp.dot(e_c.astype(BF16), vv_ref[0:n_ctx, :], preferred_element_type=F32)
         + jnp.dot(e_b.astype(BF16), vv_ref[pl.ds(start, band), :], preferred_element_type=F32)) / den
    for pair in range(WIN_REP // 2):
        lo = o[(2 * pair) * blk:(2 * pair + 1) * blk]
        hi = o[(2 * pair + 1) * blk:(2 * pair + 2) * blk]
        o_ref[0, :, pair * LANES:(pair + 1) * LANES] = jnp.where(low, lo, hi).astype(BF16)


def _win(p_all, sink, n_ctx):
    b, t, _ = p_all.shape
    qw = WIN_REP * HEAD_DIM
    assert t % WIN_QB == 0 and t >= WIN_BAND
    lo = -(-(n_ctx + WIN_BLOCK) // WIN_QB)
    hi = (t - WIN_BAND + WIN_BLOCK) // WIN_QB

    def bias_block(bi, gi, ni):
        return (jnp.where((ni >= lo) & (ni <= hi), lo, ni), 0, 0)

    kern = functools.partial(_win_kernel, n_ctx=n_ctx)
    return pl.pallas_call(
        kern,
        grid=(b, WIN_KV_HEADS, t // WIN_QB),
        in_specs=[pl.BlockSpec(memory_space=pltpu.SMEM),
                  pl.BlockSpec((1, WIN_QB, qw), lambda bi, gi, ni: (bi, ni, COL_WQ // qw + gi)),
                  pl.BlockSpec((1, t, LANES), lambda bi, gi, ni: (bi, 0, COL_WK // LANES)),
                  pl.BlockSpec((1, t, LANES), lambda bi, gi, ni: (bi, 0, COL_WV // LANES)),
                  pl.BlockSpec((1, WIN_QB, WIN_BAND), bias_block)],
        out_specs=pl.BlockSpec((1, WIN_QB, qw), lambda bi, gi, ni: (bi, ni, gi)),
        out_shape=jax.ShapeDtypeStruct((b, t, GROUP_WIDTH), BF16),
        scratch_shapes=[pltpu.VMEM((LANES, t), BF16), pltpu.VMEM((t, LANES), BF16)],
        compiler_params=_params(("arbitrary", "arbitrary", "arbitrary")),
        name="win_attn",
    )(sink, p_all, p_all, p_all, _win_bias(n_ctx, t))


def _outproj_kernel(a_ref, b_ref, c_ref, d_ref, w_ref, h_ref, m_ref, o_ref, wb_ref, *, tm, n_ctx):
    ti = pl.program_id(2)

    @pl.when((pl.program_id(1) == 0) & (ti == 0))
    def _():
        wb_ref[...] = w_ref[...].astype(BF16)

    y = None
    for i, r in enumerate((a_ref, b_ref, c_ref, d_ref)):
        part = jnp.dot(r[0], wb_ref[i * GROUP_WIDTH:(i + 1) * GROUP_WIDTH, :], preferred_element_type=F32)
        y = part if y is None else y + part
    gate = _mod_row(m_ref, 2, _row_is_ctx(ti, tm, n_ctx))
    o_ref[0] = h_ref[0] + gate * y


def _outproj(mixes, w, layer, h, modsel, n_ctx, tm=768, tn=1024):
    b, t, d = h.shape
    tn = min(tn, d)
    assert d % tn == 0 and t % tm == 0
    kern = functools.partial(_outproj_kernel, tm=tm, n_ctx=n_ctx)
    mix_spec = pl.BlockSpec((1, tm, GROUP_WIDTH), lambda j, bi, ti: (bi, ti, 0))
    return pl.pallas_call(
        kern,
        grid=(d // tn, b, t // tm),
        in_specs=[mix_spec, mix_spec, mix_spec, mix_spec,
                  pl.BlockSpec((None, 4 * GROUP_WIDTH, tn), lambda j, bi, ti: (layer, 0, j)),
                  pl.BlockSpec((1, tm, tn), lambda j, bi, ti: (bi, ti, j)),
                  pl.BlockSpec((None, 2, 6, tn), lambda j, bi, ti: (bi, 0, 0, j))],
        out_specs=pl.BlockSpec((1, tm, tn), lambda j, bi, ti: (bi, ti, j)),
        out_shape=jax.ShapeDtypeStruct((b, t, d), F32),
        scratch_shapes=[pltpu.VMEM((4 * GROUP_WIDTH, tn), BF16)],
        compiler_params=_params(("arbitrary", "arbitrary", "arbitrary")),
        name="outproj",
    )(*mixes, w, h, modsel)


def _ffn_in_kernel(x_ref, m_ref, g_ref, o_ref, *, tm, n_ctx, row0):
    def store(rs, y):
        o_ref[0, rs, :] = y
    _norm_modulate_rows(x_ref, g_ref, m_ref, 3, row0 + pl.program_id(1) * tm, tm, n_ctx, store)


def _route_kernel(x_ref, m_ref, g_ref, r_ref, o_ref, gate_ref, sel_ref, *, tm, n_ctx, row0):
    _ffn_in_kernel(x_ref, m_ref, g_ref, o_ref, tm=tm, n_ctx=n_ctx, row0=row0)
    f = o_ref[0]
    lane = lax.broadcasted_iota(jnp.int32, (1, LANES), 1)
    logits = jnp.full((tm, LANES), -jnp.inf, F32)
    for e in range(N_EXPERTS):
        logits = jnp.where(lane == e, jnp.sum(f * r_ref[e:e + 1, :], axis=-1, keepdims=True), logits)
    m1 = jnp.max(logits, axis=-1, keepdims=True)
    i1 = jnp.min(jnp.where(logits == m1, lane, LANES), axis=-1, keepdims=True)
    rest = jnp.where(lane == i1, -jnp.inf, logits)
    m2 = jnp.max(rest, axis=-1, keepdims=True)
    i2 = jnp.min(jnp.where(rest == m2, lane, LANES), axis=-1, keepdims=True)
    e2 = jnp.exp(m2 - m1)
    w1 = 1.0 / (1.0 + e2)
    w2 = e2 / (1.0 + e2)
    gate_ref[0] = jnp.where(lane == i1, w1, 0.0) + jnp.where(lane == i2, w2, 0.0)
    sel_ref[0] = ((lane == i1) | (lane == i2)).astype(jnp.int32)


def _ffn_in_route(h, modsel, g, n_ctx, row0, router, tm=256):
    b, t, d = h.shape
    nt = (t - row0) // tm
    off = row0 // tm
    in_specs = [pl.BlockSpec((1, tm, d), lambda bi, ti: (bi, ti + off, 0)),
                pl.BlockSpec((None, 2, 6, d), lambda bi, ti: (bi, 0, 0, 0)),
                pl.BlockSpec((1, d), lambda bi, ti: (0, 0))]
    f_spec = pl.BlockSpec((1, tm, d), lambda bi, ti: (bi, ti, 0))
    f_shape = jax.ShapeDtypeStruct((b, t - row0, d), F32)
    assert router.shape == (d, N_EXPERTS)
    e_spec = pl.BlockSpec((1, tm, LANES), lambda bi, ti: (bi, ti, 0))
    kern = functools.partial(_route_kernel, tm=tm, n_ctx=n_ctx, row0=row0)
    return pl.pallas_call(
        kern, grid=(b, nt),
        in_specs=in_specs + [pl.BlockSpec((N_EXPERTS, d), lambda bi, ti: (0, 0))],
        out_specs=[f_spec, e_spec, e_spec],
        out_shape=[f_shape, jax.ShapeDtypeStruct((b, t - row0, LANES), F32),
                   jax.ShapeDtypeStruct((b, t - row0, LANES), jnp.int32)],
        compiler_params=_params(("arbitrary", "arbitrary")), name="ffn_in_route",
    )(h, modsel, g.reshape(1, d), router.T)


FFN_TM = 1152
FFN_SUB = 192
FFN_VMEM_LIMIT = 60 * 1024 * 1024


def _swiglu_slice(xs, w1, w3, w2):
    h1 = jnp.dot(xs, w1.astype(BF16), preferred_element_type=F32)
    h3 = jnp.dot(xs, w3.astype(BF16), preferred_element_type=F32)
    act = (h1 * _sigmoid(h1) * h3).astype(BF16)
    return jnp.dot(act, w2.astype(BF16), preferred_element_type=F32)


def _ffn_kernel(te_ref, tr_ref, x_ref, w1_ref, w3_ref, w2_ref, o_ref, xs_ref):
    i = pl.program_id(0)
    j = pl.program_id(1)
    n_blocks = (tr_ref[i] + FFN_SUB - 1) // FFN_SUB

    @pl.when(j == 0)
    def _():
        o_ref[...] = jnp.zeros_like(o_ref)
        xs_ref[...] = x_ref[...].astype(BF16)

    for nb in range(1, FFN_TM // FFN_SUB + 1):
        @pl.when(n_blocks == nb)
        def _(nb=nb):
            m = nb * FFN_SUB
            o_ref[0:m, :] += _swiglu_slice(xs_ref[0:m, :], w1_ref[0], w3_ref[0], w2_ref[0])


def _ffn(x_rows, w1, w3, w2, tile_expert, tile_rows, tf=256):
    nr, d = x_rows.shape
    n_exp, _, f = w1.shape
    nj = f // tf

    def wcol(i, j, te, tr):
        return (te[i], 0, jnp.where(tr[i] > 0, j, nj - 1))

    def wrow(i, j, te, tr):
        return (te[i], jnp.where(tr[i] > 0, j, nj - 1), 0)

    return pl.pallas_call(
        _ffn_kernel,
        grid_spec=pltpu.PrefetchScalarGridSpec(
            num_scalar_prefetch=2,
            grid=(nr // FFN_TM, nj),
            in_specs=[pl.BlockSpec((FFN_TM, d), lambda i, j, te, tr: (i, 0)),
                      pl.BlockSpec((1, d, tf), wcol),
                      pl.BlockSpec((1, d, tf), wcol),
                      pl.BlockSpec((1, tf, d), wrow)],
            out_specs=pl.BlockSpec((FFN_TM, d), lambda i, j, te, tr: (i, 0)),
            scratch_shapes=[pltpu.VMEM((FFN_TM, d), BF16)]),
        out_shape=jax.ShapeDtypeStruct((nr, d), F32),
        compiler_params=_params(("arbitrary", "arbitrary"), FFN_VMEM_LIMIT),
        name="ffn",
    )(tile_expert, tile_rows, x_rows, w1, w3, w2)


def _ffn_dense_kernel(h_ref, m_ref, g_ref, w1_ref, w3_ref, w2_ref, o_ref, xs_ref, *, tm, n_ctx):
    ti = pl.program_id(1)
    j = pl.program_id(2)

    @pl.when(j == 0)
    def _():
        def store(rs, y):
            xs_ref[rs, :] = y.astype(BF16)
        _norm_modulate_rows(h_ref, g_ref, m_ref, 3, ti * tm, tm, n_ctx, store)
        o_ref[...] = jnp.zeros_like(o_ref)

    o_ref[0] += _swiglu_slice(xs_ref[...], w1_ref[...], w3_ref[...], w2_ref[...])

    @pl.when(j == pl.num_programs(2) - 1)
    def _():
        gate = _mod_row(m_ref, 5, _row_is_ctx(ti, tm, n_ctx))
        o_ref[0] = h_ref[0] + gate * o_ref[0]


def _ffn_dense(h, modsel, g, w1, w3, w2, layer, n_ctx, tf=256):
    b, t, d = h.shape
    f = w1.shape[2]
    tm = t // 2
    assert t % tm == 0 and f % tf == 0 and tm % NORM_ROWS == 0 and n_ctx % NORM_ROWS == 0
    kern = functools.partial(_ffn_dense_kernel, tm=tm, n_ctx=n_ctx)
    return pl.pallas_call(
        kern, grid=(b, t // tm, f // tf),
        in_specs=[pl.BlockSpec((1, tm, d), lambda bi, ti, j: (bi, ti, 0)),
                  pl.BlockSpec((None, 2, 6, d), lambda bi, ti, j: (bi, 0, 0, 0)),
                  pl.BlockSpec((1, d), lambda bi, ti, j: (0, 0)),
                  pl.BlockSpec((None, d, tf), lambda bi, ti, j: (layer, 0, j)),
                  pl.BlockSpec((None, d, tf), lambda bi, ti, j: (layer, 0, j)),
                  pl.BlockSpec((None, tf, d), lambda bi, ti, j: (layer, j, 0))],
        out_specs=pl.BlockSpec((1, tm, d), lambda bi, ti, j: (bi, ti, 0), pipeline_mode=pl.Buffered(1)),
        out_shape=jax.ShapeDtypeStruct((b, t, d), F32),
        scratch_shapes=[pltpu.VMEM((tm, d), BF16)],
        compiler_params=_params(("arbitrary", "arbitrary", "arbitrary")), name="ffn_dense",
    )(h, modsel, g.reshape(1, d), w1, w3, w2)


def _moe_final_kernel(h_ref, y0_ref, y1_ref, w0_ref, w1_ref, m_ref, g_ref, o_ref):
    y = w0_ref[0] * y0_ref[0] + w1_ref[0] * y1_ref[0]
    hn = h_ref[0] + m_ref[1, 5:6, :] * y
    ms = jnp.mean(hn * hn, axis=-1, keepdims=True)
    o_ref[0] = hn * lax.rsqrt(ms + EPS) * g_ref[...]


def _moe_final(h, y0, y1, w0, w1, modsel, g, row0, tm=256):
    b, t, d = h.shape
    n_lat = t - row0
    off = row0 // tm
    spec = pl.BlockSpec((1, tm, d), lambda bi, ti: (bi, ti, 0))
    wspec = pl.BlockSpec((1, tm, 1), lambda bi, ti: (bi, ti, 0))
    return pl.pallas_call(
        _moe_final_kernel, grid=(b, n_lat // tm),
        in_specs=[pl.BlockSpec((1, tm, d), lambda bi, ti: (bi, ti + off, 0)), spec, spec, wspec, wspec,
                  pl.BlockSpec((None, 2, 6, d), lambda bi, ti: (bi, 0, 0, 0)),
                  pl.BlockSpec((1, d), lambda bi, ti: (0, 0))],
        out_specs=spec, out_shape=jax.ShapeDtypeStruct((b, n_lat, d), F32),
        compiler_params=_params(("arbitrary", "arbitrary")), name="moe_final",
    )(h, y0, y1, w0, w1, modsel, g.reshape(1, d))


def _routing_tables(sel, gates, n_tiles):
    n, n_exp = sel.shape
    seli = sel.astype(jnp.int32)
    counts = jnp.sum(seli, axis=0)
    rank = jnp.cumsum(seli, axis=0) - seli
    tiles_e = (counts + FFN_TM - 1) // FFN_TM
    tile_end = jnp.cumsum(tiles_e)
    tile_start = tile_end - tiles_e
    pos = tile_start[None, :] * FFN_TM + rank
    n_rows = n_tiles * FFN_TM
    expert = jnp.arange(n_exp, dtype=jnp.int32)[None, :]
    e_lo = jnp.min(jnp.where(sel, expert, n_exp), axis=1, keepdims=True)
    e_hi = jnp.max(jnp.where(sel, expert, -1), axis=1, keepdims=True)
    pick = lambda a, e: jnp.sum(jnp.where(expert == e, a, 0), axis=1)
    pos0, pos1 = pick(pos, e_lo), pick(pos, e_hi)
    w0, w1 = pick(gates, e_lo), pick(gates, e_hi)
    tok = jnp.arange(n, dtype=jnp.int32)
    src = (jnp.arange(n_rows, dtype=jnp.int32) % n).at[jnp.concatenate([pos0, pos1])].set(
        jnp.concatenate([tok, tok]), mode="promise_in_bounds", unique_indices=True)

    tile = jnp.arange(n_tiles, dtype=jnp.int32)
    te = jnp.minimum(jnp.sum((tile_end[None, :] <= tile[:, None]).astype(jnp.int32), axis=1), n_exp - 1)
    tr = jnp.clip(counts[te] - (tile - tile_start[te]) * FFN_TM, 0, FFN_TM)
    tr = jnp.where(tile < tile_end[-1], tr, 0)
    return src, pos0, pos1, w0, w1, te.astype(jnp.int32), tr.astype(jnp.int32)


def kernel(x, c, ctx, c_ctx, w_mod, b_mod, norm_mix, norm_ffn, w_in, w_out, pool_w, pool_scale, diff_lam,
           diff_subln, sgu_ln_g, sgu_ln_b, sgu_w, sgu_b, win_sink, ffn_w1, ffn_w3, ffn_w2, moe_router,
           moe_w1, moe_w3, moe_w2, norm_final):
    b, n_lat, d = x.shape
    n_ctx = ctx.shape[1]
    t = n_ctx + n_lat
    depth = w_mod.shape[0]
    segments = ((0, n_ctx), (n_ctx, n_lat))

    c_rows = jnp.zeros((8, d), F32).at[:b].set(c).at[b].set(c_ctx)
    mod = _adaln(c_rows, w_mod, b_mod).reshape(depth, 8, 6, d)
    cos_t, sin_t = _rope_tables(n_ctx, n_lat)
    h = jnp.concatenate([ctx, x], axis=1)

    out = None
    for l in range(depth):
        last = l == depth - 1
        lat_m = mod[l, :b]
        ctx_m = jnp.broadcast_to(mod[l, b:b + 1], lat_m.shape)
        modsel = jnp.stack([ctx_m, lat_m], axis=1)

        p_all = _inproj(h, modsel, norm_mix[l], w_in, l, cos_t, sin_t, n_ctx)

        lam_init = 0.8 - 0.6 * math.exp(-0.3 * l)
        lq1, lk1, lq2, lk2 = diff_lam[l].astype(F32)
        lam = jnp.exp(jnp.sum(lq1 * lk1)) - jnp.exp(jnp.sum(lq2 * lk2)) + lam_init

        mixes = (_pool(p_all, pool_w[l], pool_scale[l], segments),
                 _diff(p_all, lam, diff_subln[l], n_ctx, lam_init),
                 _sgu(p_all, sgu_ln_g[l], sgu_ln_b[l], sgu_w[l], sgu_b[l]),
                 _win(p_all, win_sink[l], n_ctx))
        h = _outproj(mixes, w_out, l, h, modsel, n_ctx)

        i = l // 2
        row0 = n_ctx if last else 0
        rows = b * (t - row0)
        if l % 2 == 0:
            assert not last
            h = _ffn_dense(h, modsel, norm_ffn[l], ffn_w1, ffn_w3, ffn_w2, i, n_ctx)
        else:
            assert last
            f, gates, sel = _ffn_in_route(h, modsel, norm_ffn[l], n_ctx, row0, moe_router[i])
            f = f.reshape(rows, d)
            n_exp = moe_w1.shape[1]
            n_tiles = (2 * rows) // FFN_TM + n_exp
            src, pos0, pos1, w0, w1, te, tr = _routing_tables(
                sel.reshape(rows, LANES)[:, :n_exp] > 0, gates.reshape(rows, LANES)[:, :n_exp], n_tiles)
            rows_of = lambda a, idx: a.at[idx].get(mode="promise_in_bounds")
            ys = _ffn(rows_of(f, src), moe_w1[i], moe_w3[i], moe_w2[i], te, tr)
            shp = (b, t - row0, d)
            out = _moe_final(h, rows_of(ys, pos0).reshape(shp), rows_of(ys, pos1).reshape(shp),
                             w0.reshape(b, t - row0, 1), w1.reshape(b, t - row0, 1), modsel, norm_final, row0)
    return out
```

```python
import functools
import math

import jax
import jax.numpy as jnp
from jax import lax
from jax.experimental import pallas as pl
from jax.experimental.pallas import tpu as pltpu

F32 = jnp.float32
BF16 = jnp.bfloat16

GRID_W = 64
GROUP_WIDTH = 512
HEAD_DIM = 64
ROPE_BASE = 10000.0
EPS = 1e-6
NEG_INF = -1e30
LANES = 128

POOL_WINDOWS = (2, 4, 8, 16)
POOL_PAD = 16
DIFF_HEADS = GROUP_WIDTH // (2 * HEAD_DIM)
SGU_CHUNK = 128
SGU_GROUPS = 4
WIN_HEADS = GROUP_WIDTH // HEAD_DIM
WIN_KV_HEADS = 2
WIN_REP = WIN_HEADS // WIN_KV_HEADS
WINDOW = 128
WIN_BLOCK = 128
WIN_QB = 2 * WIN_BLOCK
WIN_BAND = WIN_QB + 2 * WIN_BLOCK
N_EXPERTS = 8

COL_POOL = 0
COL_DQ = GROUP_WIDTH
COL_DK = 2 * GROUP_WIDTH
COL_DV = 3 * GROUP_WIDTH
COL_Z = 4 * GROUP_WIDTH
COL_WQ = 6 * GROUP_WIDTH
COL_WK = COL_WQ + WIN_HEADS * HEAD_DIM
COL_WV = COL_WK + WIN_KV_HEADS * HEAD_DIM
IN_COLS = COL_WV + WIN_KV_HEADS * HEAD_DIM

VMEM_LIMIT = 56 * 1024 * 1024
NORM_ROWS = 32


def _params(sem, vmem=VMEM_LIMIT):
    return pltpu.CompilerParams(dimension_semantics=sem, vmem_limit_bytes=vmem)


def _sigmoid(x):
    return 1.0 / (1.0 + jnp.exp(-x))


def _row_is_ctx(tile_idx, tm, n_ctx):
    rows = tile_idx * tm + lax.broadcasted_iota(jnp.int32, (tm, 1), 0)
    return rows < n_ctx


def _mod_row(m_ref, k, is_ctx):
    return jnp.where(is_ctx, m_ref[0, k:k + 1, :], m_ref[1, k:k + 1, :])


def _norm_modulate_rows(x_ref, g_ref, m_ref, k_shift, first_row, tm, n_ctx, store):
    def body(r, carry):
        r0 = pl.multiple_of(r * NORM_ROWS, NORM_ROWS)
        which = ((first_row + r0) >= n_ctx).astype(jnp.int32)
        rs = pl.ds(r0, NORM_ROWS)
        x = x_ref[0, rs, :]
        ms = jnp.mean(x * x, axis=-1, keepdims=True)
        a = g_ref[...] * (1.0 + m_ref[which, k_shift + 1:k_shift + 2, :])
        store(rs, x * lax.rsqrt(ms + EPS) * a + m_ref[which, k_shift:k_shift + 1, :])
        return carry
    lax.fori_loop(0, tm // NORM_ROWS, body, 0, unroll=2)


def _adaln_kernel(c_ref, w_ref, b_ref, o_ref):
    cv = c_ref[...]
    s = (cv * _sigmoid(cv)).astype(BF16)
    o_ref[0] = jnp.dot(s, w_ref[0].astype(BF16), preferred_element_type=F32) + b_ref[0]


def _adaln(c_rows, w_mod, b_mod, tn=1024):
    depth, d, n = w_mod.shape
    r = c_rows.shape[0]
    return pl.pallas_call(
        _adaln_kernel,
        grid=(depth, n // tn),
        in_specs=[pl.BlockSpec((r, d), lambda l, j: (0, 0)),
                  pl.BlockSpec((1, d, tn), lambda l, j: (l, 0, j)),
                  pl.BlockSpec((1, 1, tn), lambda l, j: (l, 0, j))],
        out_specs=pl.BlockSpec((1, r, tn), lambda l, j: (l, 0, j)),
        out_shape=jax.ShapeDtypeStruct((depth, r, n), F32),
        compiler_params=_params(("arbitrary", "arbitrary")),
        name="adaln",
    )(c_rows, w_mod, b_mod.reshape(depth, 1, n))


def _inproj_kernel(x_ref, m_ref, g_ref, w_ref, o_ref, xn_ref, *, tm, n_ctx):
    @pl.when(pl.program_id(2) == 0)
    def _():
        def store(rs, y):
            xn_ref[rs, :] = y.astype(BF16)
        _norm_modulate_rows(x_ref, g_ref, m_ref, 0, pl.program_id(1) * tm, tm, n_ctx, store)

    o_ref[0] = jnp.dot(xn_ref[...], w_ref[...].astype(BF16), preferred_element_type=F32)


def _rope(a, cos, sin):
    lane = lax.broadcasted_iota(jnp.int32, (1, LANES), 1)
    first_half = (lane % HEAD_DIM) < (HEAD_DIM // 2)
    partner = jnp.where(first_half, pltpu.roll(a, LANES - HEAD_DIM // 2, 1), pltpu.roll(a, HEAD_DIM // 2, 1))
    return a * cos + partner * sin


def _inproj(h, modsel, g, w, layer, n_ctx, tm=768, tn=1280):
    b, t, d = h.shape
    kern = functools.partial(_inproj_kernel, tm=tm, n_ctx=n_ctx)
    return pl.pallas_call(
        kern,
        grid=(b, t // tm, IN_COLS // tn),
        in_specs=[pl.BlockSpec((1, tm, d), lambda bi, ti, j: (bi, ti, 0)),
                  pl.BlockSpec((None, 2, 6, d), lambda bi, ti, j: (bi, 0, 0, 0)),
                  pl.BlockSpec((1, d), lambda bi, ti, j: (0, 0)),
                  pl.BlockSpec((None, d, tn), lambda bi, ti, j: (layer, 0, j))],
        out_specs=pl.BlockSpec((1, tm, tn), lambda bi, ti, j: (bi, ti, j)),
        out_shape=jax.ShapeDtypeStruct((b, t, IN_COLS), F32),
        scratch_shapes=[pltpu.VMEM((tm, d), BF16)],
        compiler_params=_params(("arbitrary", "arbitrary", "arbitrary")),
        name="inproj",
    )(h, modsel, g.reshape(1, d), w)


def _rope_tables(n_ctx, n_lat):
    n_rows = n_lat // GRID_W
    rows = jnp.repeat(jnp.arange(n_rows, dtype=F32), GRID_W)
    cols = jnp.tile(jnp.arange(GRID_W, dtype=F32), n_rows)
    n_freq = HEAD_DIM // 4
    inv_freq = ROPE_BASE ** (-jnp.arange(n_freq, dtype=F32) / n_freq)
    ang = jnp.concatenate([rows[:, None] * inv_freq, cols[:, None] * inv_freq], axis=-1)
    cos, sin = jnp.cos(ang), jnp.sin(ang)
    cos_h = jnp.concatenate([cos, cos], axis=-1)
    sin_h = jnp.concatenate([-sin, sin], axis=-1)
    cos_t = jnp.concatenate([jnp.ones((n_ctx, HEAD_DIM), F32), cos_h], axis=0)
    sin_t = jnp.concatenate([jnp.zeros((n_ctx, HEAD_DIM), F32), sin_h], axis=0)
    rep = LANES // HEAD_DIM
    return jnp.tile(cos_t, (1, rep)), jnp.tile(sin_t, (1, rep))


def _pool_kernel(a_ref, w_ref, s_ref, o_ref, pad_ref, *, segments):
    for g, win in enumerate(POOL_WINDOWS):
        half = win // 2
        cs = slice(g * LANES, (g + 1) * LANES)
        wg = w_ref[g].astype(BF16)
        for s0, n in segments:
            a = a_ref[0, s0:s0 + n, cs]
            pad_ref[0:POOL_PAD, :] = jnp.zeros((POOL_PAD, LANES), F32)
            pad_ref[POOL_PAD:POOL_PAD + n, :] = a
            pad_ref[POOL_PAD + n:2 * POOL_PAD + n, :] = jnp.zeros((POOL_PAD, LANES), F32)
            tot = pad_ref[POOL_PAD - half:POOL_PAD - half + n, :]
            for k in range(-half + 1, half):
                tot = tot + pad_ref[POOL_PAD + k:POOL_PAD + k + n, :]
            t = lax.broadcasted_iota(jnp.int32, (n, 1), 0)
            cnt = (jnp.minimum(t + half, n) - jnp.maximum(t - half, 0)).astype(F32)
            dlt = (tot / cnt - a).astype(BF16)
            y = jnp.dot(dlt, wg, preferred_element_type=F32)
            o_ref[0, s0:s0 + n, cs] = (y * s_ref[:, cs]).astype(BF16)


def _pool(p_all, w, scale, segments):
    b, t, _ = p_all.shape
    max_n = max(n for _, n in segments)
    kern = functools.partial(_pool_kernel, segments=segments)
    return pl.pallas_call(
        kern,
        grid=(b,),
        in_specs=[pl.BlockSpec((1, t, GROUP_WIDTH), lambda bi: (bi, 0, COL_POOL // GROUP_WIDTH)),
                  pl.BlockSpec(w.shape, lambda bi: (0, 0, 0)),
                  pl.BlockSpec((1, GROUP_WIDTH), lambda bi: (0, 0))],
        out_specs=pl.BlockSpec((1, t, GROUP_WIDTH), lambda bi: (bi, 0, 0)),
        out_shape=jax.ShapeDtypeStruct((b, t, GROUP_WIDTH), BF16),
        scratch_shapes=[pltpu.VMEM((max_n + 2 * POOL_PAD, LANES), F32)],
        compiler_params=_params(("arbitrary",)),
        name="pool",
    )(p_all, w, scale.reshape(1, GROUP_WIDTH))


def _diff_kernel(lam_ref, q_ref, k_ref, v_ref, g_ref, cos_ref, sin_ref, o_ref, kt_ref, vb_ref,
                 *, tq, n_ctx, out_scale):
    ti = pl.program_id(2)
    lam = lam_ref[0]

    @pl.when(ti == 0)
    def _():
        kt_ref[...] = _rope(k_ref[0], cos_ref[...], sin_ref[...]).T.astype(BF16)
        vb_ref[...] = v_ref[0].astype(BF16)

    def attend(n_keys):
        lane = lax.broadcasted_iota(jnp.int32, (1, 2 * HEAD_DIM), 1)
        rows = pl.ds(pl.multiple_of(ti * tq, tq), tq)
        q = _rope(q_ref[0], cos_ref[rows, :], sin_ref[rows, :]) * (HEAD_DIM ** -0.5)
        q1 = jnp.where(lane < HEAD_DIM, q, 0.0).astype(BF16)
        q2 = jnp.where(lane >= HEAD_DIM, q, 0.0).astype(BF16)
        kt = kt_ref[:, 0:n_keys]
        v = vb_ref[0:n_keys, :]
        s1 = jnp.dot(q1, kt, preferred_element_type=F32)
        s2 = jnp.dot(q2, kt, preferred_element_type=F32)
        e1 = jnp.exp(s1 - jnp.max(s1, axis=-1, keepdims=True))
        e2 = jnp.exp(s2 - jnp.max(s2, axis=-1, keepdims=True))
        r1 = 1.0 / jnp.sum(e1, axis=-1, keepdims=True)
        r2 = lam / jnp.sum(e2, axis=-1, keepdims=True)
        o = (jnp.dot(e1.astype(BF16), v, preferred_element_type=F32) * r1
             - jnp.dot(e2.astype(BF16), v, preferred_element_type=F32) * r2)
        ms = jnp.mean(o * o, axis=-1, keepdims=True)
        o_ref[0] = (o * lax.rsqrt(ms + EPS) * g_ref[...] * out_scale).astype(BF16)

    @pl.when(ti * tq < n_ctx)
    def _():
        attend(n_ctx)

    @pl.when(ti * tq >= n_ctx)
    def _():
        attend(k_ref.shape[1])


def _diff(p_all, lam, subln, cos_t, sin_t, n_ctx, lam_init, tq=256):
    b, t, _ = p_all.shape
    assert n_ctx % tq == 0 and t % tq == 0
    w = 2 * HEAD_DIM
    table = pl.BlockSpec((t, LANES), lambda bi, hi, ti: (0, 0))
    kern = functools.partial(_diff_kernel, tq=tq, n_ctx=n_ctx, out_scale=1.0 - lam_init)
    return pl.pallas_call(
        kern,
        grid=(b, DIFF_HEADS, t // tq),
        in_specs=[pl.BlockSpec(memory_space=pltpu.SMEM),
                  pl.BlockSpec((1, tq, w), lambda bi, hi, ti: (bi, ti, COL_DQ // w + hi)),
                  pl.BlockSpec((1, t, w), lambda bi, hi, ti: (bi, 0, COL_DK // w + hi)),
                  pl.BlockSpec((1, t, w), lambda bi, hi, ti: (bi, 0, COL_DV // w + hi)),
                  pl.BlockSpec((1, w), lambda bi, hi, ti: (0, 0)), table, table],
        out_specs=pl.BlockSpec((1, tq, w), lambda bi, hi, ti: (bi, ti, hi)),
        out_shape=jax.ShapeDtypeStruct((b, t, GROUP_WIDTH), BF16),
        scratch_shapes=[pltpu.VMEM((w, t), BF16), pltpu.VMEM((t, w), BF16)],
        compiler_params=_params(("arbitrary", "arbitrary", "arbitrary")),
        name="diff_attn",
    )(lam.reshape(1), p_all, p_all, p_all, subln.reshape(1, w), cos_t, sin_t)


def _gelu_tanh(x):
    return 0.5 * x * (1.0 + jnp.tanh(math.sqrt(2.0 / math.pi) * (x + 0.044715 * (x * x * x))))


def _sgu_kernel(z_ref, g_ref, b_ref, w_ref, bs_ref, o_ref, *, tm):
    z = _gelu_tanh(z_ref[0])
    u = z[:, :GROUP_WIDTH]
    v = z[:, GROUP_WIDTH:]
    mu = jnp.mean(v, axis=-1, keepdims=True)
    var = jnp.mean(jnp.square(v - mu), axis=-1, keepdims=True)
    vn = ((v - mu) * lax.rsqrt(var + EPS) * g_ref[...] + b_ref[...]).astype(BF16)
    for g in range(SGU_GROUPS):
        wg = w_ref[g].astype(BF16)
        bias = bs_ref[:, g:g + 1]
        cs = slice(g * LANES, (g + 1) * LANES)
        for c in range(tm // SGU_CHUNK):
            rs = slice(c * SGU_CHUNK, (c + 1) * SGU_CHUNK)
            sv = jnp.dot(wg, vn[rs, cs], preferred_element_type=F32) + bias
            o_ref[0, rs, cs] = (u[rs, cs] * sv).astype(BF16)


def _sgu(p_all, ln_g, ln_b, w_s, b_s, tm=768):
    b, t, _ = p_all.shape
    zw = 2 * GROUP_WIDTH
    kern = functools.partial(_sgu_kernel, tm=tm)
    return pl.pallas_call(
        kern,
        grid=(b, t // tm),
        in_specs=[pl.BlockSpec((1, tm, zw), lambda bi, ti: (bi, ti, COL_Z // zw)),
                  pl.BlockSpec((1, GROUP_WIDTH), lambda bi, ti: (0, 0)),
                  pl.BlockSpec((1, GROUP_WIDTH), lambda bi, ti: (0, 0)),
                  pl.BlockSpec(w_s.shape, lambda bi, ti: (0, 0, 0)),
                  pl.BlockSpec((SGU_CHUNK, SGU_GROUPS), lambda bi, ti: (0, 0))],
        out_specs=pl.BlockSpec((1, tm, GROUP_WIDTH), lambda bi, ti: (bi, ti, 0)),
        out_shape=jax.ShapeDtypeStruct((b, t, GROUP_WIDTH), BF16),
        compiler_params=_params(("arbitrary", "arbitrary")),
        name="sgu",
    )(p_all, ln_g.reshape(1, GROUP_WIDTH), ln_b.reshape(1, GROUP_WIDTH), w_s, b_s.T)


def _win_band_start(step, t):
    return jnp.clip(step * WIN_QB - WIN_BLOCK, 0, t - WIN_BAND)


def _win_bias(n_ctx, t):
    step = jnp.arange(t // WIN_QB, dtype=jnp.int32)[:, None, None]
    qpos = step * WIN_QB + jnp.arange(WIN_QB, dtype=jnp.int32)[None, :, None]
    kpos = _win_band_start(step, t) + jnp.arange(WIN_BAND, dtype=jnp.int32)[None, None, :]
    valid = (kpos >= n_ctx) & (qpos >= n_ctx) & (jnp.abs(kpos - qpos) <= WINDOW)
    return jnp.where(valid, 0.0, NEG_INF).astype(F32)


def _win_kernel(sink_ref, q_ref, k_ref, v_ref, bias_ref, cos_ref, sin_ref, o_ref, kt_ref, vv_ref, *, n_ctx):
    g = pl.program_id(1)
    n = pl.program_id(2)
    t = k_ref.shape[1]
    blk = WIN_QB
    band = WIN_BAND
    lane = lax.broadcasted_iota(jnp.int32, (1, LANES), 1)
    low = lane < HEAD_DIM

    @pl.when(n == 0)
    def _():
        mine = (lane >= g * HEAD_DIM) & (lane < (g + 1) * HEAD_DIM)
        km = jnp.where(mine, _rope(k_ref[0], cos_ref[...], sin_ref[...]), 0.0)
        vm = jnp.where(mine, v_ref[0], 0.0)
        kt_ref[...] = (km + pltpu.roll(km, HEAD_DIM, 1)).T.astype(BF16)
        vv_ref[...] = (vm + pltpu.roll(vm, HEAD_DIM, 1)).astype(BF16)

    rows = WIN_REP * blk
    head = lax.broadcasted_iota(jnp.int32, (rows, 1), 0) // blk
    start = pl.multiple_of(_win_band_start(n, t), WIN_BLOCK)

    q_rows = pl.ds(pl.multiple_of(n * blk, blk), blk)
    cos = cos_ref[q_rows, :]
    sin = sin_ref[q_rows, :]
    pieces = []
    for pair in range(WIN_REP // 2):
        qp = _rope(q_ref[0, :, pair * LANES:(pair + 1) * LANES], cos, sin) * (HEAD_DIM ** -0.5)
        pieces += [jnp.where(low, qp, 0.0), jnp.where(low, 0.0, qp)]
    qs = jnp.concatenate(pieces, axis=0).astype(BF16)
    s_c = jnp.dot(qs, kt_ref[:, 0:n_ctx], preferred_element_type=F32)
    s_b = jnp.dot(qs, kt_ref[:, pl.ds(start, band)], preferred_element_type=F32)
    s_b = (s_b.reshape(WIN_REP, blk, band) + bias_ref[...]).reshape(rows, band)
    sink = jnp.full((rows, 1), sink_ref[g * WIN_REP], F32)
    for r in range(1, WIN_REP):
        sink = jnp.where(head == r, sink_ref[g * WIN_REP + r], sink)
    m = jnp.maximum(jnp.maximum(jnp.max(s_c, axis=-1, keepdims=True),
                                jnp.max(s_b, axis=-1, keepdims=True)), sink)
    e_c = jnp.exp(s_c - m)
    e_b = jnp.exp(s_b - m)
    den = jnp.sum(e_c, axis=-1, keepdims=True) + jnp.sum(e_b, axis=-1, keepdims=True) + jnp.exp(sink - m)
    o = (jnp.dot(e_c.astype(BF16), vv_ref[0:n_ctx, :], preferred_element_type=F32)
         + jnp.dot(e_b.astype(BF16), vv_ref[pl.ds(start, band), :], preferred_element_type=F32)) / den
    for pair in range(WIN_REP // 2):
        lo = o[(2 * pair) * blk:(2 * pair + 1) * blk]
        hi = o[(2 * pair + 1) * blk:(2 * pair + 2) * blk]
        o_ref[0, :, pair * LANES:(pair + 1) * LANES] = jnp.where(low, lo, hi).astype(BF16)


def _win(p_all, sink, cos_t, sin_t, n_ctx):
    b, t, _ = p_all.shape
    qw = WIN_REP * HEAD_DIM
    table = pl.BlockSpec((t, LANES), lambda bi, gi, ni: (0, 0))
    assert t % WIN_QB == 0 and t >= WIN_BAND
    lo = -(-(n_ctx + WIN_BLOCK) // WIN_QB)
    hi = (t - WIN_BAND + WIN_BLOCK) // WIN_QB

    def bias_block(bi, gi, ni):
        return (jnp.where((ni >= lo) & (ni <= hi), lo, ni), 0, 0)

    kern = functools.partial(_win_kernel, n_ctx=n_ctx)
    return pl.pallas_call(
        kern,
        grid=(b, WIN_KV_HEADS, t // WIN_QB),
        in_specs=[pl.BlockSpec(memory_space=pltpu.SMEM),
                  pl.BlockSpec((1, WIN_QB, qw), lambda bi, gi, ni: (bi, ni, COL_WQ // qw + gi)),
                  pl.BlockSpec((1, t, LANES), lambda bi, gi, ni: (bi, 0, COL_WK // LANES)),
                  pl.BlockSpec((1, t, LANES), lambda bi, gi, ni: (bi, 0, COL_WV // LANES)),
                  pl.BlockSpec((1, WIN_QB, WIN_BAND), bias_block), table, table],
        out_specs=pl.BlockSpec((1, WIN_QB, qw), lambda bi, gi, ni: (bi, ni, gi)),
        out_shape=jax.ShapeDtypeStruct((b, t, GROUP_WIDTH), BF16),
        scratch_shapes=[pltpu.VMEM((LANES, t), BF16), pltpu.VMEM((t, LANES), BF16)],
        compiler_params=_params(("arbitrary", "arbitrary", "arbitrary")),
        name="win_attn",
    )(sink, p_all, p_all, p_all, _win_bias(n_ctx, t), cos_t, sin_t)


def _outproj_kernel(a_ref, b_ref, c_ref, d_ref, w_ref, h_ref, m_ref, o_ref, wb_ref, *, tm, n_ctx):
    ti = pl.program_id(2)

    @pl.when((pl.program_id(1) == 0) & (ti == 0))
    def _():
        wb_ref[...] = w_ref[...].astype(BF16)

    y = None
    for i, r in enumerate((a_ref, b_ref, c_ref, d_ref)):
        part = jnp.dot(r[0], wb_ref[i * GROUP_WIDTH:(i + 1) * GROUP_WIDTH, :], preferred_element_type=F32)
        y = part if y is None else y + part
    gate = _mod_row(m_ref, 2, _row_is_ctx(ti, tm, n_ctx))
    o_ref[0] = h_ref[0] + gate * y


def _outproj(mixes, w, layer, h, modsel, n_ctx, tm=768, tn=1024):
    b, t, d = h.shape
    tn = min(tn, d)
    assert d % tn == 0 and t % tm == 0
    kern = functools.partial(_outproj_kernel, tm=tm, n_ctx=n_ctx)
    mix_spec = pl.BlockSpec((1, tm, GROUP_WIDTH), lambda j, bi, ti: (bi, ti, 0))
    return pl.pallas_call(
        kern,
        grid=(d // tn, b, t // tm),
        in_specs=[mix_spec, mix_spec, mix_spec, mix_spec,
                  pl.BlockSpec((None, 4 * GROUP_WIDTH, tn), lambda j, bi, ti: (layer, 0, j)),
                  pl.BlockSpec((1, tm, tn), lambda j, bi, ti: (bi, ti, j)),
                  pl.BlockSpec((None, 2, 6, tn), lambda j, bi, ti: (bi, 0, 0, j))],
        out_specs=pl.BlockSpec((1, tm, tn), lambda j, bi, ti: (bi, ti, j)),
        out_shape=jax.ShapeDtypeStruct((b, t, d), F32),
        scratch_shapes=[pltpu.VMEM((4 * GROUP_WIDTH, tn), BF16)],
        compiler_params=_params(("arbitrary", "arbitrary", "arbitrary")),
        name="outproj",
    )(*mixes, w, h, modsel)


def _ffn_in_kernel(x_ref, m_ref, g_ref, o_ref, *, tm, n_ctx, row0):
    def store(rs, y):
        o_ref[0, rs, :] = y
    _norm_modulate_rows(x_ref, g_ref, m_ref, 3, row0 + pl.program_id(1) * tm, tm, n_ctx, store)


def _route_kernel(x_ref, m_ref, g_ref, r_ref, o_ref, gate_ref, sel_ref, *, tm, n_ctx, row0):
    _ffn_in_kernel(x_ref, m_ref, g_ref, o_ref, tm=tm, n_ctx=n_ctx, row0=row0)
    f = o_ref[0]
    lane = lax.broadcasted_iota(jnp.int32, (1, LANES), 1)
    logits = jnp.full((tm, LANES), -jnp.inf, F32)
    for e in range(N_EXPERTS):
        logits = jnp.where(lane == e, jnp.sum(f * r_ref[e:e + 1, :], axis=-1, keepdims=True), logits)
    m1 = jnp.max(logits, axis=-1, keepdims=True)
    i1 = jnp.min(jnp.where(logits == m1, lane, LANES), axis=-1, keepdims=True)
    rest = jnp.where(lane == i1, -jnp.inf, logits)
    m2 = jnp.max(rest, axis=-1, keepdims=True)
    i2 = jnp.min(jnp.where(rest == m2, lane, LANES), axis=-1, keepdims=True)
    e2 = jnp.exp(m2 - m1)
    w1 = 1.0 / (1.0 + e2)
    w2 = e2 / (1.0 + e2)
    gate_ref[0] = jnp.where(lane == i1, w1, 0.0) + jnp.where(lane == i2, w2, 0.0)
    sel_ref[0] = ((lane == i1) | (lane == i2)).astype(jnp.int32)


def _ffn_in_route(h, modsel, g, n_ctx, row0, router, tm=256):
    b, t, d = h.shape
    nt = (t - row0) // tm
    off = row0 // tm
    in_specs = [pl.BlockSpec((1, tm, d), lambda bi, ti: (bi, ti + off, 0)),
                pl.BlockSpec((None, 2, 6, d), lambda bi, ti: (bi, 0, 0, 0)),
                pl.BlockSpec((1, d), lambda bi, ti: (0, 0))]
    f_spec = pl.BlockSpec((1, tm, d), lambda bi, ti: (bi, ti, 0))
    f_shape = jax.ShapeDtypeStruct((b, t - row0, d), F32)
    assert router.shape == (d, N_EXPERTS)
    e_spec = pl.BlockSpec((1, tm, LANES), lambda bi, ti: (bi, ti, 0))
    kern = functools.partial(_route_kernel, tm=tm, n_ctx=n_ctx, row0=row0)
    return pl.pallas_call(
        kern, grid=(b, nt),
        in_specs=in_specs + [pl.BlockSpec((N_EXPERTS, d), lambda bi, ti: (0, 0))],
        out_specs=[f_spec, e_spec, e_spec],
        out_shape=[f_shape, jax.ShapeDtypeStruct((b, t - row0, LANES), F32),
                   jax.ShapeDtypeStruct((b, t - row0, LANES), jnp.int32)],
        compiler_params=_params(("arbitrary", "arbitrary")), name="ffn_in_route",
    )(h, modsel, g.reshape(1, d), router.T)


FFN_TM = 1152
FFN_SUB = 192
FFN_VMEM_LIMIT = 60 * 1024 * 1024


def _swiglu_slice(xs, w1, w3, w2):
    h1 = jnp.dot(xs, w1.astype(BF16), preferred_element_type=F32)
    h3 = jnp.dot(xs, w3.astype(BF16), preferred_element_type=F32)
    act = (h1 * _sigmoid(h1) * h3).astype(BF16)
    return jnp.dot(act, w2.astype(BF16), preferred_element_type=F32)


def _ffn_kernel(te_ref, tr_ref, x_ref, w1_ref, w3_ref, w2_ref, o_ref, xs_ref):
    i = pl.program_id(0)
    j = pl.program_id(1)
    n_blocks = (tr_ref[i] + FFN_SUB - 1) // FFN_SUB

    @pl.when(j == 0)
    def _():
        o_ref[...] = jnp.zeros_like(o_ref)
        xs_ref[...] = x_ref[...].astype(BF16)

    for nb in range(1, FFN_TM // FFN_SUB + 1):
        @pl.when(n_blocks == nb)
        def _(nb=nb):
            m = nb * FFN_SUB
            o_ref[0:m, :] += _swiglu_slice(xs_ref[0:m, :], w1_ref[0], w3_ref[0], w2_ref[0])


def _ffn(x_rows, w1, w3, w2, tile_expert, tile_rows, tf=256):
    nr, d = x_rows.shape
    n_exp, _, f = w1.shape
    nj = f // tf

    def wcol(i, j, te, tr):
        return (te[i], 0, jnp.where(tr[i] > 0, j, nj - 1))

    def wrow(i, j, te, tr):
        return (te[i], jnp.where(tr[i] > 0, j, nj - 1), 0)

    return pl.pallas_call(
        _ffn_kernel,
        grid_spec=pltpu.PrefetchScalarGridSpec(
            num_scalar_prefetch=2,
            grid=(nr // FFN_TM, nj),
            in_specs=[pl.BlockSpec((FFN_TM, d), lambda i, j, te, tr: (i, 0)),
                      pl.BlockSpec((1, d, tf), wcol),
                      pl.BlockSpec((1, d, tf), wcol),
                      pl.BlockSpec((1, tf, d), wrow)],
            out_specs=pl.BlockSpec((FFN_TM, d), lambda i, j, te, tr: (i, 0)),
            scratch_shapes=[pltpu.VMEM((FFN_TM, d), BF16)]),
        out_shape=jax.ShapeDtypeStruct((nr, d), F32),
        compiler_params=_params(("arbitrary", "arbitrary"), FFN_VMEM_LIMIT),
        name="ffn",
    )(tile_expert, tile_rows, x_rows, w1, w3, w2)


def _ffn_dense_kernel(h_ref, m_ref, g_ref, w1_ref, w3_ref, w2_ref, o_ref, xs_ref, *, tm, n_ctx):
    ti = pl.program_id(1)
    j = pl.program_id(2)

    @pl.when(j == 0)
    def _():
        def store(rs, y):
            xs_ref[rs, :] = y.astype(BF16)
        _norm_modulate_rows(h_ref, g_ref, m_ref, 3, ti * tm, tm, n_ctx, store)
        o_ref[...] = jnp.zeros_like(o_ref)

    o_ref[0] += _swiglu_slice(xs_ref[...], w1_ref[...], w3_ref[...], w2_ref[...])

    @pl.when(j == pl.num_programs(2) - 1)
    def _():
        gate = _mod_row(m_ref, 5, _row_is_ctx(ti, tm, n_ctx))
        o_ref[0] = h_ref[0] + gate * o_ref[0]


def _ffn_dense(h, modsel, g, w1, w3, w2, layer, n_ctx, tf=256):
    b, t, d = h.shape
    f = w1.shape[2]
    tm = t // 2
    assert t % tm == 0 and f % tf == 0 and tm % NORM_ROWS == 0 and n_ctx % NORM_ROWS == 0
    kern = functools.partial(_ffn_dense_kernel, tm=tm, n_ctx=n_ctx)
    return pl.pallas_call(
        kern, grid=(b, t // tm, f // tf),
        in_specs=[pl.BlockSpec((1, tm, d), lambda bi, ti, j: (bi, ti, 0)),
                  pl.BlockSpec((None, 2, 6, d), lambda bi, ti, j: (bi, 0, 0, 0)),
                  pl.BlockSpec((1, d), lambda bi, ti, j: (0, 0)),
                  pl.BlockSpec((None, d, tf), lambda bi, ti, j: (layer, 0, j)),
                  pl.BlockSpec((None, d, tf), lambda bi, ti, j: (layer, 0, j)),
                  pl.BlockSpec((None, tf, d), lambda bi, ti, j: (layer, j, 0))],
        out_specs=pl.BlockSpec((1, tm, d), lambda bi, ti, j: (bi, ti, 0), pipeline_mode=pl.Buffered(1)),
        out_shape=jax.ShapeDtypeStruct((b, t, d), F32),
        scratch_shapes=[pltpu.VMEM((tm, d), BF16)],
        compiler_params=_params(("arbitrary", "arbitrary", "arbitrary")), name="ffn_dense",
    )(h, modsel, g.reshape(1, d), w1, w3, w2)


def _moe_final_kernel(h_ref, y0_ref, y1_ref, w0_ref, w1_ref, m_ref, g_ref, o_ref):
    y = w0_ref[0] * y0_ref[0] + w1_ref[0] * y1_ref[0]
    hn = h_ref[0] + m_ref[1, 5:6, :] * y
    ms = jnp.mean(hn * hn, axis=-1, keepdims=True)
    o_ref[0] = hn * lax.rsqrt(ms + EPS) * g_ref[...]


def _moe_final(h, y0, y1, w0, w1, modsel, g, row0, tm=256):
    b, t, d = h.shape
    n_lat = t - row0
    off = row0 // tm
    spec = pl.BlockSpec((1, tm, d), lambda bi, ti: (bi, ti, 0))
    wspec = pl.BlockSpec((1, tm, 1), lambda bi, ti: (bi, ti, 0))
    return pl.pallas_call(
        _moe_final_kernel, grid=(b, n_lat // tm),
        in_specs=[pl.BlockSpec((1, tm, d), lambda bi, ti: (bi, ti + off, 0)), spec, spec, wspec, wspec,
                  pl.BlockSpec((None, 2, 6, d), lambda bi, ti: (bi, 0, 0, 0)),
                  pl.BlockSpec((1, d), lambda bi, ti: (0, 0))],
        out_specs=spec, out_shape=jax.ShapeDtypeStruct((b, n_lat, d), F32),
        compiler_params=_params(("arbitrary", "arbitrary")), name="moe_final",
    )(h, y0, y1, w0, w1, modsel, g.reshape(1, d))


def _routing_tables(sel, gates, n_tiles):
    n, n_exp = sel.shape
    seli = sel.astype(jnp.int32)
    counts = jnp.sum(seli, axis=0)
    rank = jnp.cumsum(seli, axis=0) - seli
    tiles_e = (counts + FFN_TM - 1) // FFN_TM
    tile_end = jnp.cumsum(tiles_e)
    tile_start = tile_end - tiles_e
    pos = tile_start[None, :] * FFN_TM + rank
    n_rows = n_tiles * FFN_TM
    expert = jnp.arange(n_exp, dtype=jnp.int32)[None, :]
    e_lo = jnp.min(jnp.where(sel, expert, n_exp), axis=1, keepdims=True)
    e_hi = jnp.max(jnp.where(sel, expert, -1), axis=1, keepdims=True)
    pick = lambda a, e: jnp.sum(jnp.where(expert == e, a, 0), axis=1)
    pos0, pos1 = pick(pos, e_lo), pick(pos, e_hi)
    w0, w1 = pick(gates, e_lo), pick(gates, e_hi)
    tok = jnp.arange(n, dtype=jnp.int32)
    src = (jnp.arange(n_rows, dtype=jnp.int32) % n).at[jnp.concatenate([pos0, pos1])].set(
        jnp.concatenate([tok, tok]), mode="promise_in_bounds", unique_indices=True)

    tile = jnp.arange(n_tiles, dtype=jnp.int32)
    te = jnp.minimum(jnp.sum((tile_end[None, :] <= tile[:, None]).astype(jnp.int32), axis=1), n_exp - 1)
    tr = jnp.clip(counts[te] - (tile - tile_start[te]) * FFN_TM, 0, FFN_TM)
    tr = jnp.where(tile < tile_end[-1], tr, 0)
    return src, pos0, pos1, w0, w1, te.astype(jnp.int32), tr.astype(jnp.int32)


def kernel(x, c, ctx, c_ctx, w_mod, b_mod, norm_mix, norm_ffn, w_in, w_out, pool_w, pool_scale, diff_lam,
           diff_subln, sgu_ln_g, sgu_ln_b, sgu_w, sgu_b, win_sink, ffn_w1, ffn_w3, ffn_w2, moe_router,
           moe_w1, moe_w3, moe_w2, norm_final):
    b, n_lat, d = x.shape
    n_ctx = ctx.shape[1]
    t = n_ctx + n_lat
    depth = w_mod.shape[0]
    segments = ((0, n_ctx), (n_ctx, n_lat))

    c_rows = jnp.zeros((8, d), F32).at[:b].set(c).at[b].set(c_ctx)
    mod = _adaln(c_rows, w_mod, b_mod).reshape(depth, 8, 6, d)
    cos_t, sin_t = _rope_tables(n_ctx, n_lat)
    h = jnp.concatenate([ctx, x], axis=1)

    out = None
    for l in range(depth):
        last = l == depth - 1
        lat_m = mod[l, :b]
        ctx_m = jnp.broadcast_to(mod[l, b:b + 1], lat_m.shape)
        modsel = jnp.stack([ctx_m, lat_m], axis=1)

        p_all = _inproj(h, modsel, norm_mix[l], w_in, l, n_ctx)

        lam_init = 0.8 - 0.6 * math.exp(-0.3 * l)
        lq1, lk1, lq2, lk2 = diff_lam[l].astype(F32)
        lam = jnp.exp(jnp.sum(lq1 * lk1)) - jnp.exp(jnp.sum(lq2 * lk2)) + lam_init

        mixes = (_pool(p_all, pool_w[l], pool_scale[l], segments),
                 _diff(p_all, lam, diff_subln[l], cos_t, sin_t, n_ctx, lam_init),
                 _sgu(p_all, sgu_ln_g[l], sgu_ln_b[l], sgu_w[l], sgu_b[l]),
                 _win(p_all, win_sink[l], cos_t, sin_t, n_ctx))
        h = _outproj(mixes, w_out, l, h, modsel, n_ctx)

        i = l // 2
        row0 = n_ctx if last else 0
        rows = b * (t - row0)
        if l % 2 == 0:
            assert not last
            h = _ffn_dense(h, modsel, norm_ffn[l], ffn_w1, ffn_w3, ffn_w2, i, n_ctx)
        else:
            assert last
            f, gates, sel = _ffn_in_route(h, modsel, norm_ffn[l], n_ctx, row0, moe_router[i])
            f = f.reshape(rows, d)
            n_exp = moe_w1.shape[1]
            n_tiles = (2 * rows) // FFN_TM + n_exp
            src, pos0, pos1, w0, w1, te, tr = _routing_tables(
                sel.reshape(rows, LANES)[:, :n_exp] > 0, gates.reshape(rows, LANES)[:, :n_exp], n_tiles)
            rows_of = lambda a, idx: a.at[idx].get(mode="promise_in_bounds")
            ys = _ffn(rows_of(f, src), moe_w1[i], moe_w3[i], moe_w2[i], te, tr)
            shp = (b, t - row0, d)
            out = _moe_final(h, rows_of(ys, pos0).reshape(shp), rows_of(ys, pos1).reshape(shp),
                             w0.reshape(b, t - row0, 1), w1.reshape(b, t - row0, 1), modsel, norm_final, row0)
    return out
```

```python
import functools
import math

import jax
import jax.numpy as jnp
from jax import lax
from jax.experimental import pallas as pl
from jax.experimental.pallas import tpu as pltpu

F32 = jnp.float32
BF16 = jnp.bfloat16

GRID_W = 64
GROUP_WIDTH = 512
HEAD_DIM = 64
ROPE_BASE = 10000.0
EPS = 1e-6
NEG_INF = -1e30
LANES = 128

POOL_WINDOWS = (2, 4, 8, 16)
POOL_PAD = 16
DIFF_HEADS = GROUP_WIDTH // (2 * HEAD_DIM)
DIFF_STEP_HEADS = 4
SGU_CHUNK = 128
SGU_GROUPS = 4
WIN_HEADS = GROUP_WIDTH // HEAD_DIM
WIN_KV_HEADS = 2
WIN_REP = WIN_HEADS // WIN_KV_HEADS
WINDOW = 128
WIN_BLOCK = 128
WIN_QB = 2 * WIN_BLOCK
WIN_BAND = WIN_QB + 2 * WIN_BLOCK
N_EXPERTS = 8

COL_POOL = 0
COL_DQ = GROUP_WIDTH
COL_DK = 2 * GROUP_WIDTH
COL_DV = 3 * GROUP_WIDTH
COL_Z = 4 * GROUP_WIDTH
COL_WQ = 6 * GROUP_WIDTH
COL_WK = COL_WQ + WIN_HEADS * HEAD_DIM
COL_WV = COL_WK + WIN_KV_HEADS * HEAD_DIM
IN_COLS = COL_WV + WIN_KV_HEADS * HEAD_DIM

VMEM_LIMIT = 56 * 1024 * 1024
NORM_ROWS = 32


def _params(sem, vmem=VMEM_LIMIT):
    return pltpu.CompilerParams(dimension_semantics=sem, vmem_limit_bytes=vmem)


def _sigmoid(x):
    return 1.0 / (1.0 + jnp.exp(-x))


def _row_is_ctx(tile_idx, tm, n_ctx):
    rows = tile_idx * tm + lax.broadcasted_iota(jnp.int32, (tm, 1), 0)
    return rows < n_ctx


def _mod_row(m_ref, k, is_ctx):
    return jnp.where(is_ctx, m_ref[0, k:k + 1, :], m_ref[1, k:k + 1, :])


def _norm_modulate_rows(x_ref, g_ref, m_ref, k_shift, first_row, tm, n_ctx, store):
    def body(r, carry):
        r0 = pl.multiple_of(r * NORM_ROWS, NORM_ROWS)
        which = ((first_row + r0) >= n_ctx).astype(jnp.int32)
        rs = pl.ds(r0, NORM_ROWS)
        x = x_ref[0, rs, :]
        ms = jnp.mean(x * x, axis=-1, keepdims=True)
        a = g_ref[...] * (1.0 + m_ref[which, k_shift + 1:k_shift + 2, :])
        store(rs, x * lax.rsqrt(ms + EPS) * a + m_ref[which, k_shift:k_shift + 1, :])
        return carry
    lax.fori_loop(0, tm // NORM_ROWS, body, 0, unroll=2)


def _adaln_kernel(c_ref, w_ref, b_ref, o_ref):
    cv = c_ref[...]
    s = (cv * _sigmoid(cv)).astype(BF16)
    o_ref[0] = jnp.dot(s, w_ref[0].astype(BF16), preferred_element_type=F32) + b_ref[0]


def _adaln(c_rows, w_mod, b_mod, tn=1024):
    depth, d, n = w_mod.shape
    r = c_rows.shape[0]
    return pl.pallas_call(
        _adaln_kernel,
        grid=(depth, n // tn),
        in_specs=[pl.BlockSpec((r, d), lambda l, j: (0, 0)),
                  pl.BlockSpec((1, d, tn), lambda l, j: (l, 0, j)),
                  pl.BlockSpec((1, 1, tn), lambda l, j: (l, 0, j))],
        out_specs=pl.BlockSpec((1, r, tn), lambda l, j: (l, 0, j)),
        out_shape=jax.ShapeDtypeStruct((depth, r, n), F32),
        compiler_params=_params(("arbitrary", "arbitrary")),
        name="adaln",
    )(c_rows, w_mod, b_mod.reshape(depth, 1, n))


def _inproj_kernel(x_ref, m_ref, g_ref, w_ref, o_ref, xn_ref, *, tm, n_ctx):
    @pl.when(pl.program_id(2) == 0)
    def _():
        def store(rs, y):
            xn_ref[rs, :] = y.astype(BF16)
        _norm_modulate_rows(x_ref, g_ref, m_ref, 0, pl.program_id(1) * tm, tm, n_ctx, store)

    o_ref[0] = jnp.dot(xn_ref[...], w_ref[...].astype(BF16), preferred_element_type=F32)


def _rope(a, cos, sin):
    lane = lax.broadcasted_iota(jnp.int32, (1, LANES), 1)
    first_half = (lane % HEAD_DIM) < (HEAD_DIM // 2)
    partner = jnp.where(first_half, pltpu.roll(a, LANES - HEAD_DIM // 2, 1), pltpu.roll(a, HEAD_DIM // 2, 1))
    return a * cos + partner * sin


def _inproj(h, modsel, g, w, layer, n_ctx, tn=768):
    b, t, d = h.shape
    tm = t // 2
    assert tm % NORM_ROWS == 0 and n_ctx % NORM_ROWS == 0 and IN_COLS % tn == 0
    kern = functools.partial(_inproj_kernel, tm=tm, n_ctx=n_ctx)
    return pl.pallas_call(
        kern,
        grid=(b, t // tm, IN_COLS // tn),
        in_specs=[pl.BlockSpec((1, tm, d), lambda bi, ti, j: (bi, ti, 0)),
                  pl.BlockSpec((None, 2, 6, d), lambda bi, ti, j: (bi, 0, 0, 0)),
                  pl.BlockSpec((1, d), lambda bi, ti, j: (0, 0)),
                  pl.BlockSpec((None, d, tn), lambda bi, ti, j: (layer, 0, j))],
        out_specs=pl.BlockSpec((1, tm, tn), lambda bi, ti, j: (bi, ti, j)),
        out_shape=jax.ShapeDtypeStruct((b, t, IN_COLS), F32),
        scratch_shapes=[pltpu.VMEM((tm, d), BF16)],
        compiler_params=_params(("arbitrary", "arbitrary", "arbitrary")),
        name="inproj",
    )(h, modsel, g.reshape(1, d), w)


def _rope_tables(n_ctx, n_lat):
    n_rows = n_lat // GRID_W
    rows = jnp.repeat(jnp.arange(n_rows, dtype=F32), GRID_W)
    cols = jnp.tile(jnp.arange(GRID_W, dtype=F32), n_rows)
    n_freq = HEAD_DIM // 4
    inv_freq = ROPE_BASE ** (-jnp.arange(n_freq, dtype=F32) / n_freq)
    ang = jnp.concatenate([rows[:, None] * inv_freq, cols[:, None] * inv_freq], axis=-1)
    cos, sin = jnp.cos(ang), jnp.sin(ang)
    cos_h = jnp.concatenate([cos, cos], axis=-1)
    sin_h = jnp.concatenate([-sin, sin], axis=-1)
    cos_t = jnp.concatenate([jnp.ones((n_ctx, HEAD_DIM), F32), cos_h], axis=0)
    sin_t = jnp.concatenate([jnp.zeros((n_ctx, HEAD_DIM), F32), sin_h], axis=0)
    rep = LANES // HEAD_DIM
    return jnp.tile(cos_t, (1, rep)), jnp.tile(sin_t, (1, rep))


def _pool_kernel(a_ref, w_ref, s_ref, o_ref, pad_ref, *, segments):
    for g, win in enumerate(POOL_WINDOWS):
        half = win // 2
        cs = slice(g * LANES, (g + 1) * LANES)
        wg = w_ref[g].astype(BF16)
        for s0, n in segments:
            a = a_ref[0, s0:s0 + n, cs]
            pad_ref[0:POOL_PAD, :] = jnp.zeros((POOL_PAD, LANES), F32)
            pad_ref[POOL_PAD:POOL_PAD + n, :] = a
            pad_ref[POOL_PAD + n:2 * POOL_PAD + n, :] = jnp.zeros((POOL_PAD, LANES), F32)
            tot = pad_ref[POOL_PAD - half:POOL_PAD - half + n, :]
            for k in range(-half + 1, half):
                tot = tot + pad_ref[POOL_PAD + k:POOL_PAD + k + n, :]
            t = lax.broadcasted_iota(jnp.int32, (n, 1), 0)
            cnt = (jnp.minimum(t + half, n) - jnp.maximum(t - half, 0)).astype(F32)
            dlt = (tot / cnt - a).astype(BF16)
            y = jnp.dot(dlt, wg, preferred_element_type=F32)
            o_ref[0, s0:s0 + n, cs] = (y * s_ref[:, cs]).astype(BF16)


def _pool(p_all, w, scale, segments):
    b, t, _ = p_all.shape
    max_n = max(n for _, n in segments)
    kern = functools.partial(_pool_kernel, segments=segments)
    return pl.pallas_call(
        kern,
        grid=(b,),
        in_specs=[pl.BlockSpec((1, t, GROUP_WIDTH), lambda bi: (bi, 0, COL_POOL // GROUP_WIDTH)),
                  pl.BlockSpec(w.shape, lambda bi: (0, 0, 0)),
                  pl.BlockSpec((1, GROUP_WIDTH), lambda bi: (0, 0))],
        out_specs=pl.BlockSpec((1, t, GROUP_WIDTH), lambda bi: (bi, 0, 0)),
        out_shape=jax.ShapeDtypeStruct((b, t, GROUP_WIDTH), BF16),
        scratch_shapes=[pltpu.VMEM((max_n + 2 * POOL_PAD, LANES), F32)],
        compiler_params=_params(("arbitrary",)),
        name="pool",
    )(p_all, w, scale.reshape(1, GROUP_WIDTH))


def _diff_kernel(lam_ref, q_ref, k_ref, v_ref, g_ref, cos_ref, sin_ref, o_ref, kt_ref, vb_ref,
                 *, tq, n_ctx, out_scale):
    ti = pl.program_id(2)
    lam = lam_ref[0]
    w = 2 * HEAD_DIM

    @pl.when(ti == 0)
    def _():
        for hh in range(DIFF_STEP_HEADS):
            cs = slice(hh * w, (hh + 1) * w)
            kt_ref[cs, :] = _rope(k_ref[0, :, cs], cos_ref[...], sin_ref[...]).T.astype(BF16)
        vb_ref[...] = v_ref[0].astype(BF16)

    def attend(n_keys):
        lane = lax.broadcasted_iota(jnp.int32, (1, w), 1)
        rows = pl.ds(pl.multiple_of(ti * tq, tq), tq)
        cos = cos_ref[rows, :]
        sin = sin_ref[rows, :]
        for hh in range(DIFF_STEP_HEADS):
            cs = slice(hh * w, (hh + 1) * w)
            q = _rope(q_ref[0, :, cs], cos, sin) * (HEAD_DIM ** -0.5)
            q1 = jnp.where(lane < HEAD_DIM, q, 0.0).astype(BF16)
            q2 = jnp.where(lane >= HEAD_DIM, q, 0.0).astype(BF16)
            kt = kt_ref[cs, 0:n_keys]
            v = vb_ref[0:n_keys, cs]
            s1 = jnp.dot(q1, kt, preferred_element_type=F32)
            s2 = jnp.dot(q2, kt, preferred_element_type=F32)
            e1 = jnp.exp(s1 - jnp.max(s1, axis=-1, keepdims=True))
            e2 = jnp.exp(s2 - jnp.max(s2, axis=-1, keepdims=True))
            r1 = 1.0 / jnp.sum(e1, axis=-1, keepdims=True)
            r2 = lam / jnp.sum(e2, axis=-1, keepdims=True)
            o = (jnp.dot(e1.astype(BF16), v, preferred_element_type=F32) * r1
                 - jnp.dot(e2.astype(BF16), v, preferred_element_type=F32) * r2)
            ms = jnp.mean(o * o, axis=-1, keepdims=True)
            o_ref[0, :, cs] = (o * lax.rsqrt(ms + EPS) * g_ref[...] * out_scale).astype(BF16)

    @pl.when(ti * tq < n_ctx)
    def _():
        attend(n_ctx)

    @pl.when(ti * tq >= n_ctx)
    def _():
        attend(k_ref.shape[1])


def _diff(p_all, lam, subln, cos_t, sin_t, n_ctx, lam_init, tq=256):
    b, t, _ = p_all.shape
    assert n_ctx % tq == 0 and t % tq == 0 and DIFF_HEADS % DIFF_STEP_HEADS == 0
    w = 2 * HEAD_DIM
    sw = DIFF_STEP_HEADS * w
    table = pl.BlockSpec((t, LANES), lambda bi, hi, ti: (0, 0))
    kern = functools.partial(_diff_kernel, tq=tq, n_ctx=n_ctx, out_scale=1.0 - lam_init)
    return pl.pallas_call(
        kern,
        grid=(b, DIFF_HEADS // DIFF_STEP_HEADS, t // tq),
        in_specs=[pl.BlockSpec(memory_space=pltpu.SMEM),
                  pl.BlockSpec((1, tq, sw), lambda bi, hi, ti: (bi, ti, COL_DQ // sw + hi)),
                  pl.BlockSpec((1, t, sw), lambda bi, hi, ti: (bi, 0, COL_DK // sw + hi)),
                  pl.BlockSpec((1, t, sw), lambda bi, hi, ti: (bi, 0, COL_DV // sw + hi)),
                  pl.BlockSpec((1, w), lambda bi, hi, ti: (0, 0)), table, table],
        out_specs=pl.BlockSpec((1, tq, sw), lambda bi, hi, ti: (bi, ti, hi)),
        out_shape=jax.ShapeDtypeStruct((b, t, GROUP_WIDTH), BF16),
        scratch_shapes=[pltpu.VMEM((sw, t), BF16), pltpu.VMEM((t, sw), BF16)],
        compiler_params=_params(("arbitrary", "arbitrary", "arbitrary")),
        name="diff_attn",
    )(lam.reshape(1), p_all, p_all, p_all, subln.reshape(1, w), cos_t, sin_t)


def _gelu_tanh(x):
    return 0.5 * x * (1.0 + jnp.tanh(math.sqrt(2.0 / math.pi) * (x + 0.044715 * (x * x * x))))


def _sgu_kernel(z_ref, g_ref, b_ref, w_ref, bs_ref, o_ref, *, tm):
    z = _gelu_tanh(z_ref[0])
    u = z[:, :GROUP_WIDTH]
    v = z[:, GROUP_WIDTH:]
    mu = jnp.mean(v, axis=-1, keepdims=True)
    var = jnp.mean(jnp.square(v - mu), axis=-1, keepdims=True)
    vn = ((v - mu) * lax.rsqrt(var + EPS) * g_ref[...] + b_ref[...]).astype(BF16)
    for g in range(SGU_GROUPS):
        wg = w_ref[g].astype(BF16)
        bias = bs_ref[:, g:g + 1]
        cs = slice(g * LANES, (g + 1) * LANES)
        for c in range(tm // SGU_CHUNK):
            rs = slice(c * SGU_CHUNK, (c + 1) * SGU_CHUNK)
            sv = jnp.dot(wg, vn[rs, cs], preferred_element_type=F32) + bias
            o_ref[0, rs, cs] = (u[rs, cs] * sv).astype(BF16)


def _sgu(p_all, ln_g, ln_b, w_s, b_s, tm=768):
    b, t, _ = p_all.shape
    zw = 2 * GROUP_WIDTH
    kern = functools.partial(_sgu_kernel, tm=tm)
    return pl.pallas_call(
        kern,
        grid=(b, t // tm),
        in_specs=[pl.BlockSpec((1, tm, zw), lambda bi, ti: (bi, ti, COL_Z // zw)),
                  pl.BlockSpec((1, GROUP_WIDTH), lambda bi, ti: (0, 0)),
                  pl.BlockSpec((1, GROUP_WIDTH), lambda bi, ti: (0, 0)),
                  pl.BlockSpec(w_s.shape, lambda bi, ti: (0, 0, 0)),
                  pl.BlockSpec((SGU_CHUNK, SGU_GROUPS), lambda bi, ti: (0, 0))],
        out_specs=pl.BlockSpec((1, tm, GROUP_WIDTH), lambda bi, ti: (bi, ti, 0)),
        out_shape=jax.ShapeDtypeStruct((b, t, GROUP_WIDTH), BF16),
        compiler_params=_params(("arbitrary", "arbitrary")),
        name="sgu",
    )(p_all, ln_g.reshape(1, GROUP_WIDTH), ln_b.reshape(1, GROUP_WIDTH), w_s, b_s.T)


def _win_band_start(step, t):
    return jnp.clip(step * WIN_QB - WIN_BLOCK, 0, t - WIN_BAND)


def _win_bias(n_ctx, t):
    step = jnp.arange(t // WIN_QB, dtype=jnp.int32)[:, None, None]
    qpos = step * WIN_QB + jnp.arange(WIN_QB, dtype=jnp.int32)[None, :, None]
    kpos = _win_band_start(step, t) + jnp.arange(WIN_BAND, dtype=jnp.int32)[None, None, :]
    valid = (kpos >= n_ctx) & (qpos >= n_ctx) & (jnp.abs(kpos - qpos) <= WINDOW)
    return jnp.where(valid, 0.0, NEG_INF).astype(F32)


def _win_kernel(sink_ref, q_ref, k_ref, v_ref, bias_ref, cos_ref, sin_ref, o_ref, kt_ref, vv_ref, *, n_ctx):
    g = pl.program_id(1)
    n = pl.program_id(2)
    t = k_ref.shape[1]
    blk = WIN_QB
    band = WIN_BAND
    lane = lax.broadcasted_iota(jnp.int32, (1, LANES), 1)
    low = lane < HEAD_DIM

    @pl.when(n == 0)
    def _():
        mine = (lane >= g * HEAD_DIM) & (lane < (g + 1) * HEAD_DIM)
        km = jnp.where(mine, _rope(k_ref[0], cos_ref[...], sin_ref[...]), 0.0)
        vm = jnp.where(mine, v_ref[0], 0.0)
        kt_ref[...] = (km + pltpu.roll(km, HEAD_DIM, 1)).T.astype(BF16)
        vv_ref[...] = (vm + pltpu.roll(vm, HEAD_DIM, 1)).astype(BF16)

    rows = WIN_REP * blk
    head = lax.broadcasted_iota(jnp.int32, (rows, 1), 0) // blk
    start = pl.multiple_of(_win_band_start(n, t), WIN_BLOCK)

    q_rows = pl.ds(pl.multiple_of(n * blk, blk), blk)
    cos = cos_ref[q_rows, :]
    sin = sin_ref[q_rows, :]
    pieces = []
    for pair in range(WIN_REP // 2):
        qp = _rope(q_ref[0, :, pair * LANES:(pair + 1) * LANES], cos, sin) * (HEAD_DIM ** -0.5)
        pieces += [jnp.where(low, qp, 0.0), jnp.where(low, 0.0, qp)]
    qs = jnp.concatenate(pieces, axis=0).astype(BF16)
    s_c = jnp.dot(qs, kt_ref[:, 0:n_ctx], preferred_element_type=F32)
    s_b = jnp.dot(qs, kt_ref[:, pl.ds(start, band)], preferred_element_type=F32)
    s_b = (s_b.reshape(WIN_REP, blk, band) + bias_ref[...]).reshape(rows, band)
    sink = jnp.full((rows, 1), sink_ref[g * WIN_REP], F32)
    for r in range(1, WIN_REP):
        sink = jnp.where(head == r, sink_ref[g * WIN_REP + r], sink)
    m = jnp.maximum(jnp.maximum(jnp.max(s_c, axis=-1, keepdims=True),
                                jnp.max(s_b, axis=-1, keepdims=True)), sink)
    e_c = jnp.exp(s_c - m)
    e_b = jnp.exp(s_b - m)
    den = jnp.sum(e_c, axis=-1, keepdims=True) + jnp.sum(e_b, axis=-1, keepdims=True) + jnp.exp(sink - m)
    o = (jnp.dot(e_c.astype(BF16), vv_ref[0:n_ctx, :], preferred_element_type=F32)
         + jnp.dot(e_b.astype(BF16), vv_ref[pl.ds(start, band), :], preferred_element_type=F32)) / den
    for pair in range(WIN_REP // 2):
        lo = o[(2 * pair) * blk:(2 * pair + 1) * blk]
        hi = o[(2 * pair + 1) * blk:(2 * pair + 2) * blk]
        o_ref[0, :, pair * LANES:(pair + 1) * LANES] = jnp.where(low, lo, hi).astype(BF16)


def _win(p_all, sink, cos_t, sin_t, n_ctx):
    b, t, _ = p_all.shape
    qw = WIN_REP * HEAD_DIM
    table = pl.BlockSpec((t, LANES), lambda bi, gi, ni: (0, 0))
    assert t % WIN_QB == 0 and t >= WIN_BAND
    lo = -(-(n_ctx + WIN_BLOCK) // WIN_QB)
    hi = (t - WIN_BAND + WIN_BLOCK) // WIN_QB

    def bias_block(bi, gi, ni):
        return (jnp.where((ni >= lo) & (ni <= hi), lo, ni), 0, 0)

    kern = functools.partial(_win_kernel, n_ctx=n_ctx)
    return pl.pallas_call(
        kern,
        grid=(b, WIN_KV_HEADS, t // WIN_QB),
        in_specs=[pl.BlockSpec(memory_space=pltpu.SMEM),
                  pl.BlockSpec((1, WIN_QB, qw), lambda bi, gi, ni: (bi, ni, COL_WQ // qw + gi)),
                  pl.BlockSpec((1, t, LANES), lambda bi, gi, ni: (bi, 0, COL_WK // LANES)),
                  pl.BlockSpec((1, t, LANES), lambda bi, gi, ni: (bi, 0, COL_WV // LANES)),
                  pl.BlockSpec((1, WIN_QB, WIN_BAND), bias_block), table, table],
        out_specs=pl.BlockSpec((1, WIN_QB, qw), lambda bi, gi, ni: (bi, ni, gi)),
        out_shape=jax.ShapeDtypeStruct((b, t, GROUP_WIDTH), BF16),
        scratch_shapes=[pltpu.VMEM((LANES, t), BF16), pltpu.VMEM((t, LANES), BF16)],
        compiler_params=_params(("arbitrary", "arbitrary", "arbitrary")),
        name="win_attn",
    )(sink, p_all, p_all, p_all, _win_bias(n_ctx, t), cos_t, sin_t)


def _outproj_kernel(a_ref, b_ref, c_ref, d_ref, w_ref, h_ref, m_ref, o_ref, wb_ref, *, tm, n_ctx):
    ti = pl.program_id(2)

    @pl.when((pl.program_id(1) == 0) & (ti == 0))
    def _():
        wb_ref[...] = w_ref[...].astype(BF16)

    y = None
    for i, r in enumerate((a_ref, b_ref, c_ref, d_ref)):
        part = jnp.dot(r[0], wb_ref[i * GROUP_WIDTH:(i + 1) * GROUP_WIDTH, :], preferred_element_type=F32)
        y = part if y is None else y + part
    gate = _mod_row(m_ref, 2, _row_is_ctx(ti, tm, n_ctx))
    o_ref[0] = h_ref[0] + gate * y


def _outproj(mixes, w, layer, h, modsel, n_ctx, tm=768, tn=1024):
    b, t, d = h.shape
    tn = min(tn, d)
    assert d % tn == 0 and t % tm == 0
    kern = functools.partial(_outproj_kernel, tm=tm, n_ctx=n_ctx)
    mix_spec = pl.BlockSpec((1, tm, GROUP_WIDTH), lambda j, bi, ti: (bi, ti, 0))
    return pl.pallas_call(
        kern,
        grid=(d // tn, b, t // tm),
        in_specs=[mix_spec, mix_spec, mix_spec, mix_spec,
                  pl.BlockSpec((None, 4 * GROUP_WIDTH, tn), lambda j, bi, ti: (layer, 0, j)),
                  pl.BlockSpec((1, tm, tn), lambda j, bi, ti: (bi, ti, j)),
                  pl.BlockSpec((None, 2, 6, tn), lambda j, bi, ti: (bi, 0, 0, j))],
        out_specs=pl.BlockSpec((1, tm, tn), lambda j, bi, ti: (bi, ti, j)),
        out_shape=jax.ShapeDtypeStruct((b, t, d), F32),
        scratch_shapes=[pltpu.VMEM((4 * GROUP_WIDTH, tn), BF16)],
        compiler_params=_params(("arbitrary", "arbitrary", "arbitrary")),
        name="outproj",
    )(*mixes, w, h, modsel)


def _ffn_in_kernel(x_ref, m_ref, g_ref, o_ref, *, tm, n_ctx, row0):
    def store(rs, y):
        o_ref[0, rs, :] = y
    _norm_modulate_rows(x_ref, g_ref, m_ref, 3, row0 + pl.program_id(1) * tm, tm, n_ctx, store)


def _route_kernel(x_ref, m_ref, g_ref, r_ref, o_ref, gate_ref, sel_ref, *, tm, n_ctx, row0):
    _ffn_in_kernel(x_ref, m_ref, g_ref, o_ref, tm=tm, n_ctx=n_ctx, row0=row0)
    f = o_ref[0]
    lane = lax.broadcasted_iota(jnp.int32, (1, LANES), 1)
    logits = jnp.full((tm, LANES), -jnp.inf, F32)
    for e in range(N_EXPERTS):
        logits = jnp.where(lane == e, jnp.sum(f * r_ref[e:e + 1, :], axis=-1, keepdims=True), logits)
    m1 = jnp.max(logits, axis=-1, keepdims=True)
    i1 = jnp.min(jnp.where(logits == m1, lane, LANES), axis=-1, keepdims=True)
    rest = jnp.where(lane == i1, -jnp.inf, logits)
    m2 = jnp.max(rest, axis=-1, keepdims=True)
    i2 = jnp.min(jnp.where(rest == m2, lane, LANES), axis=-1, keepdims=True)
    e2 = jnp.exp(m2 - m1)
    w1 = 1.0 / (1.0 + e2)
    w2 = e2 / (1.0 + e2)
    gate_ref[0] = jnp.where(lane == i1, w1, 0.0) + jnp.where(lane == i2, w2, 0.0)
    sel_ref[0] = ((lane == i1) | (lane == i2)).astype(jnp.int32)


def _ffn_in_route(h, modsel, g, n_ctx, row0, router, tm=256):
    b, t, d = h.shape
    nt = (t - row0) // tm
    off = row0 // tm
    in_specs = [pl.BlockSpec((1, tm, d), lambda bi, ti: (bi, ti + off, 0)),
                pl.BlockSpec((None, 2, 6, d), lambda bi, ti: (bi, 0, 0, 0)),
                pl.BlockSpec((1, d), lambda bi, ti: (0, 0))]
    f_spec = pl.BlockSpec((1, tm, d), lambda bi, ti: (bi, ti, 0))
    f_shape = jax.ShapeDtypeStruct((b, t - row0, d), F32)
    assert router.shape == (d, N_EXPERTS)
    e_spec = pl.BlockSpec((1, tm, LANES), lambda bi, ti: (bi, ti, 0))
    kern = functools.partial(_route_kernel, tm=tm, n_ctx=n_ctx, row0=row0)
    return pl.pallas_call(
        kern, grid=(b, nt),
        in_specs=in_specs + [pl.BlockSpec((N_EXPERTS, d), lambda bi, ti: (0, 0))],
        out_specs=[f_spec, e_spec, e_spec],
        out_shape=[f_shape, jax.ShapeDtypeStruct((b, t - row0, LANES), F32),
                   jax.ShapeDtypeStruct((b, t - row0, LANES), jnp.int32)],
        compiler_params=_params(("arbitrary", "arbitrary")), name="ffn_in_route",
    )(h, modsel, g.reshape(1, d), router.T)


FFN_TM = 1152
FFN_SUB = 192
FFN_VMEM_LIMIT = 60 * 1024 * 1024


def _swiglu_slice(xs, w1, w3, w2):
    h1 = jnp.dot(xs, w1.astype(BF16), preferred_element_type=F32)
    h3 = jnp.dot(xs, w3.astype(BF16), preferred_element_type=F32)
    act = (h1 * _sigmoid(h1) * h3).astype(BF16)
    return jnp.dot(act, w2.astype(BF16), preferred_element_type=F32)


def _ffn_kernel(te_ref, tr_ref, x_ref, w1_ref, w3_ref, w2_ref, o_ref, xs_ref):
    i = pl.program_id(0)
    j = pl.program_id(1)
    n_blocks = (tr_ref[i] + FFN_SUB - 1) // FFN_SUB

    @pl.when(j == 0)
    def _():
        o_ref[...] = jnp.zeros_like(o_ref)
        xs_ref[...] = x_ref[...].astype(BF16)

    for nb in range(1, FFN_TM // FFN_SUB + 1):
        @pl.when(n_blocks == nb)
        def _(nb=nb):
            m = nb * FFN_SUB
            o_ref[0:m, :] += _swiglu_slice(xs_ref[0:m, :], w1_ref[0], w3_ref[0], w2_ref[0])


def _ffn(x_rows, w1, w3, w2, tile_expert, tile_rows, tf=256):
    nr, d = x_rows.shape
    n_exp, _, f = w1.shape
    nj = f // tf

    def wcol(i, j, te, tr):
        return (te[i], 0, jnp.where(tr[i] > 0, j, nj - 1))

    def wrow(i, j, te, tr):
        return (te[i], jnp.where(tr[i] > 0, j, nj - 1), 0)

    return pl.pallas_call(
        _ffn_kernel,
        grid_spec=pltpu.PrefetchScalarGridSpec(
            num_scalar_prefetch=2,
            grid=(nr // FFN_TM, nj),
            in_specs=[pl.BlockSpec((FFN_TM, d), lambda i, j, te, tr: (i, 0)),
                      pl.BlockSpec((1, d, tf), wcol),
                      pl.BlockSpec((1, d, tf), wcol),
                      pl.BlockSpec((1, tf, d), wrow)],
            out_specs=pl.BlockSpec((FFN_TM, d), lambda i, j, te, tr: (i, 0)),
            scratch_shapes=[pltpu.VMEM((FFN_TM, d), BF16)]),
        out_shape=jax.ShapeDtypeStruct((nr, d), F32),
        compiler_params=_params(("arbitrary", "arbitrary"), FFN_VMEM_LIMIT),
        name="ffn",
    )(tile_expert, tile_rows, x_rows, w1, w3, w2)


def _ffn_dense_kernel(h_ref, m_ref, g_ref, w1_ref, w3_ref, w2_ref, o_ref, xs_ref, *, tm, n_ctx):
    ti = pl.program_id(1)
    j = pl.program_id(2)

    @pl.when(j == 0)
    def _():
        def store(rs, y):
            xs_ref[rs, :] = y.astype(BF16)
        _norm_modulate_rows(h_ref, g_ref, m_ref, 3, ti * tm, tm, n_ctx, store)
        o_ref[...] = jnp.zeros_like(o_ref)

    o_ref[0] += _swiglu_slice(xs_ref[...], w1_ref[...], w3_ref[...], w2_ref[...])

    @pl.when(j == pl.num_programs(2) - 1)
    def _():
        gate = _mod_row(m_ref, 5, _row_is_ctx(ti, tm, n_ctx))
        o_ref[0] = h_ref[0] + gate * o_ref[0]


def _ffn_dense(h, modsel, g, w1, w3, w2, layer, n_ctx, tf=256):
    b, t, d = h.shape
    f = w1.shape[2]
    tm = t // 2
    assert t % tm == 0 and f % tf == 0 and tm % NORM_ROWS == 0 and n_ctx % NORM_ROWS == 0
    kern = functools.partial(_ffn_dense_kernel, tm=tm, n_ctx=n_ctx)
    return pl.pallas_call(
        kern, grid=(b, t // tm, f // tf),
        in_specs=[pl.BlockSpec((1, tm, d), lambda bi, ti, j: (bi, ti, 0)),
                  pl.BlockSpec((None, 2, 6, d), lambda bi, ti, j: (bi, 0, 0, 0)),
                  pl.BlockSpec((1, d), lambda bi, ti, j: (0, 0)),
                  pl.BlockSpec((None, d, tf), lambda bi, ti, j: (layer, 0, j)),
                  pl.BlockSpec((None, d, tf), lambda bi, ti, j: (layer, 0, j)),
                  pl.BlockSpec((None, tf, d), lambda bi, ti, j: (layer, j, 0))],
        out_specs=pl.BlockSpec((1, tm, d), lambda bi, ti, j: (bi, ti, 0), pipeline_mode=pl.Buffered(1)),
        out_shape=jax.ShapeDtypeStruct((b, t, d), F32),
        scratch_shapes=[pltpu.VMEM((tm, d), BF16)],
        compiler_params=_params(("arbitrary", "arbitrary", "arbitrary")), name="ffn_dense",
    )(h, modsel, g.reshape(1, d), w1, w3, w2)


def _moe_final_kernel(h_ref, y0_ref, y1_ref, w0_ref, w1_ref, m_ref, g_ref, o_ref):
    y = w0_ref[0] * y0_ref[0] + w1_ref[0] * y1_ref[0]
    hn = h_ref[0] + m_ref[1, 5:6, :] * y
    ms = jnp.mean(hn * hn, axis=-1, keepdims=True)
    o_ref[0] = hn * lax.rsqrt(ms + EPS) * g_ref[...]


def _moe_final(h, y0, y1, w0, w1, modsel, g, row0, tm=256):
    b, t, d = h.shape
    n_lat = t - row0
    off = row0 // tm
    spec = pl.BlockSpec((1, tm, d), lambda bi, ti: (bi, ti, 0))
    wspec = pl.BlockSpec((1, tm, 1), lambda bi, ti: (bi, ti, 0))
    return pl.pallas_call(
        _moe_final_kernel, grid=(b, n_lat // tm),
        in_specs=[pl.BlockSpec((1, tm, d), lambda bi, ti: (bi, ti + off, 0)), spec, spec, wspec, wspec,
                  pl.BlockSpec((None, 2, 6, d), lambda bi, ti: (bi, 0, 0, 0)),
                  pl.BlockSpec((1, d), lambda bi, ti: (0, 0))],
        out_specs=spec, out_shape=jax.ShapeDtypeStruct((b, n_lat, d), F32),
        compiler_params=_params(("arbitrary", "arbitrary")), name="moe_final",
    )(h, y0, y1, w0, w1, modsel, g.reshape(1, d))


def _routing_tables(sel, gates, n_tiles):
    n, n_exp = sel.shape
    seli = sel.astype(jnp.int32)
    counts = jnp.sum(seli, axis=0)
    rank = jnp.cumsum(seli, axis=0) - seli
    tiles_e = (counts + FFN_TM - 1) // FFN_TM
    tile_end = jnp.cumsum(tiles_e)
    tile_start = tile_end - tiles_e
    pos = tile_start[None, :] * FFN_TM + rank
    n_rows = n_tiles * FFN_TM
    expert = jnp.arange(n_exp, dtype=jnp.int32)[None, :]
    e_lo = jnp.min(jnp.where(sel, expert, n_exp), axis=1, keepdims=True)
    e_hi = jnp.max(jnp.where(sel, expert, -1), axis=1, keepdims=True)
    pick = lambda a, e: jnp.sum(jnp.where(expert == e, a, 0), axis=1)
    pos0, pos1 = pick(pos, e_lo), pick(pos, e_hi)
    w0, w1 = pick(gates, e_lo), pick(gates, e_hi)
    tok = jnp.arange(n, dtype=jnp.int32)
    src = (jnp.arange(n_rows, dtype=jnp.int32) % n).at[jnp.concatenate([pos0, pos1])].set(
        jnp.concatenate([tok, tok]), mode="promise_in_bounds", unique_indices=True)

    tile = jnp.arange(n_tiles, dtype=jnp.int32)
    te = jnp.minimum(jnp.sum((tile_end[None, :] <= tile[:, None]).astype(jnp.int32), axis=1), n_exp - 1)
    tr = jnp.clip(counts[te] - (tile - tile_start[te]) * FFN_TM, 0, FFN_TM)
    tr = jnp.where(tile < tile_end[-1], tr, 0)
    return src, pos0, pos1, w0, w1, te.astype(jnp.int32), tr.astype(jnp.int32)


def kernel(x, c, ctx, c_ctx, w_mod, b_mod, norm_mix, norm_ffn, w_in, w_out, pool_w, pool_scale, diff_lam,
           diff_subln, sgu_ln_g, sgu_ln_b, sgu_w, sgu_b, win_sink, ffn_w1, ffn_w3, ffn_w2, moe_router,
           moe_w1, moe_w3, moe_w2, norm_final):
    b, n_lat, d = x.shape
    n_ctx = ctx.shape[1]
    t = n_ctx + n_lat
    depth = w_mod.shape[0]
    segments = ((0, n_ctx), (n_ctx, n_lat))

    c_rows = jnp.zeros((8, d), F32).at[:b].set(c).at[b].set(c_ctx)
    mod = _adaln(c_rows, w_mod, b_mod).reshape(depth, 8, 6, d)
    cos_t, sin_t = _rope_tables(n_ctx, n_lat)
    h = jnp.concatenate([ctx, x], axis=1)

    out = None
    for l in range(depth):
        last = l == depth - 1
        lat_m = mod[l, :b]
        ctx_m = jnp.broadcast_to(mod[l, b:b + 1], lat_m.shape)
        modsel = jnp.stack([ctx_m, lat_m], axis=1)

        p_all = _inproj(h, modsel, norm_mix[l], w_in, l, n_ctx)

        lam_init = 0.8 - 0.6 * math.exp(-0.3 * l)
        lq1, lk1, lq2, lk2 = diff_lam[l].astype(F32)
        lam = jnp.exp(jnp.sum(lq1 * lk1)) - jnp.exp(jnp.sum(lq2 * lk2)) + lam_init

        mixes = (_pool(p_all, pool_w[l], pool_scale[l], segments),
                 _diff(p_all, lam, diff_subln[l], cos_t, sin_t, n_ctx, lam_init),
                 _sgu(p_all, sgu_ln_g[l], sgu_ln_b[l], sgu_w[l], sgu_b[l]),
                 _win(p_all, win_sink[l], cos_t, sin_t, n_ctx))
        h = _outproj(mixes, w_out, l, h, modsel, n_ctx)

        i = l // 2
        row0 = n_ctx if last else 0
        rows = b * (t - row0)
        if l % 2 == 0:
            assert not last
            h = _ffn_dense(h, modsel, norm_ffn[l], ffn_w1, ffn_w3, ffn_w2, i, n_ctx)
        else:
            assert last
            f, gates, sel = _ffn_in_route(h, modsel, norm_ffn[l], n_ctx, row0, moe_router[i])
            f = f.reshape(rows, d)
            n_exp = moe_w1.shape[1]
            n_tiles = (2 * rows) // FFN_TM + n_exp
            src, pos0, pos1, w0, w1, te, tr = _routing_tables(
                sel.reshape(rows, LANES)[:, :n_exp] > 0, gates.reshape(rows, LANES)[:, :n_exp], n_tiles)
            rows_of = lambda a, idx: a.at[idx].get(mode="promise_in_bounds")
            ys = _ffn(rows_of(f, src), moe_w1[i], moe_w3[i], moe_w2[i], te, tr)
            shp = (b, t - row0, d)
            out = _moe_final(h, rows_of(ys, pos0).reshape(shp), rows_of(ys, pos1).reshape(shp),
                             w0.reshape(b, t - row0, 1), w1.reshape(b, t - row0, 1), modsel, norm_final, row0)
    return out
```

```python
import functools
import math

import jax
import jax.numpy as jnp
from jax import lax
from jax.experimental import pallas as pl
from jax.experimental.pallas import tpu as pltpu

F32 = jnp.float32
BF16 = jnp.bfloat16

GRID_W = 64
GROUP_WIDTH = 512
HEAD_DIM = 64
ROPE_BASE = 10000.0
EPS = 1e-6
NEG_INF = -1e30
LANES = 128

POOL_WINDOWS = (2, 4, 8, 16)
POOL_PAD = 16
DIFF_HEADS = GROUP_WIDTH // (2 * HEAD_DIM)
DIFF_STEP_HEADS = 4
SGU_CHUNK = 128
SGU_GROUPS = 4
WIN_HEADS = GROUP_WIDTH // HEAD_DIM
WIN_KV_HEADS = 2
WIN_REP = WIN_HEADS // WIN_KV_HEADS
WINDOW = 128
WIN_BLOCK = 128
WIN_QB = 2 * WIN_BLOCK
WIN_BAND = WIN_QB + 2 * WIN_BLOCK
N_EXPERTS = 8

COL_POOL = 0
COL_DQ = GROUP_WIDTH
COL_DK = 2 * GROUP_WIDTH
COL_DV = 3 * GROUP_WIDTH
COL_Z = 4 * GROUP_WIDTH
COL_WQ = 6 * GROUP_WIDTH
COL_WK = COL_WQ + WIN_HEADS * HEAD_DIM
COL_WV = COL_WK + WIN_KV_HEADS * HEAD_DIM
IN_COLS = COL_WV + WIN_KV_HEADS * HEAD_DIM

VMEM_LIMIT = 56 * 1024 * 1024
NORM_ROWS = 32


def _params(sem, vmem=VMEM_LIMIT):
    return pltpu.CompilerParams(dimension_semantics=sem, vmem_limit_bytes=vmem)


def _sigmoid(x):
    return 1.0 / (1.0 + jnp.exp(-x))


def _row_is_ctx(tile_idx, tm, n_ctx):
    rows = tile_idx * tm + lax.broadcasted_iota(jnp.int32, (tm, 1), 0)
    return rows < n_ctx


def _mod_row(m_ref, k, is_ctx):
    return jnp.where(is_ctx, m_ref[0, k:k + 1, :], m_ref[1, k:k + 1, :])


def _norm_modulate_rows(x_ref, g_ref, m_ref, k_shift, first_row, tm, n_ctx, store):
    def body(r, carry):
        r0 = pl.multiple_of(r * NORM_ROWS, NORM_ROWS)
        which = ((first_row + r0) >= n_ctx).astype(jnp.int32)
        rs = pl.ds(r0, NORM_ROWS)
        x = x_ref[0, rs, :]
        ms = jnp.mean(x * x, axis=-1, keepdims=True)
        a = g_ref[...] * (1.0 + m_ref[which, k_shift + 1:k_shift + 2, :])
        store(rs, x * lax.rsqrt(ms + EPS) * a + m_ref[which, k_shift:k_shift + 1, :])
        return carry
    lax.fori_loop(0, tm // NORM_ROWS, body, 0, unroll=2)


def _adaln_kernel(c_ref, w_ref, b_ref, o_ref):
    cv = c_ref[...]
    s = (cv * _sigmoid(cv)).astype(BF16)
    o_ref[0] = jnp.dot(s, w_ref[0].astype(BF16), preferred_element_type=F32) + b_ref[0]


def _adaln(c_rows, w_mod, b_mod, tn=1024):
    depth, d, n = w_mod.shape
    r = c_rows.shape[0]
    return pl.pallas_call(
        _adaln_kernel,
        grid=(depth, n // tn),
        in_specs=[pl.BlockSpec((r, d), lambda l, j: (0, 0)),
                  pl.BlockSpec((1, d, tn), lambda l, j: (l, 0, j)),
                  pl.BlockSpec((1, 1, tn), lambda l, j: (l, 0, j))],
        out_specs=pl.BlockSpec((1, r, tn), lambda l, j: (l, 0, j)),
        out_shape=jax.ShapeDtypeStruct((depth, r, n), F32),
        compiler_params=_params(("arbitrary", "arbitrary")),
        name="adaln",
    )(c_rows, w_mod, b_mod.reshape(depth, 1, n))


def _inproj_kernel(x_ref, m_ref, g_ref, w_ref, o_ref, xn_ref, *, tm, n_ctx):
    @pl.when(pl.program_id(2) == 0)
    def _():
        def store(rs, y):
            xn_ref[rs, :] = y.astype(BF16)
        _norm_modulate_rows(x_ref, g_ref, m_ref, 0, pl.program_id(1) * tm, tm, n_ctx, store)

    o_ref[0] = jnp.dot(xn_ref[...], w_ref[...].astype(BF16), preferred_element_type=F32)


def _rope(a, cos, sin):
    lane = lax.broadcasted_iota(jnp.int32, (1, LANES), 1)
    first_half = (lane % HEAD_DIM) < (HEAD_DIM // 2)
    partner = jnp.where(first_half, pltpu.roll(a, LANES - HEAD_DIM // 2, 1), pltpu.roll(a, HEAD_DIM // 2, 1))
    return a * cos + partner * sin


def _inproj(h, modsel, g, w, layer, n_ctx, tn=768):
    b, t, d = h.shape
    tm = t // 2
    assert tm % NORM_ROWS == 0 and n_ctx % NORM_ROWS == 0 and IN_COLS % tn == 0
    kern = functools.partial(_inproj_kernel, tm=tm, n_ctx=n_ctx)
    return pl.pallas_call(
        kern,
        grid=(b, t // tm, IN_COLS // tn),
        in_specs=[pl.BlockSpec((1, tm, d), lambda bi, ti, j: (bi, ti, 0)),
                  pl.BlockSpec((None, 2, 6, d), lambda bi, ti, j: (bi, 0, 0, 0)),
                  pl.BlockSpec((1, d), lambda bi, ti, j: (0, 0)),
                  pl.BlockSpec((None, d, tn), lambda bi, ti, j: (layer, 0, j))],
        out_specs=pl.BlockSpec((1, tm, tn), lambda bi, ti, j: (bi, ti, j)),
        out_shape=jax.ShapeDtypeStruct((b, t, IN_COLS), F32),
        scratch_shapes=[pltpu.VMEM((tm, d), BF16)],
        compiler_params=_params(("arbitrary", "arbitrary", "arbitrary")),
        name="inproj",
    )(h, modsel, g.reshape(1, d), w)


def _rope_tables(n_ctx, n_lat):
    n_rows = n_lat // GRID_W
    rows = jnp.repeat(jnp.arange(n_rows, dtype=F32), GRID_W)
    cols = jnp.tile(jnp.arange(GRID_W, dtype=F32), n_rows)
    n_freq = HEAD_DIM // 4
    inv_freq = ROPE_BASE ** (-jnp.arange(n_freq, dtype=F32) / n_freq)
    ang = jnp.concatenate([rows[:, None] * inv_freq, cols[:, None] * inv_freq], axis=-1)
    cos, sin = jnp.cos(ang), jnp.sin(ang)
    cos_h = jnp.concatenate([cos, cos], axis=-1)
    sin_h = jnp.concatenate([-sin, sin], axis=-1)
    cos_t = jnp.concatenate([jnp.ones((n_ctx, HEAD_DIM), F32), cos_h], axis=0)
    sin_t = jnp.concatenate([jnp.zeros((n_ctx, HEAD_DIM), F32), sin_h], axis=0)
    rep = LANES // HEAD_DIM
    return jnp.tile(cos_t, (1, rep)), jnp.tile(sin_t, (1, rep))


def _pool_kernel(a_ref, w_ref, s_ref, o_ref, pad_ref, *, segments):
    for g, win in enumerate(POOL_WINDOWS):
        half = win // 2
        cs = slice(g * LANES, (g + 1) * LANES)
        wg = w_ref[g].astype(BF16)
        for s0, n in segments:
            a = a_ref[0, s0:s0 + n, cs]
            pad_ref[0:POOL_PAD, :] = jnp.zeros((POOL_PAD, LANES), F32)
            pad_ref[POOL_PAD:POOL_PAD + n, :] = a
            pad_ref[POOL_PAD + n:2 * POOL_PAD + n, :] = jnp.zeros((POOL_PAD, LANES), F32)
            tot = pad_ref[POOL_PAD - half:POOL_PAD - half + n, :]
            for k in range(-half + 1, half):
                tot = tot + pad_ref[POOL_PAD + k:POOL_PAD + k + n, :]
            t = lax.broadcasted_iota(jnp.int32, (n, 1), 0)
            cnt = (jnp.minimum(t + half, n) - jnp.maximum(t - half, 0)).astype(F32)
            dlt = (tot / cnt - a).astype(BF16)
            y = jnp.dot(dlt, wg, preferred_element_type=F32)
            o_ref[0, s0:s0 + n, cs] = (y * s_ref[:, cs]).astype(BF16)


def _pool(p_all, w, scale, segments):
    b, t, _ = p_all.shape
    max_n = max(n for _, n in segments)
    kern = functools.partial(_pool_kernel, segments=segments)
    return pl.pallas_call(
        kern,
        grid=(b,),
        in_specs=[pl.BlockSpec((1, t, GROUP_WIDTH), lambda bi: (bi, 0, COL_POOL // GROUP_WIDTH)),
                  pl.BlockSpec(w.shape, lambda bi: (0, 0, 0)),
                  pl.BlockSpec((1, GROUP_WIDTH), lambda bi: (0, 0))],
        out_specs=pl.BlockSpec((1, t, GROUP_WIDTH), lambda bi: (bi, 0, 0)),
        out_shape=jax.ShapeDtypeStruct((b, t, GROUP_WIDTH), BF16),
        scratch_shapes=[pltpu.VMEM((max_n + 2 * POOL_PAD, LANES), F32)],
        compiler_params=_params(("arbitrary",)),
        name="pool",
    )(p_all, w, scale.reshape(1, GROUP_WIDTH))


def _diff_kernel(lam_ref, q_ref, k_ref, v_ref, g_ref, cos_ref, sin_ref, o_ref, kt_ref, vb_ref,
                 *, tq, n_ctx, out_scale):
    ti = pl.program_id(2)
    lam = lam_ref[0]
    w = 2 * HEAD_DIM

    @pl.when(ti == 0)
    def _():
        for hh in range(DIFF_STEP_HEADS):
            cs = slice(hh * w, (hh + 1) * w)
            kt_ref[cs, :] = _rope(k_ref[0, :, cs], cos_ref[...], sin_ref[...]).T.astype(BF16)
        vb_ref[...] = v_ref[0].astype(BF16)

    def attend(n_keys):
        lane = lax.broadcasted_iota(jnp.int32, (1, w), 1)
        rows = pl.ds(pl.multiple_of(ti * tq, tq), tq)
        cos = cos_ref[rows, :]
        sin = sin_ref[rows, :]
        for hh in range(DIFF_STEP_HEADS):
            cs = slice(hh * w, (hh + 1) * w)
            q = _rope(q_ref[0, :, cs], cos, sin) * (HEAD_DIM ** -0.5)
            q1 = jnp.where(lane < HEAD_DIM, q, 0.0).astype(BF16)
            q2 = jnp.where(lane >= HEAD_DIM, q, 0.0).astype(BF16)
            kt = kt_ref[cs, 0:n_keys]
            v = vb_ref[0:n_keys, cs]
            s1 = jnp.dot(q1, kt, preferred_element_type=F32)
            s2 = jnp.dot(q2, kt, preferred_element_type=F32)
            e1 = jnp.exp(s1 - jnp.max(s1, axis=-1, keepdims=True))
            e2 = jnp.exp(s2 - jnp.max(s2, axis=-1, keepdims=True))
            r1 = 1.0 / jnp.sum(e1, axis=-1, keepdims=True)
            r2 = lam / jnp.sum(e2, axis=-1, keepdims=True)
            o = (jnp.dot(e1.astype(BF16), v, preferred_element_type=F32) * r1
                 - jnp.dot(e2.astype(BF16), v, preferred_element_type=F32) * r2)
            ms = jnp.mean(o * o, axis=-1, keepdims=True)
            o_ref[0, :, cs] = (o * lax.rsqrt(ms + EPS) * g_ref[...] * out_scale).astype(BF16)

    @pl.when(ti * tq < n_ctx)
    def _():
        attend(n_ctx)

    @pl.when(ti * tq >= n_ctx)
    def _():
        attend(k_ref.shape[1])


def _diff(p_all, lam, subln, cos_t, sin_t, n_ctx, lam_init, tq=256):
    b, t, _ = p_all.shape
    assert n_ctx % tq == 0 and t % tq == 0 and DIFF_HEADS % DIFF_STEP_HEADS == 0
    w = 2 * HEAD_DIM
    sw = DIFF_STEP_HEADS * w
    table = pl.BlockSpec((t, LANES), lambda bi, hi, ti: (0, 0))
    kern = functools.partial(_diff_kernel, tq=tq, n_ctx=n_ctx, out_scale=1.0 - lam_init)
    return pl.pallas_call(
        kern,
        grid=(b, DIFF_HEADS // DIFF_STEP_HEADS, t // tq),
        in_specs=[pl.BlockSpec(memory_space=pltpu.SMEM),
                  pl.BlockSpec((1, tq, sw), lambda bi, hi, ti: (bi, ti, COL_DQ // sw + hi)),
                  pl.BlockSpec((1, t, sw), lambda bi, hi, ti: (bi, 0, COL_DK // sw + hi)),
                  pl.BlockSpec((1, t, sw), lambda bi, hi, ti: (bi, 0, COL_DV // sw + hi)),
                  pl.BlockSpec((1, w), lambda bi, hi, ti: (0, 0)), table, table],
        out_specs=pl.BlockSpec((1, tq, sw), lambda bi, hi, ti: (bi, ti, hi)),
        out_shape=jax.ShapeDtypeStruct((b, t, GROUP_WIDTH), BF16),
        scratch_shapes=[pltpu.VMEM((sw, t), BF16), pltpu.VMEM((t, sw), BF16)],
        compiler_params=_params(("arbitrary", "arbitrary", "arbitrary")),
        name="diff_attn",
    )(lam.reshape(1), p_all, p_all, p_all, subln.reshape(1, w), cos_t, sin_t)


def _gelu_tanh(x):
    return 0.5 * x * (1.0 + jnp.tanh(math.sqrt(2.0 / math.pi) * (x + 0.044715 * (x * x * x))))


def _sgu_kernel(z_ref, g_ref, b_ref, w_ref, bs_ref, o_ref, *, tm):
    z = _gelu_tanh(z_ref[0])
    u = z[:, :GROUP_WIDTH]
    v = z[:, GROUP_WIDTH:]
    mu = jnp.mean(v, axis=-1, keepdims=True)
    var = jnp.mean(jnp.square(v - mu), axis=-1, keepdims=True)
    vn = ((v - mu) * lax.rsqrt(var + EPS) * g_ref[...] + b_ref[...]).astype(BF16)
    for g in range(SGU_GROUPS):
        wg = w_ref[g].astype(BF16)
        bias = bs_ref[:, g:g + 1]
        cs = slice(g * LANES, (g + 1) * LANES)
        for c in range(tm // SGU_CHUNK):
            rs = slice(c * SGU_CHUNK, (c + 1) * SGU_CHUNK)
            sv = jnp.dot(wg, vn[rs, cs], preferred_element_type=F32) + bias
            o_ref[0, rs, cs] = (u[rs, cs] * sv).astype(BF16)


def _sgu(p_all, ln_g, ln_b, w_s, b_s, tm=768):
    b, t, _ = p_all.shape
    zw = 2 * GROUP_WIDTH
    kern = functools.partial(_sgu_kernel, tm=tm)
    return pl.pallas_call(
        kern,
        grid=(b, t // tm),
        in_specs=[pl.BlockSpec((1, tm, zw), lambda bi, ti: (bi, ti, COL_Z // zw)),
                  pl.BlockSpec((1, GROUP_WIDTH), lambda bi, ti: (0, 0)),
                  pl.BlockSpec((1, GROUP_WIDTH), lambda bi, ti: (0, 0)),
                  pl.BlockSpec(w_s.shape, lambda bi, ti: (0, 0, 0)),
                  pl.BlockSpec((SGU_CHUNK, SGU_GROUPS), lambda bi, ti: (0, 0))],
        out_specs=pl.BlockSpec((1, tm, GROUP_WIDTH), lambda bi, ti: (bi, ti, 0)),
        out_shape=jax.ShapeDtypeStruct((b, t, GROUP_WIDTH), BF16),
        compiler_params=_params(("arbitrary", "arbitrary")),
        name="sgu",
    )(p_all, ln_g.reshape(1, GROUP_WIDTH), ln_b.reshape(1, GROUP_WIDTH), w_s, b_s.T)


def _win_band_start(step, t):
    return jnp.clip(step * WIN_QB - WIN_BLOCK, 0, t - WIN_BAND)


def _win_bias(n_ctx, t):
    step = jnp.arange(t // WIN_QB, dtype=jnp.int32)[:, None, None]
    qpos = step * WIN_QB + jnp.arange(WIN_QB, dtype=jnp.int32)[None, :, None]
    kpos = _win_band_start(step, t) + jnp.arange(WIN_BAND, dtype=jnp.int32)[None, None, :]
    valid = (kpos >= n_ctx) & (qpos >= n_ctx) & (jnp.abs(kpos - qpos) <= WINDOW)
    return jnp.where(valid, 0.0, NEG_INF).astype(F32)


def _win_kernel(sink_ref, q_ref, k_ref, v_ref, bias_ref, cos_ref, sin_ref, o_ref, kt_ref, vv_ref, *, n_ctx):
    n = pl.program_id(1)
    t = k_ref.shape[1]
    blk = WIN_QB
    band = WIN_BAND
    lane = lax.broadcasted_iota(jnp.int32, (1, LANES), 1)
    low = lane < HEAD_DIM

    @pl.when(n == 0)
    def _():
        kr = _rope(k_ref[0], cos_ref[...], sin_ref[...])
        for g in range(WIN_KV_HEADS):
            mine = (lane >= g * HEAD_DIM) & (lane < (g + 1) * HEAD_DIM)
            km = jnp.where(mine, kr, 0.0)
            vm = jnp.where(mine, v_ref[0], 0.0)
            kt_ref[g] = (km + pltpu.roll(km, HEAD_DIM, 1)).T.astype(BF16)
            vv_ref[g] = (vm + pltpu.roll(vm, HEAD_DIM, 1)).astype(BF16)

    rows = WIN_REP * blk
    head = lax.broadcasted_iota(jnp.int32, (rows, 1), 0) // blk
    start = pl.multiple_of(_win_band_start(n, t), WIN_BLOCK)
    q_rows = pl.ds(pl.multiple_of(n * blk, blk), blk)
    cos = cos_ref[q_rows, :]
    sin = sin_ref[q_rows, :]
    for g in range(WIN_KV_HEADS):
        pieces = []
        for pair in range(WIN_REP // 2):
            col = (g * (WIN_REP // 2) + pair) * LANES
            qp = _rope(q_ref[0, :, col:col + LANES], cos, sin) * (HEAD_DIM ** -0.5)
            pieces += [jnp.where(low, qp, 0.0), jnp.where(low, 0.0, qp)]
        qs = jnp.concatenate(pieces, axis=0).astype(BF16)
        s_c = jnp.dot(qs, kt_ref[g, :, 0:n_ctx], preferred_element_type=F32)
        s_b = jnp.dot(qs, kt_ref[g, :, pl.ds(start, band)], preferred_element_type=F32)
        s_b = (s_b.reshape(WIN_REP, blk, band) + bias_ref[...]).reshape(rows, band)
        sink = jnp.full((rows, 1), sink_ref[g * WIN_REP], F32)
        for r in range(1, WIN_REP):
            sink = jnp.where(head == r, sink_ref[g * WIN_REP + r], sink)
        chunks = lambda a: [a[:, c * LANES:(c + 1) * LANES] for c in range(a.shape[1] // LANES)]
        m = jnp.maximum(jnp.max(functools.reduce(jnp.maximum, chunks(s_c) + chunks(s_b)), axis=-1, keepdims=True),
                        sink)
        e_c = jnp.exp(s_c - m)
        e_b = jnp.exp(s_b - m)
        den = (jnp.sum(functools.reduce(jnp.add, chunks(e_c) + chunks(e_b)), axis=-1, keepdims=True)
               + jnp.exp(sink - m))
        o = (jnp.dot(e_c.astype(BF16), vv_ref[g, 0:n_ctx, :], preferred_element_type=F32)
             + jnp.dot(e_b.astype(BF16), vv_ref[g, pl.ds(start, band), :], preferred_element_type=F32)) / den
        for pair in range(WIN_REP // 2):
            col = (g * (WIN_REP // 2) + pair) * LANES
            lo = o[(2 * pair) * blk:(2 * pair + 1) * blk]
            hi = o[(2 * pair + 1) * blk:(2 * pair + 2) * blk]
            o_ref[0, :, col:col + LANES] = jnp.where(low, lo, hi).astype(BF16)


def _win(p_all, sink, cos_t, sin_t, n_ctx):
    b, t, _ = p_all.shape
    qw = WIN_REP * HEAD_DIM
    table = pl.BlockSpec((t, LANES), lambda bi, ni: (0, 0))
    assert t % WIN_QB == 0 and t >= WIN_BAND and qw * WIN_KV_HEADS == GROUP_WIDTH
    lo = -(-(n_ctx + WIN_BLOCK) // WIN_QB)
    hi = (t - WIN_BAND + WIN_BLOCK) // WIN_QB

    def bias_block(bi, ni):
        return (jnp.where((ni >= lo) & (ni <= hi), lo, ni), 0, 0)

    kern = functools.partial(_win_kernel, n_ctx=n_ctx)
    return pl.pallas_call(
        kern,
        grid=(b, t // WIN_QB),
        in_specs=[pl.BlockSpec(memory_space=pltpu.SMEM),
                  pl.BlockSpec((1, WIN_QB, GROUP_WIDTH), lambda bi, ni: (bi, ni, COL_WQ // GROUP_WIDTH)),
                  pl.BlockSpec((1, t, LANES), lambda bi, ni: (bi, 0, COL_WK // LANES)),
                  pl.BlockSpec((1, t, LANES), lambda bi, ni: (bi, 0, COL_WV // LANES)),
                  pl.BlockSpec((1, WIN_QB, WIN_BAND), bias_block), table, table],
        out_specs=pl.BlockSpec((1, WIN_QB, GROUP_WIDTH), lambda bi, ni: (bi, ni, 0)),
        out_shape=jax.ShapeDtypeStruct((b, t, GROUP_WIDTH), BF16),
        scratch_shapes=[pltpu.VMEM((WIN_KV_HEADS, LANES, t), BF16), pltpu.VMEM((WIN_KV_HEADS, t, LANES), BF16)],
        compiler_params=_params(("arbitrary", "arbitrary")),
        name="win_attn",
    )(sink, p_all, p_all, p_all, _win_bias(n_ctx, t), cos_t, sin_t)


def _outproj_kernel(a_ref, b_ref, c_ref, d_ref, w_ref, h_ref, m_ref, o_ref, wb_ref, *, tm, n_ctx):
    ti = pl.program_id(2)

    @pl.when((pl.program_id(1) == 0) & (ti == 0))
    def _():
        wb_ref[...] = w_ref[...].astype(BF16)

    y = None
    for i, r in enumerate((a_ref, b_ref, c_ref, d_ref)):
        part = jnp.dot(r[0], wb_ref[i * GROUP_WIDTH:(i + 1) * GROUP_WIDTH, :], preferred_element_type=F32)
        y = part if y is None else y + part
    gate = _mod_row(m_ref, 2, _row_is_ctx(ti, tm, n_ctx))
    o_ref[0] = h_ref[0] + gate * y


def _outproj(mixes, w, layer, h, modsel, n_ctx, tm=768, tn=1024):
    b, t, d = h.shape
    tn = min(tn, d)
    assert d % tn == 0 and t % tm == 0
    kern = functools.partial(_outproj_kernel, tm=tm, n_ctx=n_ctx)
    mix_spec = pl.BlockSpec((1, tm, GROUP_WIDTH), lambda j, bi, ti: (bi, ti, 0))
    return pl.pallas_call(
        kern,
        grid=(d // tn, b, t // tm),
        in_specs=[mix_spec, mix_spec, mix_spec, mix_spec,
                  pl.BlockSpec((None, 4 * GROUP_WIDTH, tn), lambda j, bi, ti: (layer, 0, j)),
                  pl.BlockSpec((1, tm, tn), lambda j, bi, ti: (bi, ti, j)),
                  pl.BlockSpec((None, 2, 6, tn), lambda j, bi, ti: (bi, 0, 0, j))],
        out_specs=pl.BlockSpec((1, tm, tn), lambda j, bi, ti: (bi, ti, j)),
        out_shape=jax.ShapeDtypeStruct((b, t, d), F32),
        scratch_shapes=[pltpu.VMEM((4 * GROUP_WIDTH, tn), BF16)],
        compiler_params=_params(("arbitrary", "arbitrary", "arbitrary")),
        name="outproj",
    )(*mixes, w, h, modsel)


def _ffn_in_kernel(x_ref, m_ref, g_ref, o_ref, *, tm, n_ctx, row0):
    def store(rs, y):
        o_ref[0, rs, :] = y
    _norm_modulate_rows(x_ref, g_ref, m_ref, 3, row0 + pl.program_id(1) * tm, tm, n_ctx, store)


def _route_kernel(x_ref, m_ref, g_ref, r_ref, o_ref, gate_ref, sel_ref, *, tm, n_ctx, row0):
    _ffn_in_kernel(x_ref, m_ref, g_ref, o_ref, tm=tm, n_ctx=n_ctx, row0=row0)
    f = o_ref[0]
    lane = lax.broadcasted_iota(jnp.int32, (1, LANES), 1)
    logits = jnp.full((tm, LANES), -jnp.inf, F32)
    for e in range(N_EXPERTS):
        logits = jnp.where(lane == e, jnp.sum(f * r_ref[e:e + 1, :], axis=-1, keepdims=True), logits)
    m1 = jnp.max(logits, axis=-1, keepdims=True)
    i1 = jnp.min(jnp.where(logits == m1, lane, LANES), axis=-1, keepdims=True)
    rest = jnp.where(lane == i1, -jnp.inf, logits)
    m2 = jnp.max(rest, axis=-1, keepdims=True)
    i2 = jnp.min(jnp.where(rest == m2, lane, LANES), axis=-1, keepdims=True)
    e2 = jnp.exp(m2 - m1)
    w1 = 1.0 / (1.0 + e2)
    w2 = e2 / (1.0 + e2)
    gate_ref[0] = jnp.where(lane == i1, w1, 0.0) + jnp.where(lane == i2, w2, 0.0)
    sel_ref[0] = ((lane == i1) | (lane == i2)).astype(jnp.int32)


def _ffn_in_route(h, modsel, g, n_ctx, row0, router, tm=256):
    b, t, d = h.shape
    nt = (t - row0) // tm
    off = row0 // tm
    in_specs = [pl.BlockSpec((1, tm, d), lambda bi, ti: (bi, ti + off, 0)),
                pl.BlockSpec((None, 2, 6, d), lambda bi, ti: (bi, 0, 0, 0)),
                pl.BlockSpec((1, d), lambda bi, ti: (0, 0))]
    f_spec = pl.BlockSpec((1, tm, d), lambda bi, ti: (bi, ti, 0))
    f_shape = jax.ShapeDtypeStruct((b, t - row0, d), F32)
    assert router.shape == (d, N_EXPERTS)
    e_spec = pl.BlockSpec((1, tm, LANES), lambda bi, ti: (bi, ti, 0))
    kern = functools.partial(_route_kernel, tm=tm, n_ctx=n_ctx, row0=row0)
    return pl.pallas_call(
        kern, grid=(b, nt),
        in_specs=in_specs + [pl.BlockSpec((N_EXPERTS, d), lambda bi, ti: (0, 0))],
        out_specs=[f_spec, e_spec, e_spec],
        out_shape=[f_shape, jax.ShapeDtypeStruct((b, t - row0, LANES), F32),
                   jax.ShapeDtypeStruct((b, t - row0, LANES), jnp.int32)],
        compiler_params=_params(("arbitrary", "arbitrary")), name="ffn_in_route",
    )(h, modsel, g.reshape(1, d), router.T)


FFN_TM = 1152
FFN_SUB = 192
FFN_VMEM_LIMIT = 60 * 1024 * 1024


def _swiglu_slice(xs, w1, w3, w2):
    h1 = jnp.dot(xs, w1.astype(BF16), preferred_element_type=F32)
    h3 = jnp.dot(xs, w3.astype(BF16), preferred_element_type=F32)
    act = (h1 * _sigmoid(h1) * h3).astype(BF16)
    return jnp.dot(act, w2.astype(BF16), preferred_element_type=F32)


def _ffn_kernel(te_ref, tr_ref, x_ref, w1_ref, w3_ref, w2_ref, o_ref, xs_ref):
    i = pl.program_id(0)
    j = pl.program_id(1)
    n_blocks = (tr_ref[i] + FFN_SUB - 1) // FFN_SUB

    @pl.when(j == 0)
    def _():
        o_ref[...] = jnp.zeros_like(o_ref)

    @pl.when((j == 0) & (n_blocks > 0))
    def _():
        xs_ref[...] = x_ref[...].astype(BF16)

    for nb in range(1, FFN_TM // FFN_SUB + 1):
        @pl.when(n_blocks == nb)
        def _(nb=nb):
            m = nb * FFN_SUB
            o_ref[0:m, :] += _swiglu_slice(xs_ref[0:m, :], w1_ref[0], w3_ref[0], w2_ref[0])


def _ffn(x_rows, w1, w3, w2, tile_expert, tile_rows, tf=256):
    nr, d = x_rows.shape
    n_exp, _, f = w1.shape
    nj = f // tf
    n_tiles = nr // FFN_TM

    def xrow(i, j, te, tr):
        return (jnp.where(tr[i] > 0, i, te[n_tiles]), 0)

    def wcol(i, j, te, tr):
        return (te[i], 0, jnp.where(tr[i] > 0, j, nj - 1))

    def wrow(i, j, te, tr):
        return (te[i], jnp.where(tr[i] > 0, j, nj - 1), 0)

    return pl.pallas_call(
        _ffn_kernel,
        grid_spec=pltpu.PrefetchScalarGridSpec(
            num_scalar_prefetch=2,
            grid=(n_tiles, nj),
            in_specs=[pl.BlockSpec((FFN_TM, d), xrow),
                      pl.BlockSpec((1, d, tf), wcol),
                      pl.BlockSpec((1, d, tf), wcol),
                      pl.BlockSpec((1, tf, d), wrow)],
            out_specs=pl.BlockSpec((FFN_TM, d), lambda i, j, te, tr: (i, 0)),
            scratch_shapes=[pltpu.VMEM((FFN_TM, d), BF16)]),
        out_shape=jax.ShapeDtypeStruct((nr, d), F32),
        compiler_params=_params(("arbitrary", "arbitrary"), FFN_VMEM_LIMIT),
        name="ffn",
    )(tile_expert, tile_rows, x_rows, w1, w3, w2)


def _ffn_dense_kernel(h_ref, m_ref, g_ref, w1_ref, w3_ref, w2_ref, o_ref, xs_ref, *, tm, n_ctx):
    ti = pl.program_id(1)
    j = pl.program_id(2)

    @pl.when(j == 0)
    def _():
        def store(rs, y):
            xs_ref[rs, :] = y.astype(BF16)
        _norm_modulate_rows(h_ref, g_ref, m_ref, 3, ti * tm, tm, n_ctx, store)
        o_ref[...] = jnp.zeros_like(o_ref)

    o_ref[0] += _swiglu_slice(xs_ref[...], w1_ref[...], w3_ref[...], w2_ref[...])

    @pl.when(j == pl.num_programs(2) - 1)
    def _():
        gate = _mod_row(m_ref, 5, _row_is_ctx(ti, tm, n_ctx))
        o_ref[0] = h_ref[0] + gate * o_ref[0]


def _ffn_dense(h, modsel, g, w1, w3, w2, layer, n_ctx, tf=256):
    b, t, d = h.shape
    f = w1.shape[2]
    tm = t // 2
    assert t % tm == 0 and f % tf == 0 and tm % NORM_ROWS == 0 and n_ctx % NORM_ROWS == 0
    kern = functools.partial(_ffn_dense_kernel, tm=tm, n_ctx=n_ctx)
    return pl.pallas_call(
        kern, grid=(b, t // tm, f // tf),
        in_specs=[pl.BlockSpec((1, tm, d), lambda bi, ti, j: (bi, ti, 0)),
                  pl.BlockSpec((None, 2, 6, d), lambda bi, ti, j: (bi, 0, 0, 0)),
                  pl.BlockSpec((1, d), lambda bi, ti, j: (0, 0)),
                  pl.BlockSpec((None, d, tf), lambda bi, ti, j: (layer, 0, j)),
                  pl.BlockSpec((None, d, tf), lambda bi, ti, j: (layer, 0, j)),
                  pl.BlockSpec((None, tf, d), lambda bi, ti, j: (layer, j, 0))],
        out_specs=pl.BlockSpec((1, tm, d), lambda bi, ti, j: (bi, ti, 0), pipeline_mode=pl.Buffered(1)),
        out_shape=jax.ShapeDtypeStruct((b, t, d), F32),
        scratch_shapes=[pltpu.VMEM((tm, d), BF16)],
        compiler_params=_params(("arbitrary", "arbitrary", "arbitrary")), name="ffn_dense",
    )(h, modsel, g.reshape(1, d), w1, w3, w2)


def _moe_final_kernel(h_ref, y0_ref, y1_ref, w0_ref, w1_ref, m_ref, g_ref, o_ref):
    y = w0_ref[0] * y0_ref[0] + w1_ref[0] * y1_ref[0]
    hn = h_ref[0] + m_ref[1, 5:6, :] * y
    ms = jnp.mean(hn * hn, axis=-1, keepdims=True)
    o_ref[0] = hn * lax.rsqrt(ms + EPS) * g_ref[...]


def _moe_final(h, y01, w0, w1, modsel, g, row0, tm=256):
    b, t, d = h.shape
    n_lat = t - row0
    off = row0 // tm
    spec = pl.BlockSpec((1, tm, d), lambda bi, ti: (bi, ti, 0))
    wspec = pl.BlockSpec((1, tm, 1), lambda bi, ti: (bi, ti, 0))
    yspec = lambda slot: pl.BlockSpec((None, 1, tm, d), lambda bi, ti: (slot, bi, ti, 0))
    return pl.pallas_call(
        _moe_final_kernel, grid=(b, n_lat // tm),
        in_specs=[pl.BlockSpec((1, tm, d), lambda bi, ti: (bi, ti + off, 0)), yspec(0), yspec(1), wspec, wspec,
                  pl.BlockSpec((None, 2, 6, d), lambda bi, ti: (bi, 0, 0, 0)),
                  pl.BlockSpec((1, d), lambda bi, ti: (0, 0))],
        out_specs=spec, out_shape=jax.ShapeDtypeStruct((b, n_lat, d), F32),
        compiler_params=_params(("arbitrary", "arbitrary")), name="moe_final",
    )(h, y01, y01, w0, w1, modsel, g.reshape(1, d))


def _routing_tables(sel, gates, n_tiles):
    n, n_exp = sel.shape
    seli = sel.astype(jnp.int32)
    counts = jnp.sum(seli, axis=0)
    rank = jnp.cumsum(seli, axis=0) - seli
    tiles_e = (counts + FFN_TM - 1) // FFN_TM
    tile_end = jnp.cumsum(tiles_e)
    tile_start = tile_end - tiles_e
    pos = tile_start[None, :] * FFN_TM + rank
    n_rows = n_tiles * FFN_TM
    expert = jnp.arange(n_exp, dtype=jnp.int32)[None, :]
    e_lo = jnp.min(jnp.where(sel, expert, n_exp), axis=1, keepdims=True)
    e_hi = jnp.max(jnp.where(sel, expert, -1), axis=1, keepdims=True)
    pick = lambda a, e: jnp.sum(jnp.where(expert == e, a, 0), axis=1)
    pos0, pos1 = pick(pos, e_lo), pick(pos, e_hi)
    w0, w1 = pick(gates, e_lo), pick(gates, e_hi)
    tok = jnp.arange(n, dtype=jnp.int32)
    src = (jnp.arange(n_rows, dtype=jnp.int32) % n).at[jnp.concatenate([pos0, pos1])].set(
        jnp.concatenate([tok, tok]), mode="promise_in_bounds", unique_indices=True)

    tile = jnp.arange(n_tiles, dtype=jnp.int32)
    te = jnp.minimum(jnp.sum((tile_end[None, :] <= tile[:, None]).astype(jnp.int32), axis=1), n_exp - 1)
    tr = jnp.clip(counts[te] - (tile - tile_start[te]) * FFN_TM, 0, FFN_TM)
    tr = jnp.where(tile < tile_end[-1], tr, 0)
    te = jnp.concatenate([te, tile_end[-1:] - 1])
    return src, pos0, pos1, w0, w1, te.astype(jnp.int32), tr.astype(jnp.int32)


def kernel(x, c, ctx, c_ctx, w_mod, b_mod, norm_mix, norm_ffn, w_in, w_out, pool_w, pool_scale, diff_lam,
           diff_subln, sgu_ln_g, sgu_ln_b, sgu_w, sgu_b, win_sink, ffn_w1, ffn_w3, ffn_w2, moe_router,
           moe_w1, moe_w3, moe_w2, norm_final):
    b, n_lat, d = x.shape
    n_ctx = ctx.shape[1]
    t = n_ctx + n_lat
    depth = w_mod.shape[0]
    segments = ((0, n_ctx), (n_ctx, n_lat))

    c_rows = jnp.zeros((8, d), F32).at[:b].set(c).at[b].set(c_ctx)
    mod = _adaln(c_rows, w_mod, b_mod).reshape(depth, 8, 6, d)
    cos_t, sin_t = _rope_tables(n_ctx, n_lat)
    h = jnp.concatenate([ctx, x], axis=1)

    out = None
    for l in range(depth):
        last = l == depth - 1
        lat_m = mod[l, :b]
        ctx_m = jnp.broadcast_to(mod[l, b:b + 1], lat_m.shape)
        modsel = jnp.stack([ctx_m, lat_m], axis=1)

        p_all = _inproj(h, modsel, norm_mix[l], w_in, l, n_ctx)

        lam_init = 0.8 - 0.6 * math.exp(-0.3 * l)
        lq1, lk1, lq2, lk2 = diff_lam[l].astype(F32)
        lam = jnp.exp(jnp.sum(lq1 * lk1)) - jnp.exp(jnp.sum(lq2 * lk2)) + lam_init

        mixes = (_pool(p_all, pool_w[l], pool_scale[l], segments),
                 _diff(p_all, lam, diff_subln[l], cos_t, sin_t, n_ctx, lam_init),
                 _sgu(p_all, sgu_ln_g[l], sgu_ln_b[l], sgu_w[l], sgu_b[l]),
                 _win(p_all, win_sink[l], cos_t, sin_t, n_ctx))
        h = _outproj(mixes, w_out, l, h, modsel, n_ctx)

        i = l // 2
        row0 = n_ctx if last else 0
        rows = b * (t - row0)
        if l % 2 == 0:
            assert not last
            h = _ffn_dense(h, modsel, norm_ffn[l], ffn_w1, ffn_w3, ffn_w2, i, n_ctx)
        else:
            assert last
            f, gates, sel = _ffn_in_route(h, modsel, norm_ffn[l], n_ctx, row0, moe_router[i])
            f = f.reshape(rows, d)
            n_exp = moe_w1.shape[1]
            n_tiles = (2 * rows) // FFN_TM + n_exp
            src, pos0, pos1, w0, w1, te, tr = _routing_tables(
                sel.reshape(rows, LANES)[:, :n_exp] > 0, gates.reshape(rows, LANES)[:, :n_exp], n_tiles)
            rows_of = lambda a, idx: a.at[idx].get(mode="promise_in_bounds")
            ys = _ffn(rows_of(f, src), moe_w1[i], moe_w3[i], moe_w2[i], te, tr)
            y01 = rows_of(ys, jnp.concatenate([pos0, pos1])).reshape(2, b, t - row0, d)
            out = _moe_final(h, y01, w0.reshape(b, t - row0, 1), w1.reshape(b, t - row0, 1),
                             modsel, norm_final, row0)
    return out
```

```python
import functools
import math

import jax
import jax.numpy as jnp
from jax import lax
from jax.experimental import pallas as pl
from jax.experimental.pallas import tpu as pltpu

F32 = jnp.float32
BF16 = jnp.bfloat16

GRID_W = 64
GROUP_WIDTH = 512
HEAD_DIM = 64
ROPE_BASE = 10000.0
EPS = 1e-6
NEG_INF = -1e30
LANES = 128

POOL_WINDOWS = (2, 4, 8, 16)
POOL_PAD = 16
DIFF_HEADS = GROUP_WIDTH // (2 * HEAD_DIM)
DIFF_STEP_HEADS = 4
SGU_CHUNK = 128
SGU_GROUPS = 4
WIN_HEADS = GROUP_WIDTH // HEAD_DIM
WIN_KV_HEADS = 2
WIN_REP = WIN_HEADS // WIN_KV_HEADS
WINDOW = 128
WIN_BLOCK = 128
WIN_QB = 2 * WIN_BLOCK
WIN_BAND = WIN_QB + 2 * WIN_BLOCK
N_EXPERTS = 8

COL_POOL = 0
COL_DQ = GROUP_WIDTH
COL_DK = 2 * GROUP_WIDTH
COL_DV = 3 * GROUP_WIDTH
COL_Z = 4 * GROUP_WIDTH
COL_WQ = 6 * GROUP_WIDTH
COL_WK = COL_WQ + WIN_HEADS * HEAD_DIM
COL_WV = COL_WK + WIN_KV_HEADS * HEAD_DIM
IN_COLS = COL_WV + WIN_KV_HEADS * HEAD_DIM

VMEM_LIMIT = 56 * 1024 * 1024
NORM_ROWS = 32


def _params(sem, vmem=VMEM_LIMIT):
    return pltpu.CompilerParams(dimension_semantics=sem, vmem_limit_bytes=vmem)


def _sigmoid(x):
    return 1.0 / (1.0 + jnp.exp(-x))


def _row_is_ctx(tile_idx, tm, n_ctx):
    rows = tile_idx * tm + lax.broadcasted_iota(jnp.int32, (tm, 1), 0)
    return rows < n_ctx


def _mod_row(m_ref, k, is_ctx):
    return jnp.where(is_ctx, m_ref[0, k:k + 1, :], m_ref[1, k:k + 1, :])


def _norm_modulate_rows(x_ref, g_ref, m_ref, k_shift, first_row, tm, n_ctx, store):
    def body(r, carry):
        r0 = pl.multiple_of(r * NORM_ROWS, NORM_ROWS)
        which = ((first_row + r0) >= n_ctx).astype(jnp.int32)
        rs = pl.ds(r0, NORM_ROWS)
        x = x_ref[0, rs, :]
        ms = jnp.mean(x * x, axis=-1, keepdims=True)
        a = g_ref[...] * (1.0 + m_ref[which, k_shift + 1:k_shift + 2, :])
        store(rs, x * lax.rsqrt(ms + EPS) * a + m_ref[which, k_shift:k_shift + 1, :])
        return carry
    lax.fori_loop(0, tm // NORM_ROWS, body, 0, unroll=4)


def _adaln_kernel(c_ref, w_ref, b_ref, o_ref):
    cv = c_ref[...]
    s = (cv * _sigmoid(cv)).astype(BF16)
    o_ref[0] = jnp.dot(s, w_ref[0].astype(BF16), preferred_element_type=F32) + b_ref[0]


def _adaln(c_rows, w_mod, b_mod, tn=1024):
    depth, d, n = w_mod.shape
    r = c_rows.shape[0]
    return pl.pallas_call(
        _adaln_kernel,
        grid=(depth, n // tn),
        in_specs=[pl.BlockSpec((r, d), lambda l, j: (0, 0)),
                  pl.BlockSpec((1, d, tn), lambda l, j: (l, 0, j)),
                  pl.BlockSpec((1, 1, tn), lambda l, j: (l, 0, j))],
        out_specs=pl.BlockSpec((1, r, tn), lambda l, j: (l, 0, j)),
        out_shape=jax.ShapeDtypeStruct((depth, r, n), F32),
        compiler_params=_params(("arbitrary", "arbitrary")),
        name="adaln",
    )(c_rows, w_mod, b_mod.reshape(depth, 1, n))


def _inproj_kernel(x_ref, m_ref, g_ref, w_ref, o_ref, xn_ref, *, tm, n_ctx):
    @pl.when(pl.program_id(2) == 0)
    def _():
        def store(rs, y):
            xn_ref[rs, :] = y.astype(BF16)
        _norm_modulate_rows(x_ref, g_ref, m_ref, 0, pl.program_id(1) * tm, tm, n_ctx, store)

    o_ref[0] = jnp.dot(xn_ref[...], w_ref[...].astype(BF16), preferred_element_type=F32)


def _rope(a, cos, sin):
    lane = lax.broadcasted_iota(jnp.int32, (1, LANES), 1)
    first_half = (lane % HEAD_DIM) < (HEAD_DIM // 2)
    partner = jnp.where(first_half, pltpu.roll(a, LANES - HEAD_DIM // 2, 1), pltpu.roll(a, HEAD_DIM // 2, 1))
    return a * cos + partner * sin


def _inproj(h, modsel, g, w, layer, n_ctx, tn=768):
    b, t, d = h.shape
    tm = t // 2
    assert tm % NORM_ROWS == 0 and n_ctx % NORM_ROWS == 0 and IN_COLS % tn == 0
    kern = functools.partial(_inproj_kernel, tm=tm, n_ctx=n_ctx)
    return pl.pallas_call(
        kern,
        grid=(b, t // tm, IN_COLS // tn),
        in_specs=[pl.BlockSpec((1, tm, d), lambda bi, ti, j: (bi, ti, 0)),
                  pl.BlockSpec((None, 2, 6, d), lambda bi, ti, j: (bi, 0, 0, 0)),
                  pl.BlockSpec((1, d), lambda bi, ti, j: (0, 0)),
                  pl.BlockSpec((None, d, tn), lambda bi, ti, j: (layer, 0, j))],
        out_specs=pl.BlockSpec((1, tm, tn), lambda bi, ti, j: (bi, ti, j)),
        out_shape=jax.ShapeDtypeStruct((b, t, IN_COLS), F32),
        scratch_shapes=[pltpu.VMEM((tm, d), BF16)],
        compiler_params=_params(("arbitrary", "arbitrary", "arbitrary")),
        name="inproj",
    )(h, modsel, g.reshape(1, d), w)


def _rope_tables(n_ctx, n_lat):
    n_rows = n_lat // GRID_W
    rows = jnp.repeat(jnp.arange(n_rows, dtype=F32), GRID_W)
    cols = jnp.tile(jnp.arange(GRID_W, dtype=F32), n_rows)
    n_freq = HEAD_DIM // 4
    inv_freq = ROPE_BASE ** (-jnp.arange(n_freq, dtype=F32) / n_freq)
    ang = jnp.concatenate([rows[:, None] * inv_freq, cols[:, None] * inv_freq], axis=-1)
    cos, sin = jnp.cos(ang), jnp.sin(ang)
    cos_h = jnp.concatenate([cos, cos], axis=-1)
    sin_h = jnp.concatenate([-sin, sin], axis=-1)
    cos_t = jnp.concatenate([jnp.ones((n_ctx, HEAD_DIM), F32), cos_h], axis=0)
    sin_t = jnp.concatenate([jnp.zeros((n_ctx, HEAD_DIM), F32), sin_h], axis=0)
    rep = LANES // HEAD_DIM
    return jnp.tile(cos_t, (1, rep)), jnp.tile(sin_t, (1, rep))


def _pool_kernel(a_ref, w_ref, s_ref, o_ref, pad_ref, *, segments):
    for g, win in enumerate(POOL_WINDOWS):
        half = win // 2
        cs = slice(g * LANES, (g + 1) * LANES)
        wg = w_ref[g].astype(BF16)
        for s0, n in segments:
            a = a_ref[0, s0:s0 + n, cs]
            pad_ref[0:POOL_PAD, :] = jnp.zeros((POOL_PAD, LANES), F32)
            pad_ref[POOL_PAD:POOL_PAD + n, :] = a
            pad_ref[POOL_PAD + n:2 * POOL_PAD + n, :] = jnp.zeros((POOL_PAD, LANES), F32)
            tot = pad_ref[POOL_PAD - half:POOL_PAD - half + n, :]
            for k in range(-half + 1, half):
                tot = tot + pad_ref[POOL_PAD + k:POOL_PAD + k + n, :]
            t = lax.broadcasted_iota(jnp.int32, (n, 1), 0)
            cnt = (jnp.minimum(t + half, n) - jnp.maximum(t - half, 0)).astype(F32)
            dlt = (tot / cnt - a).astype(BF16)
            y = jnp.dot(dlt, wg, preferred_element_type=F32)
            o_ref[0, s0:s0 + n, cs] = (y * s_ref[:, cs]).astype(BF16)


def _pool(p_all, w, scale, segments):
    b, t, _ = p_all.shape
    max_n = max(n for _, n in segments)
    kern = functools.partial(_pool_kernel, segments=segments)
    return pl.pallas_call(
        kern,
        grid=(b,),
        in_specs=[pl.BlockSpec((1, t, GROUP_WIDTH), lambda bi: (bi, 0, COL_POOL // GROUP_WIDTH)),
                  pl.BlockSpec(w.shape, lambda bi: (0, 0, 0)),
                  pl.BlockSpec((1, GROUP_WIDTH), lambda bi: (0, 0))],
        out_specs=pl.BlockSpec((1, t, GROUP_WIDTH), lambda bi: (bi, 0, 0)),
        out_shape=jax.ShapeDtypeStruct((b, t, GROUP_WIDTH), BF16),
        scratch_shapes=[pltpu.VMEM((max_n + 2 * POOL_PAD, LANES), F32)],
        compiler_params=_params(("arbitrary",)),
        name="pool",
    )(p_all, w, scale.reshape(1, GROUP_WIDTH))


def _diff_kernel(lam_ref, q_ref, k_ref, v_ref, g_ref, cos_ref, sin_ref, o_ref, kt_ref, vb_ref,
                 *, tq, n_ctx, out_scale):
    ti = pl.program_id(2)
    lam = lam_ref[0]
    w = 2 * HEAD_DIM

    @pl.when(ti == 0)
    def _():
        for hh in range(DIFF_STEP_HEADS):
            cs = slice(hh * w, (hh + 1) * w)
            kt_ref[cs, :] = _rope(k_ref[0, :, cs], cos_ref[...], sin_ref[...]).T.astype(BF16)
        vb_ref[...] = v_ref[0].astype(BF16)

    def attend(n_keys):
        lane = lax.broadcasted_iota(jnp.int32, (1, w), 1)
        rows = pl.ds(pl.multiple_of(ti * tq, tq), tq)
        cos = cos_ref[rows, :]
        sin = sin_ref[rows, :]
        for hh in range(DIFF_STEP_HEADS):
            cs = slice(hh * w, (hh + 1) * w)
            q = _rope(q_ref[0, :, cs], cos, sin) * (HEAD_DIM ** -0.5)
            q1 = jnp.where(lane < HEAD_DIM, q, 0.0).astype(BF16)
            q2 = jnp.where(lane >= HEAD_DIM, q, 0.0).astype(BF16)
            kt = kt_ref[cs, 0:n_keys]
            v = vb_ref[0:n_keys, cs]
            s1 = jnp.dot(q1, kt, preferred_element_type=F32)
            s2 = jnp.dot(q2, kt, preferred_element_type=F32)
            e1 = jnp.exp(s1 - jnp.max(s1, axis=-1, keepdims=True))
            e2 = jnp.exp(s2 - jnp.max(s2, axis=-1, keepdims=True))
            r1 = 1.0 / jnp.sum(e1, axis=-1, keepdims=True)
            r2 = lam / jnp.sum(e2, axis=-1, keepdims=True)
            o = (jnp.dot(e1.astype(BF16), v, preferred_element_type=F32) * r1
                 - jnp.dot(e2.astype(BF16), v, preferred_element_type=F32) * r2)
            ms = jnp.mean(o * o, axis=-1, keepdims=True)
            o_ref[0, :, cs] = (o * lax.rsqrt(ms + EPS) * g_ref[...] * out_scale).astype(BF16)

    @pl.when(ti * tq < n_ctx)
    def _():
        attend(n_ctx)

    @pl.when(ti * tq >= n_ctx)
    def _():
        attend(k_ref.shape[1])


def _diff(p_all, lam, subln, cos_t, sin_t, n_ctx, lam_init, tq=256):
    b, t, _ = p_all.shape
    assert n_ctx % tq == 0 and t % tq == 0 and DIFF_HEADS % DIFF_STEP_HEADS == 0
    w = 2 * HEAD_DIM
    sw = DIFF_STEP_HEADS * w
    table = pl.BlockSpec((t, LANES), lambda bi, hi, ti: (0, 0))
    kern = functools.partial(_diff_kernel, tq=tq, n_ctx=n_ctx, out_scale=1.0 - lam_init)
    return pl.pallas_call(
        kern,
        grid=(b, DIFF_HEADS // DIFF_STEP_HEADS, t // tq),
        in_specs=[pl.BlockSpec(memory_space=pltpu.SMEM),
                  pl.BlockSpec((1, tq, sw), lambda bi, hi, ti: (bi, ti, COL_DQ // sw + hi)),
                  pl.BlockSpec((1, t, sw), lambda bi, hi, ti: (bi, 0, COL_DK // sw + hi)),
                  pl.BlockSpec((1, t, sw), lambda bi, hi, ti: (bi, 0, COL_DV // sw + hi)),
                  pl.BlockSpec((1, w), lambda bi, hi, ti: (0, 0)), table, table],
        out_specs=pl.BlockSpec((1, tq, sw), lambda bi, hi, ti: (bi, ti, hi)),
        out_shape=jax.ShapeDtypeStruct((b, t, GROUP_WIDTH), BF16),
        scratch_shapes=[pltpu.VMEM((sw, t), BF16), pltpu.VMEM((t, sw), BF16)],
        compiler_params=_params(("arbitrary", "arbitrary", "arbitrary")),
        name="diff_attn",
    )(lam.reshape(1), p_all, p_all, p_all, subln.reshape(1, w), cos_t, sin_t)


def _gelu_tanh(x):
    return 0.5 * x * (1.0 + jnp.tanh(math.sqrt(2.0 / math.pi) * (x + 0.044715 * (x * x * x))))


def _sgu_kernel(z_ref, g_ref, b_ref, w_ref, bs_ref, o_ref, *, tm):
    z = _gelu_tanh(z_ref[0])
    u = z[:, :GROUP_WIDTH]
    v = z[:, GROUP_WIDTH:]
    mu = jnp.mean(v, axis=-1, keepdims=True)
    var = jnp.mean(jnp.square(v - mu), axis=-1, keepdims=True)
    vn = ((v - mu) * lax.rsqrt(var + EPS) * g_ref[...] + b_ref[...]).astype(BF16)
    for g in range(SGU_GROUPS):
        wg = w_ref[g].astype(BF16)
        bias = bs_ref[:, g:g + 1]
        cs = slice(g * LANES, (g + 1) * LANES)
        for c in range(tm // SGU_CHUNK):
            rs = slice(c * SGU_CHUNK, (c + 1) * SGU_CHUNK)
            sv = jnp.dot(wg, vn[rs, cs], preferred_element_type=F32) + bias
            o_ref[0, rs, cs] = (u[rs, cs] * sv).astype(BF16)


def _sgu(p_all, ln_g, ln_b, w_s, b_s, tm=768):
    b, t, _ = p_all.shape
    zw = 2 * GROUP_WIDTH
    kern = functools.partial(_sgu_kernel, tm=tm)
    return pl.pallas_call(
        kern,
        grid=(b, t // tm),
        in_specs=[pl.BlockSpec((1, tm, zw), lambda bi, ti: (bi, ti, COL_Z // zw)),
                  pl.BlockSpec((1, GROUP_WIDTH), lambda bi, ti: (0, 0)),
                  pl.BlockSpec((1, GROUP_WIDTH), lambda bi, ti: (0, 0)),
                  pl.BlockSpec(w_s.shape, lambda bi, ti: (0, 0, 0)),
                  pl.BlockSpec((SGU_CHUNK, SGU_GROUPS), lambda bi, ti: (0, 0))],
        out_specs=pl.BlockSpec((1, tm, GROUP_WIDTH), lambda bi, ti: (bi, ti, 0)),
        out_shape=jax.ShapeDtypeStruct((b, t, GROUP_WIDTH), BF16),
        compiler_params=_params(("arbitrary", "arbitrary")),
        name="sgu",
    )(p_all, ln_g.reshape(1, GROUP_WIDTH), ln_b.reshape(1, GROUP_WIDTH), w_s, b_s.T)


def _win_band_start(step, t):
    return jnp.clip(step * WIN_QB - WIN_BLOCK, 0, t - WIN_BAND)


def _win_bias(n_ctx, t):
    step = jnp.arange(t // WIN_QB, dtype=jnp.int32)[:, None, None]
    qpos = step * WIN_QB + jnp.arange(WIN_QB, dtype=jnp.int32)[None, :, None]
    kpos = _win_band_start(step, t) + jnp.arange(WIN_BAND, dtype=jnp.int32)[None, None, :]
    valid = (kpos >= n_ctx) & (qpos >= n_ctx) & (jnp.abs(kpos - qpos) <= WINDOW)
    return jnp.where(valid, 0.0, NEG_INF).astype(F32)


def _win_kernel(sink_ref, q_ref, k_ref, v_ref, bias_ref, cos_ref, sin_ref, o_ref, kt_ref, vv_ref, *, n_ctx):
    n = pl.program_id(1)
    t = k_ref.shape[1]
    blk = WIN_QB
    band = WIN_BAND
    lane = lax.broadcasted_iota(jnp.int32, (1, LANES), 1)
    low = lane < HEAD_DIM

    @pl.when(n == 0)
    def _():
        kr = _rope(k_ref[0], cos_ref[...], sin_ref[...])
        for g in range(WIN_KV_HEADS):
            mine = (lane >= g * HEAD_DIM) & (lane < (g + 1) * HEAD_DIM)
            km = jnp.where(mine, kr, 0.0)
            vm = jnp.where(mine, v_ref[0], 0.0)
            kt_ref[g] = (km + pltpu.roll(km, HEAD_DIM, 1)).T.astype(BF16)
            vv_ref[g] = (vm + pltpu.roll(vm, HEAD_DIM, 1)).astype(BF16)

    rows = WIN_REP * blk
    head = lax.broadcasted_iota(jnp.int32, (rows, 1), 0) // blk
    start = pl.multiple_of(_win_band_start(n, t), WIN_BLOCK)
    q_rows = pl.ds(pl.multiple_of(n * blk, blk), blk)
    cos = cos_ref[q_rows, :]
    sin = sin_ref[q_rows, :]
    for g in range(WIN_KV_HEADS):
        pieces = []
        for pair in range(WIN_REP // 2):
            col = (g * (WIN_REP // 2) + pair) * LANES
            qp = _rope(q_ref[0, :, col:col + LANES], cos, sin) * (HEAD_DIM ** -0.5)
            pieces += [jnp.where(low, qp, 0.0), jnp.where(low, 0.0, qp)]
        qs = jnp.concatenate(pieces, axis=0).astype(BF16)
        s_c = jnp.dot(qs, kt_ref[g, :, 0:n_ctx], preferred_element_type=F32)
        s_b = jnp.dot(qs, kt_ref[g, :, pl.ds(start, band)], preferred_element_type=F32)
        s_b = (s_b.reshape(WIN_REP, blk, band) + bias_ref[...]).reshape(rows, band)
        sink = jnp.full((rows, 1), sink_ref[g * WIN_REP], F32)
        for r in range(1, WIN_REP):
            sink = jnp.where(head == r, sink_ref[g * WIN_REP + r], sink)
        chunks = lambda a: [a[:, c * LANES:(c + 1) * LANES] for c in range(a.shape[1] // LANES)]
        m = jnp.maximum(jnp.max(functools.reduce(jnp.maximum, chunks(s_c) + chunks(s_b)), axis=-1, keepdims=True),
                        sink)
        e_c = jnp.exp(s_c - m)
        e_b = jnp.exp(s_b - m)
        den = (jnp.sum(functools.reduce(jnp.add, chunks(e_c) + chunks(e_b)), axis=-1, keepdims=True)
               + jnp.exp(sink - m))
        o = (jnp.dot(e_c.astype(BF16), vv_ref[g, 0:n_ctx, :], preferred_element_type=F32)
             + jnp.dot(e_b.astype(BF16), vv_ref[g, pl.ds(start, band), :], preferred_element_type=F32)) / den
        for pair in range(WIN_REP // 2):
            col = (g * (WIN_REP // 2) + pair) * LANES
            lo = o[(2 * pair) * blk:(2 * pair + 1) * blk]
            hi = o[(2 * pair + 1) * blk:(2 * pair + 2) * blk]
            o_ref[0, :, col:col + LANES] = jnp.where(low, lo, hi).astype(BF16)


def _win(p_all, sink, cos_t, sin_t, n_ctx):
    b, t, _ = p_all.shape
    qw = WIN_REP * HEAD_DIM
    table = pl.BlockSpec((t, LANES), lambda bi, ni: (0, 0))
    assert t % WIN_QB == 0 and t >= WIN_BAND and qw * WIN_KV_HEADS == GROUP_WIDTH
    lo = -(-(n_ctx + WIN_BLOCK) // WIN_QB)
    hi = (t - WIN_BAND + WIN_BLOCK) // WIN_QB

    def bias_block(bi, ni):
        return (jnp.where((ni >= lo) & (ni <= hi), lo, ni), 0, 0)

    kern = functools.partial(_win_kernel, n_ctx=n_ctx)
    return pl.pallas_call(
        kern,
        grid=(b, t // WIN_QB),
        in_specs=[pl.BlockSpec(memory_space=pltpu.SMEM),
                  pl.BlockSpec((1, WIN_QB, GROUP_WIDTH), lambda bi, ni: (bi, ni, COL_WQ // GROUP_WIDTH)),
                  pl.BlockSpec((1, t, LANES), lambda bi, ni: (bi, 0, COL_WK // LANES)),
                  pl.BlockSpec((1, t, LANES), lambda bi, ni: (bi, 0, COL_WV // LANES)),
                  pl.BlockSpec((1, WIN_QB, WIN_BAND), bias_block), table, table],
        out_specs=pl.BlockSpec((1, WIN_QB, GROUP_WIDTH), lambda bi, ni: (bi, ni, 0)),
        out_shape=jax.ShapeDtypeStruct((b, t, GROUP_WIDTH), BF16),
        scratch_shapes=[pltpu.VMEM((WIN_KV_HEADS, LANES, t), BF16), pltpu.VMEM((WIN_KV_HEADS, t, LANES), BF16)],
        compiler_params=_params(("arbitrary", "arbitrary")),
        name="win_attn",
    )(sink, p_all, p_all, p_all, _win_bias(n_ctx, t), cos_t, sin_t)


def _outproj_kernel(a_ref, b_ref, c_ref, d_ref, w_ref, h_ref, m_ref, o_ref, wb_ref, *, tm, n_ctx):
    ti = pl.program_id(2)

    @pl.when((pl.program_id(1) == 0) & (ti == 0))
    def _():
        wb_ref[...] = w_ref[...].astype(BF16)

    y = None
    for i, r in enumerate((a_ref, b_ref, c_ref, d_ref)):
        part = jnp.dot(r[0], wb_ref[i * GROUP_WIDTH:(i + 1) * GROUP_WIDTH, :], preferred_element_type=F32)
        y = part if y is None else y + part
    gate = _mod_row(m_ref, 2, _row_is_ctx(ti, tm, n_ctx))
    o_ref[0] = h_ref[0] + gate * y


def _outproj(mixes, w, layer, h, modsel, n_ctx, tm=768, tn=1024):
    b, t, d = h.shape
    tn = min(tn, d)
    assert d % tn == 0 and t % tm == 0
    kern = functools.partial(_outproj_kernel, tm=tm, n_ctx=n_ctx)
    mix_spec = pl.BlockSpec((1, tm, GROUP_WIDTH), lambda j, bi, ti: (bi, ti, 0))
    return pl.pallas_call(
        kern,
        grid=(d // tn, b, t // tm),
        in_specs=[mix_spec, mix_spec, mix_spec, mix_spec,
                  pl.BlockSpec((None, 4 * GROUP_WIDTH, tn), lambda j, bi, ti: (layer, 0, j)),
                  pl.BlockSpec((1, tm, tn), lambda j, bi, ti: (bi, ti, j)),
                  pl.BlockSpec((None, 2, 6, tn), lambda j, bi, ti: (bi, 0, 0, j))],
        out_specs=pl.BlockSpec((1, tm, tn), lambda j, bi, ti: (bi, ti, j)),
        out_shape=jax.ShapeDtypeStruct((b, t, d), F32),
        scratch_shapes=[pltpu.VMEM((4 * GROUP_WIDTH, tn), BF16)],
        compiler_params=_params(("arbitrary", "arbitrary", "arbitrary")),
        name="outproj",
    )(*mixes, w, h, modsel)


def _ffn_in_kernel(x_ref, m_ref, g_ref, o_ref, *, tm, n_ctx, row0):
    def store(rs, y):
        o_ref[0, rs, :] = y
    _norm_modulate_rows(x_ref, g_ref, m_ref, 3, row0 + pl.program_id(1) * tm, tm, n_ctx, store)


def _route_kernel(x_ref, m_ref, g_ref, r_ref, o_ref, gate_ref, sel_ref, *, tm, n_ctx, row0):
    _ffn_in_kernel(x_ref, m_ref, g_ref, o_ref, tm=tm, n_ctx=n_ctx, row0=row0)
    f = o_ref[0]
    lane = lax.broadcasted_iota(jnp.int32, (1, LANES), 1)
    logits = jnp.full((tm, LANES), -jnp.inf, F32)
    for e in range(N_EXPERTS):
        logits = jnp.where(lane == e, jnp.sum(f * r_ref[e:e + 1, :], axis=-1, keepdims=True), logits)
    m1 = jnp.max(logits, axis=-1, keepdims=True)
    i1 = jnp.min(jnp.where(logits == m1, lane, LANES), axis=-1, keepdims=True)
    rest = jnp.where(lane == i1, -jnp.inf, logits)
    m2 = jnp.max(rest, axis=-1, keepdims=True)
    i2 = jnp.min(jnp.where(rest == m2, lane, LANES), axis=-1, keepdims=True)
    e2 = jnp.exp(m2 - m1)
    w1 = 1.0 / (1.0 + e2)
    w2 = e2 / (1.0 + e2)
    gate_ref[0] = jnp.where(lane == i1, w1, 0.0) + jnp.where(lane == i2, w2, 0.0)
    sel_ref[0] = ((lane == i1) | (lane == i2)).astype(jnp.int32)


def _ffn_in_route(h, modsel, g, n_ctx, row0, router, tm=256):
    b, t, d = h.shape
    nt = (t - row0) // tm
    off = row0 // tm
    in_specs = [pl.BlockSpec((1, tm, d), lambda bi, ti: (bi, ti + off, 0)),
                pl.BlockSpec((None, 2, 6, d), lambda bi, ti: (bi, 0, 0, 0)),
                pl.BlockSpec((1, d), lambda bi, ti: (0, 0))]
    f_spec = pl.BlockSpec((1, tm, d), lambda bi, ti: (bi, ti, 0))
    f_shape = jax.ShapeDtypeStruct((b, t - row0, d), F32)
    assert router.shape == (d, N_EXPERTS)
    e_spec = pl.BlockSpec((1, tm, LANES), lambda bi, ti: (bi, ti, 0))
    kern = functools.partial(_route_kernel, tm=tm, n_ctx=n_ctx, row0=row0)
    return pl.pallas_call(
        kern, grid=(b, nt),
        in_specs=in_specs + [pl.BlockSpec((N_EXPERTS, d), lambda bi, ti: (0, 0))],
        out_specs=[f_spec, e_spec, e_spec],
        out_shape=[f_shape, jax.ShapeDtypeStruct((b, t - row0, LANES), F32),
                   jax.ShapeDtypeStruct((b, t - row0, LANES), jnp.int32)],
        compiler_params=_params(("arbitrary", "arbitrary")), name="ffn_in_route",
    )(h, modsel, g.reshape(1, d), router.T)


FFN_TM = 1152
FFN_SUB = 192
FFN_VMEM_LIMIT = 60 * 1024 * 1024


def _swiglu_slice(xs, w1, w3, w2):
    h1 = jnp.dot(xs, w1.astype(BF16), preferred_element_type=F32)
    h3 = jnp.dot(xs, w3.astype(BF16), preferred_element_type=F32)
    act = (h1 * _sigmoid(h1) * h3).astype(BF16)
    return jnp.dot(act, w2.astype(BF16), preferred_element_type=F32)


def _ffn_kernel(te_ref, tr_ref, x_ref, w1_ref, w3_ref, w2_ref, o_ref, xs_ref):
    i = pl.program_id(0)
    j = pl.program_id(1)
    n_blocks = (tr_ref[i] + FFN_SUB - 1) // FFN_SUB

    @pl.when(j == 0)
    def _():
        o_ref[...] = jnp.zeros_like(o_ref)

    @pl.when((j == 0) & (n_blocks > 0))
    def _():
        xs_ref[...] = x_ref[...].astype(BF16)

    for nb in range(1, FFN_TM // FFN_SUB + 1):
        @pl.when(n_blocks == nb)
        def _(nb=nb):
            m = nb * FFN_SUB
            o_ref[0:m, :] += _swiglu_slice(xs_ref[0:m, :], w1_ref[0], w3_ref[0], w2_ref[0])


def _ffn(x_rows, w1, w3, w2, tile_expert, tile_rows, tf=256):
    nr, d = x_rows.shape
    n_exp, _, f = w1.shape
    nj = f // tf
    n_tiles = nr // FFN_TM

    def xrow(i, j, te, tr):
        return (jnp.where(tr[i] > 0, i, te[n_tiles]), 0)

    def wcol(i, j, te, tr):
        return (te[i], 0, jnp.where(tr[i] > 0, j, nj - 1))

    def wrow(i, j, te, tr):
        return (te[i], jnp.where(tr[i] > 0, j, nj - 1), 0)

    return pl.pallas_call(
        _ffn_kernel,
        grid_spec=pltpu.PrefetchScalarGridSpec(
            num_scalar_prefetch=2,
            grid=(n_tiles, nj),
            in_specs=[pl.BlockSpec((FFN_TM, d), xrow),
                      pl.BlockSpec((1, d, tf), wcol),
                      pl.BlockSpec((1, d, tf), wcol),
                      pl.BlockSpec((1, tf, d), wrow)],
            out_specs=pl.BlockSpec((FFN_TM, d), lambda i, j, te, tr: (i, 0)),
            scratch_shapes=[pltpu.VMEM((FFN_TM, d), BF16)]),
        out_shape=jax.ShapeDtypeStruct((nr, d), F32),
        compiler_params=_params(("arbitrary", "arbitrary"), FFN_VMEM_LIMIT),
        name="ffn",
    )(tile_expert, tile_rows, x_rows, w1, w3, w2)


def _ffn_dense_kernel(h_ref, m_ref, g_ref, w1_ref, w3_ref, w2_ref, o_ref, xs_ref, *, tm, n_ctx):
    ti = pl.program_id(1)
    j = pl.program_id(2)

    @pl.when(j == 0)
    def _():
        def store(rs, y):
            xs_ref[rs, :] = y.astype(BF16)
        _norm_modulate_rows(h_ref, g_ref, m_ref, 3, ti * tm, tm, n_ctx, store)
        o_ref[...] = jnp.zeros_like(o_ref)

    o_ref[0] += _swiglu_slice(xs_ref[...], w1_ref[...], w3_ref[...], w2_ref[...])

    @pl.when(j == pl.num_programs(2) - 1)
    def _():
        gate = _mod_row(m_ref, 5, _row_is_ctx(ti, tm, n_ctx))
        o_ref[0] = h_ref[0] + gate * o_ref[0]


def _ffn_dense(h, modsel, g, w1, w3, w2, layer, n_ctx, tf=256):
    b, t, d = h.shape
    f = w1.shape[2]
    tm = t // 2
    assert t % tm == 0 and f % tf == 0 and tm % NORM_ROWS == 0 and n_ctx % NORM_ROWS == 0
    kern = functools.partial(_ffn_dense_kernel, tm=tm, n_ctx=n_ctx)
    return pl.pallas_call(
        kern, grid=(b, t // tm, f // tf),
        in_specs=[pl.BlockSpec((1, tm, d), lambda bi, ti, j: (bi, ti, 0)),
                  pl.BlockSpec((None, 2, 6, d), lambda bi, ti, j: (bi, 0, 0, 0)),
                  pl.BlockSpec((1, d), lambda bi, ti, j: (0, 0)),
                  pl.BlockSpec((None, d, tf), lambda bi, ti, j: (layer, 0, j)),
                  pl.BlockSpec((None, d, tf), lambda bi, ti, j: (layer, 0, j)),
                  pl.BlockSpec((None, tf, d), lambda bi, ti, j: (layer, j, 0))],
        out_specs=pl.BlockSpec((1, tm, d), lambda bi, ti, j: (bi, ti, 0)),
        out_shape=jax.ShapeDtypeStruct((b, t, d), F32),
        scratch_shapes=[pltpu.VMEM((tm, d), BF16)],
        compiler_params=_params(("arbitrary", "arbitrary", "arbitrary"), FFN_VMEM_LIMIT), name="ffn_dense",
    )(h, modsel, g.reshape(1, d), w1, w3, w2)


def _moe_final_kernel(h_ref, y0_ref, y1_ref, w0_ref, w1_ref, m_ref, g_ref, o_ref):
    y = w0_ref[0] * y0_ref[0] + w1_ref[0] * y1_ref[0]
    hn = h_ref[0] + m_ref[1, 5:6, :] * y
    ms = jnp.mean(hn * hn, axis=-1, keepdims=True)
    o_ref[0] = hn * lax.rsqrt(ms + EPS) * g_ref[...]


def _moe_final(h, y01, w0, w1, modsel, g, row0, tm=256):
    b, t, d = h.shape
    n_lat = t - row0
    off = row0 // tm
    spec = pl.BlockSpec((1, tm, d), lambda bi, ti: (bi, ti, 0))
    wspec = pl.BlockSpec((1, tm, 1), lambda bi, ti: (bi, ti, 0))
    yspec = lambda slot: pl.BlockSpec((None, 1, tm, d), lambda bi, ti: (slot, bi, ti, 0))
    return pl.pallas_call(
        _moe_final_kernel, grid=(b, n_lat // tm),
        in_specs=[pl.BlockSpec((1, tm, d), lambda bi, ti: (bi, ti + off, 0)), yspec(0), yspec(1), wspec, wspec,
                  pl.BlockSpec((None, 2, 6, d), lambda bi, ti: (bi, 0, 0, 0)),
                  pl.BlockSpec((1, d), lambda bi, ti: (0, 0))],
        out_specs=spec, out_shape=jax.ShapeDtypeStruct((b, n_lat, d), F32),
        compiler_params=_params(("arbitrary", "arbitrary")), name="moe_final",
    )(h, y01, y01, w0, w1, modsel, g.reshape(1, d))


def _routing_tables(sel, gates, n_tiles):
    n, n_exp = sel.shape
    seli = sel.astype(jnp.int32)
    counts = jnp.sum(seli, axis=0)
    rank = jnp.cumsum(seli, axis=0) - seli
    tiles_e = (counts + FFN_TM - 1) // FFN_TM
    tile_end = jnp.cumsum(tiles_e)
    tile_start = tile_end - tiles_e
    pos = tile_start[None, :] * FFN_TM + rank
    n_rows = n_tiles * FFN_TM
    expert = jnp.arange(n_exp, dtype=jnp.int32)[None, :]
    e_lo = jnp.min(jnp.where(sel, expert, n_exp), axis=1, keepdims=True)
    e_hi = jnp.max(jnp.where(sel, expert, -1), axis=1, keepdims=True)
    pick = lambda a, e: jnp.sum(jnp.where(expert == e, a, 0), axis=1)
    pos0, pos1 = pick(pos, e_lo), pick(pos, e_hi)
    w0, w1 = pick(gates, e_lo), pick(gates, e_hi)
    tok = jnp.arange(n, dtype=jnp.int32)
    src = (jnp.arange(n_rows, dtype=jnp.int32) % n).at[jnp.concatenate([pos0, pos1])].set(
        jnp.concatenate([tok, tok]), mode="promise_in_bounds", unique_indices=True)

    tile = jnp.arange(n_tiles, dtype=jnp.int32)
    te = jnp.minimum(jnp.sum((tile_end[None, :] <= tile[:, None]).astype(jnp.int32), axis=1), n_exp - 1)
    tr = jnp.clip(counts[te] - (tile - tile_start[te]) * FFN_TM, 0, FFN_TM)
    tr = jnp.where(tile < tile_end[-1], tr, 0)
    te = jnp.concatenate([te, tile_end[-1:] - 1])
    return src, pos0, pos1, w0, w1, te.astype(jnp.int32), tr.astype(jnp.int32)


def kernel(x, c, ctx, c_ctx, w_mod, b_mod, norm_mix, norm_ffn, w_in, w_out, pool_w, pool_scale, diff_lam,
           diff_subln, sgu_ln_g, sgu_ln_b, sgu_w, sgu_b, win_sink, ffn_w1, ffn_w3, ffn_w2, moe_router,
           moe_w1, moe_w3, moe_w2, norm_final):
    b, n_lat, d = x.shape
    n_ctx = ctx.shape[1]
    t = n_ctx + n_lat
    depth = w_mod.shape[0]
    segments = ((0, n_ctx), (n_ctx, n_lat))

    c_rows = jnp.zeros((8, d), F32).at[:b].set(c).at[b].set(c_ctx)
    mod = _adaln(c_rows, w_mod, b_mod).reshape(depth, 8, 6, d)
    cos_t, sin_t = _rope_tables(n_ctx, n_lat)
    h = jnp.concatenate([ctx, x], axis=1)

    out = None
    for l in range(depth):
        last = l == depth - 1
        lat_m = mod[l, :b]
        ctx_m = jnp.broadcast_to(mod[l, b:b + 1], lat_m.shape)
        modsel = jnp.stack([ctx_m, lat_m], axis=1)

        p_all = _inproj(h, modsel, norm_mix[l], w_in, l, n_ctx)

        lam_init = 0.8 - 0.6 * math.exp(-0.3 * l)
        lq1, lk1, lq2, lk2 = diff_lam[l].astype(F32)
        lam = jnp.exp(jnp.sum(lq1 * lk1)) - jnp.exp(jnp.sum(lq2 * lk2)) + lam_init

        mixes = (_pool(p_all, pool_w[l], pool_scale[l], segments),
                 _diff(p_all, lam, diff_subln[l], cos_t, sin_t, n_ctx, lam_init),
                 _sgu(p_all, sgu_ln_g[l], sgu_ln_b[l], sgu_w[l], sgu_b[l]),
                 _win(p_all, win_sink[l], cos_t, sin_t, n_ctx))
        h = _outproj(mixes, w_out, l, h, modsel, n_ctx)

        i = l // 2
        row0 = n_ctx if last else 0
        rows = b * (t - row0)
        if l % 2 == 0:
            assert not last
            h = _ffn_dense(h, modsel, norm_ffn[l], ffn_w1, ffn_w3, ffn_w2, i, n_ctx)
        else:
            assert last
            f, gates, sel = _ffn_in_route(h, modsel, norm_ffn[l], n_ctx, row0, moe_router[i])
            f = f.reshape(rows, d)
            n_exp = moe_w1.shape[1]
            n_tiles = (2 * rows) // FFN_TM + n_exp
            src, pos0, pos1, w0, w1, te, tr = _routing_tables(
                sel.reshape(rows, LANES)[:, :n_exp] > 0, gates.reshape(rows, LANES)[:, :n_exp], n_tiles)
            rows_of = lambda a, idx: a.at[idx].get(mode="promise_in_bounds")
            ys = _ffn(rows_of(f, src), moe_w1[i], moe_w3[i], moe_w2[i], te, tr)
            y01 = rows_of(ys, jnp.concatenate([pos0, pos1])).reshape(2, b, t - row0, d)
            out = _moe_final(h, y01, w0.reshape(b, t - row0, 1), w1.reshape(b, t - row0, 1),
                             modsel, norm_final, row0)
    return out
```

```python
import functools
import math

import jax
import jax.numpy as jnp
from jax import lax
from jax.experimental import pallas as pl
from jax.experimental.pallas import tpu as pltpu

F32 = jnp.float32
BF16 = jnp.bfloat16

GRID_W = 64
GROUP_WIDTH = 512
HEAD_DIM = 64
ROPE_BASE = 10000.0
EPS = 1e-6
NEG_INF = -1e30
LANES = 128

POOL_WINDOWS = (2, 4, 8, 16)
POOL_PAD = 16
DIFF_HEADS = GROUP_WIDTH // (2 * HEAD_DIM)
DIFF_STEP_HEADS = 4
SGU_CHUNK = 128
SGU_GROUPS = 4
WIN_HEADS = GROUP_WIDTH // HEAD_DIM
WIN_KV_HEADS = 2
WIN_REP = WIN_HEADS // WIN_KV_HEADS
WINDOW = 128
WIN_BLOCK = 128
WIN_QB = 2 * WIN_BLOCK
WIN_BAND = WIN_QB + 2 * WIN_BLOCK
N_EXPERTS = 8

COL_POOL = 0
COL_DQ = GROUP_WIDTH
COL_DK = 2 * GROUP_WIDTH
COL_DV = 3 * GROUP_WIDTH
COL_Z = 4 * GROUP_WIDTH
COL_WQ = 6 * GROUP_WIDTH
COL_WK = COL_WQ + WIN_HEADS * HEAD_DIM
COL_WV = COL_WK + WIN_KV_HEADS * HEAD_DIM
IN_COLS = COL_WV + WIN_KV_HEADS * HEAD_DIM

VMEM_LIMIT = 56 * 1024 * 1024
NORM_ROWS = 32


def _params(sem, vmem=VMEM_LIMIT):
    return pltpu.CompilerParams(dimension_semantics=sem, vmem_limit_bytes=vmem)


def _sigmoid(x):
    return 1.0 / (1.0 + jnp.exp(-x))


def _row_is_ctx(tile_idx, tm, n_ctx):
    rows = tile_idx * tm + lax.broadcasted_iota(jnp.int32, (tm, 1), 0)
    return rows < n_ctx


def _mod_row(m_ref, k, is_ctx):
    return jnp.where(is_ctx, m_ref[0, k:k + 1, :], m_ref[1, k:k + 1, :])


def _norm_modulate_rows(x_ref, g_ref, m_ref, k_shift, first_row, tm, n_ctx, store):
    def body(r, carry):
        r0 = pl.multiple_of(r * NORM_ROWS, NORM_ROWS)
        which = ((first_row + r0) >= n_ctx).astype(jnp.int32)
        rs = pl.ds(r0, NORM_ROWS)
        x = x_ref[0, rs, :]
        ms = jnp.mean(x * x, axis=-1, keepdims=True)
        a = g_ref[...] * (1.0 + m_ref[which, k_shift + 1:k_shift + 2, :])
        store(rs, x * lax.rsqrt(ms + EPS) * a + m_ref[which, k_shift:k_shift + 1, :])
        return carry
    lax.fori_loop(0, tm // NORM_ROWS, body, 0, unroll=4)


def _adaln_kernel(c_ref, w_ref, b_ref, o_ref):
    cv = c_ref[...]
    s = (cv * _sigmoid(cv)).astype(BF16)
    o_ref[0] = jnp.dot(s, w_ref[0].astype(BF16), preferred_element_type=F32) + b_ref[0]


def _adaln(c_rows, w_mod, b_mod, tn=1024):
    depth, d, n = w_mod.shape
    r = c_rows.shape[0]
    return pl.pallas_call(
        _adaln_kernel,
        grid=(depth, n // tn),
        in_specs=[pl.BlockSpec((r, d), lambda l, j: (0, 0)),
                  pl.BlockSpec((1, d, tn), lambda l, j: (l, 0, j)),
                  pl.BlockSpec((1, 1, tn), lambda l, j: (l, 0, j))],
        out_specs=pl.BlockSpec((1, r, tn), lambda l, j: (l, 0, j)),
        out_shape=jax.ShapeDtypeStruct((depth, r, n), F32),
        compiler_params=_params(("arbitrary", "arbitrary")),
        name="adaln",
    )(c_rows, w_mod, b_mod.reshape(depth, 1, n))


def _inproj_kernel(x_ref, m_ref, g_ref, w_ref, o_ref, xn_ref, *, tm, n_ctx):
    @pl.when(pl.program_id(2) == 0)
    def _():
        def store(rs, y):
            xn_ref[rs, :] = y.astype(BF16)
        _norm_modulate_rows(x_ref, g_ref, m_ref, 0, pl.program_id(1) * tm, tm, n_ctx, store)

    o_ref[0] = jnp.dot(xn_ref[...], w_ref[...].astype(BF16), preferred_element_type=F32)


def _rope(a, cos, sin):
    lane = lax.broadcasted_iota(jnp.int32, (1, LANES), 1)
    first_half = (lane % HEAD_DIM) < (HEAD_DIM // 2)
    partner = jnp.where(first_half, pltpu.roll(a, LANES - HEAD_DIM // 2, 1), pltpu.roll(a, HEAD_DIM // 2, 1))
    return a * cos + partner * sin


def _inproj(h, modsel, g, w, layer, n_ctx, tn=768):
    b, t, d = h.shape
    tm = t // 2
    assert tm % NORM_ROWS == 0 and n_ctx % NORM_ROWS == 0 and IN_COLS % tn == 0
    kern = functools.partial(_inproj_kernel, tm=tm, n_ctx=n_ctx)
    return pl.pallas_call(
        kern,
        grid=(b, t // tm, IN_COLS // tn),
        in_specs=[pl.BlockSpec((1, tm, d), lambda bi, ti, j: (bi, ti, 0)),
                  pl.BlockSpec((None, 2, 6, d), lambda bi, ti, j: (bi, 0, 0, 0)),
                  pl.BlockSpec((1, d), lambda bi, ti, j: (0, 0)),
                  pl.BlockSpec((None, d, tn), lambda bi, ti, j: (layer, 0, j))],
        out_specs=pl.BlockSpec((1, tm, tn), lambda bi, ti, j: (bi, ti, j)),
        out_shape=jax.ShapeDtypeStruct((b, t, IN_COLS), F32),
        scratch_shapes=[pltpu.VMEM((tm, d), BF16)],
        compiler_params=_params(("arbitrary", "arbitrary", "arbitrary")),
        name="inproj",
    )(h, modsel, g.reshape(1, d), w)


def _rope_tables(n_ctx, n_lat):
    n_rows = n_lat // GRID_W
    rows = jnp.repeat(jnp.arange(n_rows, dtype=F32), GRID_W)
    cols = jnp.tile(jnp.arange(GRID_W, dtype=F32), n_rows)
    n_freq = HEAD_DIM // 4
    inv_freq = ROPE_BASE ** (-jnp.arange(n_freq, dtype=F32) / n_freq)
    ang = jnp.concatenate([rows[:, None] * inv_freq, cols[:, None] * inv_freq], axis=-1)
    cos, sin = jnp.cos(ang), jnp.sin(ang)
    cos_h = jnp.concatenate([cos, cos], axis=-1)
    sin_h = jnp.concatenate([-sin, sin], axis=-1)
    cos_t = jnp.concatenate([jnp.ones((n_ctx, HEAD_DIM), F32), cos_h], axis=0)
    sin_t = jnp.concatenate([jnp.zeros((n_ctx, HEAD_DIM), F32), sin_h], axis=0)
    rep = LANES // HEAD_DIM
    return jnp.tile(cos_t, (1, rep)), jnp.tile(sin_t, (1, rep))


def _pool_kernel(a_ref, w_ref, s_ref, o_ref, pad_ref, *, segments):
    for g, win in enumerate(POOL_WINDOWS):
        half = win // 2
        cs = slice(g * LANES, (g + 1) * LANES)
        wg = w_ref[g].astype(BF16)
        for s0, n in segments:
            a = a_ref[0, s0:s0 + n, cs]
            pad_ref[0:POOL_PAD, :] = jnp.zeros((POOL_PAD, LANES), F32)
            pad_ref[POOL_PAD:POOL_PAD + n, :] = a
            pad_ref[POOL_PAD + n:2 * POOL_PAD + n, :] = jnp.zeros((POOL_PAD, LANES), F32)
            tot = pad_ref[POOL_PAD - half:POOL_PAD - half + n, :]
            for k in range(-half + 1, half):
                tot = tot + pad_ref[POOL_PAD + k:POOL_PAD + k + n, :]
            t = lax.broadcasted_iota(jnp.int32, (n, 1), 0)
            cnt = (jnp.minimum(t + half, n) - jnp.maximum(t - half, 0)).astype(F32)
            dlt = (tot / cnt - a).astype(BF16)
            y = jnp.dot(dlt, wg, preferred_element_type=F32)
            o_ref[0, s0:s0 + n, cs] = (y * s_ref[:, cs]).astype(BF16)


def _pool(p_all, w, scale, segments):
    b, t, _ = p_all.shape
    max_n = max(n for _, n in segments)
    kern = functools.partial(_pool_kernel, segments=segments)
    return pl.pallas_call(
        kern,
        grid=(b,),
        in_specs=[pl.BlockSpec((1, t, GROUP_WIDTH), lambda bi: (bi, 0, COL_POOL // GROUP_WIDTH)),
                  pl.BlockSpec(w.shape, lambda bi: (0, 0, 0)),
                  pl.BlockSpec((1, GROUP_WIDTH), lambda bi: (0, 0))],
        out_specs=pl.BlockSpec((1, t, GROUP_WIDTH), lambda bi: (bi, 0, 0)),
        out_shape=jax.ShapeDtypeStruct((b, t, GROUP_WIDTH), BF16),
        scratch_shapes=[pltpu.VMEM((max_n + 2 * POOL_PAD, LANES), F32)],
        compiler_params=_params(("arbitrary",)),
        name="pool",
    )(p_all, w, scale.reshape(1, GROUP_WIDTH))


def _diff_kernel(lam_ref, q_ref, k_ref, v_ref, g_ref, cos_ref, sin_ref, o_ref, kt_ref, vb_ref,
                 *, tq, n_ctx, out_scale):
    ti = pl.program_id(2)
    lam = lam_ref[0]
    w = 2 * HEAD_DIM

    @pl.when(ti == 0)
    def _():
        for hh in range(DIFF_STEP_HEADS):
            cs = slice(hh * w, (hh + 1) * w)
            kt_ref[cs, :] = _rope(k_ref[0, :, cs], cos_ref[...], sin_ref[...]).T.astype(BF16)
        vb_ref[...] = v_ref[0].astype(BF16)

    def attend(n_keys):
        lane = lax.broadcasted_iota(jnp.int32, (1, w), 1)
        rows = pl.ds(pl.multiple_of(ti * tq, tq), tq)
        cos = cos_ref[rows, :]
        sin = sin_ref[rows, :]
        for hh in range(DIFF_STEP_HEADS):
            cs = slice(hh * w, (hh + 1) * w)
            q = _rope(q_ref[0, :, cs], cos, sin) * (HEAD_DIM ** -0.5)
            q1 = jnp.where(lane < HEAD_DIM, q, 0.0).astype(BF16)
            q2 = jnp.where(lane >= HEAD_DIM, q, 0.0).astype(BF16)
            kt = kt_ref[cs, 0:n_keys]
            v = vb_ref[0:n_keys, cs]
            s1 = jnp.dot(q1, kt, preferred_element_type=F32)
            s2 = jnp.dot(q2, kt, preferred_element_type=F32)
            e1 = jnp.exp(s1 - jnp.max(s1, axis=-1, keepdims=True))
            e2 = jnp.exp(s2 - jnp.max(s2, axis=-1, keepdims=True))
            r1 = 1.0 / jnp.sum(e1, axis=-1, keepdims=True)
            r2 = lam / jnp.sum(e2, axis=-1, keepdims=True)
            o = (jnp.dot(e1.astype(BF16), v, preferred_element_type=F32) * r1
                 - jnp.dot(e2.astype(BF16), v, preferred_element_type=F32) * r2)
            ms = jnp.mean(o * o, axis=-1, keepdims=True)
            o_ref[0, :, cs] = (o * lax.rsqrt(ms + EPS) * g_ref[...] * out_scale).astype(BF16)

    @pl.when(ti * tq < n_ctx)
    def _():
        attend(n_ctx)

    @pl.when(ti * tq >= n_ctx)
    def _():
        attend(k_ref.shape[1])


def _diff(p_all, lam, subln, cos_t, sin_t, n_ctx, lam_init, tq=256):
    b, t, _ = p_all.shape
    assert n_ctx % tq == 0 and t % tq == 0 and DIFF_HEADS % DIFF_STEP_HEADS == 0
    w = 2 * HEAD_DIM
    sw = DIFF_STEP_HEADS * w
    table = pl.BlockSpec((t, LANES), lambda bi, hi, ti: (0, 0))
    kern = functools.partial(_diff_kernel, tq=tq, n_ctx=n_ctx, out_scale=1.0 - lam_init)
    return pl.pallas_call(
        kern,
        grid=(b, DIFF_HEADS // DIFF_STEP_HEADS, t // tq),
        in_specs=[pl.BlockSpec(memory_space=pltpu.SMEM),
                  pl.BlockSpec((1, tq, sw), lambda bi, hi, ti: (bi, ti, COL_DQ // sw + hi)),
                  pl.BlockSpec((1, t, sw), lambda bi, hi, ti: (bi, 0, COL_DK // sw + hi)),
                  pl.BlockSpec((1, t, sw), lambda bi, hi, ti: (bi, 0, COL_DV // sw + hi)),
                  pl.BlockSpec((1, w), lambda bi, hi, ti: (0, 0)), table, table],
        out_specs=pl.BlockSpec((1, tq, sw), lambda bi, hi, ti: (bi, ti, hi)),
        out_shape=jax.ShapeDtypeStruct((b, t, GROUP_WIDTH), BF16),
        scratch_shapes=[pltpu.VMEM((sw, t), BF16), pltpu.VMEM((t, sw), BF16)],
        compiler_params=_params(("arbitrary", "arbitrary", "arbitrary")),
        name="diff_attn",
    )(lam.reshape(1), p_all, p_all, p_all, subln.reshape(1, w), cos_t, sin_t)


def _gelu_tanh(x):
    return 0.5 * x * (1.0 + jnp.tanh(math.sqrt(2.0 / math.pi) * (x + 0.044715 * (x * x * x))))


def _sgu_kernel(z_ref, g_ref, b_ref, w_ref, bs_ref, o_ref, *, tm):
    z = _gelu_tanh(z_ref[0])
    u = z[:, :GROUP_WIDTH]
    v = z[:, GROUP_WIDTH:]
    mu = jnp.mean(v, axis=-1, keepdims=True)
    var = jnp.mean(jnp.square(v - mu), axis=-1, keepdims=True)
    vn = ((v - mu) * lax.rsqrt(var + EPS) * g_ref[...] + b_ref[...]).astype(BF16)
    for g in range(SGU_GROUPS):
        wg = w_ref[g].astype(BF16)
        bias = bs_ref[:, g:g + 1]
        cs = slice(g * LANES, (g + 1) * LANES)
        for c in range(tm // SGU_CHUNK):
            rs = slice(c * SGU_CHUNK, (c + 1) * SGU_CHUNK)
            sv = jnp.dot(wg, vn[rs, cs], preferred_element_type=F32) + bias
            o_ref[0, rs, cs] = (u[rs, cs] * sv).astype(BF16)


def _sgu(p_all, ln_g, ln_b, w_s, b_s, tm=768):
    b, t, _ = p_all.shape
    zw = 2 * GROUP_WIDTH
    kern = functools.partial(_sgu_kernel, tm=tm)
    return pl.pallas_call(
        kern,
        grid=(b, t // tm),
        in_specs=[pl.BlockSpec((1, tm, zw), lambda bi, ti: (bi, ti, COL_Z // zw)),
                  pl.BlockSpec((1, GROUP_WIDTH), lambda bi, ti: (0, 0)),
                  pl.BlockSpec((1, GROUP_WIDTH), lambda bi, ti: (0, 0)),
                  pl.BlockSpec(w_s.shape, lambda bi, ti: (0, 0, 0)),
                  pl.BlockSpec((SGU_CHUNK, SGU_GROUPS), lambda bi, ti: (0, 0))],
        out_specs=pl.BlockSpec((1, tm, GROUP_WIDTH), lambda bi, ti: (bi, ti, 0)),
        out_shape=jax.ShapeDtypeStruct((b, t, GROUP_WIDTH), BF16),
        compiler_params=_params(("arbitrary", "arbitrary")),
        name="sgu",
    )(p_all, ln_g.reshape(1, GROUP_WIDTH), ln_b.reshape(1, GROUP_WIDTH), w_s, b_s.T)


def _win_band_start(step, t):
    return jnp.clip(step * WIN_QB - WIN_BLOCK, 0, t - WIN_BAND)


def _win_bias(n_ctx, t):
    step = jnp.arange(t // WIN_QB, dtype=jnp.int32)[:, None, None]
    qpos = step * WIN_QB + jnp.arange(WIN_QB, dtype=jnp.int32)[None, :, None]
    kpos = _win_band_start(step, t) + jnp.arange(WIN_BAND, dtype=jnp.int32)[None, None, :]
    valid = (kpos >= n_ctx) & (qpos >= n_ctx) & (jnp.abs(kpos - qpos) <= WINDOW)
    return jnp.where(valid, 0.0, NEG_INF).astype(F32)


def _win_kernel(sink_ref, q_ref, k_ref, v_ref, bias_ref, cos_ref, sin_ref, o_ref, kt_ref, vv_ref, *, n_ctx):
    n = pl.program_id(1)
    t = k_ref.shape[1]
    blk = WIN_QB
    band = WIN_BAND
    lane = lax.broadcasted_iota(jnp.int32, (1, LANES), 1)
    low = lane < HEAD_DIM

    @pl.when(n == 0)
    def _():
        kr = _rope(k_ref[0], cos_ref[...], sin_ref[...])
        for g in range(WIN_KV_HEADS):
            mine = (lane >= g * HEAD_DIM) & (lane < (g + 1) * HEAD_DIM)
            km = jnp.where(mine, kr, 0.0)
            vm = jnp.where(mine, v_ref[0], 0.0)
            kt_ref[g] = (km + pltpu.roll(km, HEAD_DIM, 1)).T.astype(BF16)
            vv_ref[g] = (vm + pltpu.roll(vm, HEAD_DIM, 1)).astype(BF16)

    rows = WIN_REP * blk
    head = lax.broadcasted_iota(jnp.int32, (rows, 1), 0) // blk
    start = pl.multiple_of(_win_band_start(n, t), WIN_BLOCK)
    q_rows = pl.ds(pl.multiple_of(n * blk, blk), blk)
    cos = cos_ref[q_rows, :]
    sin = sin_ref[q_rows, :]
    for g in range(WIN_KV_HEADS):
        pieces = []
        for pair in range(WIN_REP // 2):
            col = (g * (WIN_REP // 2) + pair) * LANES
            qp = _rope(q_ref[0, :, col:col + LANES], cos, sin) * (HEAD_DIM ** -0.5)
            pieces += [jnp.where(low, qp, 0.0), jnp.where(low, 0.0, qp)]
        qs = jnp.concatenate(pieces, axis=0).astype(BF16)
        s_c = jnp.dot(qs, kt_ref[g, :, 0:n_ctx], preferred_element_type=F32)
        s_b = jnp.dot(qs, kt_ref[g, :, pl.ds(start, band)], preferred_element_type=F32)
        s_b = (s_b.reshape(WIN_REP, blk, band) + bias_ref[...]).reshape(rows, band)
        sink = jnp.full((rows, 1), sink_ref[g * WIN_REP], F32)
        for r in range(1, WIN_REP):
            sink = jnp.where(head == r, sink_ref[g * WIN_REP + r], sink)
        chunks = lambda a: [a[:, c * LANES:(c + 1) * LANES] for c in range(a.shape[1] // LANES)]
        m = jnp.maximum(jnp.max(functools.reduce(jnp.maximum, chunks(s_c) + chunks(s_b)), axis=-1, keepdims=True),
                        sink)
        e_c = jnp.exp(s_c - m)
        e_b = jnp.exp(s_b - m)
        den = (jnp.sum(functools.reduce(jnp.add, chunks(e_c) + chunks(e_b)), axis=-1, keepdims=True)
               + jnp.exp(sink - m))
        o = (jnp.dot(e_c.astype(BF16), vv_ref[g, 0:n_ctx, :], preferred_element_type=F32)
             + jnp.dot(e_b.astype(BF16), vv_ref[g, pl.ds(start, band), :], preferred_element_type=F32)) / den
        for pair in range(WIN_REP // 2):
            col = (g * (WIN_REP // 2) + pair) * LANES
            lo = o[(2 * pair) * blk:(2 * pair + 1) * blk]
            hi = o[(2 * pair + 1) * blk:(2 * pair + 2) * blk]
            o_ref[0, :, col:col + LANES] = jnp.where(low, lo, hi).astype(BF16)


def _win(p_all, sink, cos_t, sin_t, n_ctx):
    b, t, _ = p_all.shape
    qw = WIN_REP * HEAD_DIM
    table = pl.BlockSpec((t, LANES), lambda bi, ni: (0, 0))
    assert t % WIN_QB == 0 and t >= WIN_BAND and qw * WIN_KV_HEADS == GROUP_WIDTH
    lo = -(-(n_ctx + WIN_BLOCK) // WIN_QB)
    hi = (t - WIN_BAND + WIN_BLOCK) // WIN_QB

    def bias_block(bi, ni):
        return (jnp.where((ni >= lo) & (ni <= hi), lo, ni), 0, 0)

    kern = functools.partial(_win_kernel, n_ctx=n_ctx)
    return pl.pallas_call(
        kern,
        grid=(b, t // WIN_QB),
        in_specs=[pl.BlockSpec(memory_space=pltpu.SMEM),
                  pl.BlockSpec((1, WIN_QB, GROUP_WIDTH), lambda bi, ni: (bi, ni, COL_WQ // GROUP_WIDTH)),
                  pl.BlockSpec((1, t, LANES), lambda bi, ni: (bi, 0, COL_WK // LANES)),
                  pl.BlockSpec((1, t, LANES), lambda bi, ni: (bi, 0, COL_WV // LANES)),
                  pl.BlockSpec((1, WIN_QB, WIN_BAND), bias_block), table, table],
        out_specs=pl.BlockSpec((1, WIN_QB, GROUP_WIDTH), lambda bi, ni: (bi, ni, 0)),
        out_shape=jax.ShapeDtypeStruct((b, t, GROUP_WIDTH), BF16),
        scratch_shapes=[pltpu.VMEM((WIN_KV_HEADS, LANES, t), BF16), pltpu.VMEM((WIN_KV_HEADS, t, LANES), BF16)],
        compiler_params=_params(("arbitrary", "arbitrary")),
        name="win_attn",
    )(sink, p_all, p_all, p_all, _win_bias(n_ctx, t), cos_t, sin_t)


def _outproj_kernel(a_ref, b_ref, c_ref, d_ref, w_ref, h_ref, m_ref, o_ref, wb_ref, *, tm, n_ctx):
    ti = pl.program_id(2)

    @pl.when((pl.program_id(1) == 0) & (ti == 0))
    def _():
        wb_ref[...] = w_ref[...].astype(BF16)

    y = None
    for i, r in enumerate((a_ref, b_ref, c_ref, d_ref)):
        part = jnp.dot(r[0], wb_ref[i * GROUP_WIDTH:(i + 1) * GROUP_WIDTH, :], preferred_element_type=F32)
        y = part if y is None else y + part
    gate = _mod_row(m_ref, 2, _row_is_ctx(ti, tm, n_ctx))
    o_ref[0] = h_ref[0] + gate * y


def _outproj(mixes, w, layer, h, modsel, n_ctx, tm=768, tn=1024):
    b, t, d = h.shape
    tn = min(tn, d)
    assert d % tn == 0 and t % tm == 0
    kern = functools.partial(_outproj_kernel, tm=tm, n_ctx=n_ctx)
    mix_spec = pl.BlockSpec((1, tm, GROUP_WIDTH), lambda j, bi, ti: (bi, ti, 0))
    return pl.pallas_call(
        kern,
        grid=(d // tn, b, t // tm),
        in_specs=[mix_spec, mix_spec, mix_spec, mix_spec,
                  pl.BlockSpec((None, 4 * GROUP_WIDTH, tn), lambda j, bi, ti: (layer, 0, j)),
                  pl.BlockSpec((1, tm, tn), lambda j, bi, ti: (bi, ti, j)),
                  pl.BlockSpec((None, 2, 6, tn), lambda j, bi, ti: (bi, 0, 0, j))],
        out_specs=pl.BlockSpec((1, tm, tn), lambda j, bi, ti: (bi, ti, j)),
        out_shape=jax.ShapeDtypeStruct((b, t, d), F32),
        scratch_shapes=[pltpu.VMEM((4 * GROUP_WIDTH, tn), BF16)],
        compiler_params=_params(("arbitrary", "arbitrary", "arbitrary")),
        name="outproj",
    )(*mixes, w, h, modsel)


def _ffn_in_kernel(x_ref, m_ref, g_ref, o_ref, *, tm, n_ctx, row0):
    def store(rs, y):
        o_ref[0, rs, :] = y
    _norm_modulate_rows(x_ref, g_ref, m_ref, 3, row0 + pl.program_id(1) * tm, tm, n_ctx, store)


def _route_kernel(x_ref, m_ref, g_ref, r_ref, o_ref, gate_ref, sel_ref, *, tm, n_ctx, row0):
    _ffn_in_kernel(x_ref, m_ref, g_ref, o_ref, tm=tm, n_ctx=n_ctx, row0=row0)
    f = o_ref[0]
    lane = lax.broadcasted_iota(jnp.int32, (1, LANES), 1)
    logits = jnp.full((tm, LANES), -jnp.inf, F32)
    for e in range(N_EXPERTS):
        logits = jnp.where(lane == e, jnp.sum(f * r_ref[e:e + 1, :], axis=-1, keepdims=True), logits)
    m1 = jnp.max(logits, axis=-1, keepdims=True)
    i1 = jnp.min(jnp.where(logits == m1, lane, LANES), axis=-1, keepdims=True)
    rest = jnp.where(lane == i1, -jnp.inf, logits)
    m2 = jnp.max(rest, axis=-1, keepdims=True)
    i2 = jnp.min(jnp.where(rest == m2, lane, LANES), axis=-1, keepdims=True)
    e2 = jnp.exp(m2 - m1)
    w1 = 1.0 / (1.0 + e2)
    w2 = e2 / (1.0 + e2)
    gate_ref[0] = (jnp.where(lane == i1, w1, 0.0) + jnp.where(lane == i2, w2, 0.0))[:, :N_EXPERTS]
    sel_ref[0] = ((lane == i1) | (lane == i2)).astype(jnp.int32)[:, :N_EXPERTS]


def _ffn_in_route(h, modsel, g, n_ctx, row0, router, tm=256):
    b, t, d = h.shape
    nt = (t - row0) // tm
    off = row0 // tm
    in_specs = [pl.BlockSpec((1, tm, d), lambda bi, ti: (bi, ti + off, 0)),
                pl.BlockSpec((None, 2, 6, d), lambda bi, ti: (bi, 0, 0, 0)),
                pl.BlockSpec((1, d), lambda bi, ti: (0, 0))]
    f_spec = pl.BlockSpec((1, tm, d), lambda bi, ti: (bi, ti, 0))
    f_shape = jax.ShapeDtypeStruct((b, t - row0, d), F32)
    assert router.shape == (d, N_EXPERTS)
    e_spec = pl.BlockSpec((1, tm, N_EXPERTS), lambda bi, ti: (bi, ti, 0))
    kern = functools.partial(_route_kernel, tm=tm, n_ctx=n_ctx, row0=row0)
    return pl.pallas_call(
        kern, grid=(b, nt),
        in_specs=in_specs + [pl.BlockSpec((N_EXPERTS, d), lambda bi, ti: (0, 0))],
        out_specs=[f_spec, e_spec, e_spec],
        out_shape=[f_shape, jax.ShapeDtypeStruct((b, t - row0, N_EXPERTS), F32),
                   jax.ShapeDtypeStruct((b, t - row0, N_EXPERTS), jnp.int32)],
        compiler_params=_params(("arbitrary", "arbitrary")), name="ffn_in_route",
    )(h, modsel, g.reshape(1, d), router.T)


FFN_TM = 1152
FFN_SUB = 192
FFN_VMEM_LIMIT = 60 * 1024 * 1024


def _swiglu_slice(xs, w1, w3, w2):
    h1 = jnp.dot(xs, w1.astype(BF16), preferred_element_type=F32)
    h3 = jnp.dot(xs, w3.astype(BF16), preferred_element_type=F32)
    act = (h1 * _sigmoid(h1) * h3).astype(BF16)
    return jnp.dot(act, w2.astype(BF16), preferred_element_type=F32)


def _ffn_kernel(te_ref, tr_ref, x_ref, w1_ref, w3_ref, w2_ref, o_ref, xs_ref):
    i = pl.program_id(0)
    j = pl.program_id(1)
    n_blocks = (tr_ref[i] + FFN_SUB - 1) // FFN_SUB

    @pl.when(j == 0)
    def _():
        o_ref[...] = jnp.zeros_like(o_ref)

    @pl.when((j == 0) & (n_blocks > 0))
    def _():
        xs_ref[...] = x_ref[...].astype(BF16)

    for nb in range(1, FFN_TM // FFN_SUB + 1):
        @pl.when(n_blocks == nb)
        def _(nb=nb):
            m = nb * FFN_SUB
            o_ref[0:m, :] += _swiglu_slice(xs_ref[0:m, :], w1_ref[0], w3_ref[0], w2_ref[0])


def _ffn(x_rows, w1, w3, w2, tile_expert, tile_rows, tf=256):
    nr, d = x_rows.shape
    n_exp, _, f = w1.shape
    nj = f // tf
    n_tiles = nr // FFN_TM

    def xrow(i, j, te, tr):
        return (jnp.where(tr[i] > 0, i, te[n_tiles]), 0)

    def wcol(i, j, te, tr):
        return (te[i], 0, jnp.where(tr[i] > 0, j, nj - 1))

    def wrow(i, j, te, tr):
        return (te[i], jnp.where(tr[i] > 0, j, nj - 1), 0)

    return pl.pallas_call(
        _ffn_kernel,
        grid_spec=pltpu.PrefetchScalarGridSpec(
            num_scalar_prefetch=2,
            grid=(n_tiles, nj),
            in_specs=[pl.BlockSpec((FFN_TM, d), xrow),
                      pl.BlockSpec((1, d, tf), wcol),
                      pl.BlockSpec((1, d, tf), wcol),
                      pl.BlockSpec((1, tf, d), wrow)],
            out_specs=pl.BlockSpec((FFN_TM, d), lambda i, j, te, tr: (i, 0)),
            scratch_shapes=[pltpu.VMEM((FFN_TM, d), BF16)]),
        out_shape=jax.ShapeDtypeStruct((nr, d), F32),
        compiler_params=_params(("arbitrary", "arbitrary"), FFN_VMEM_LIMIT),
        name="ffn",
    )(tile_expert, tile_rows, x_rows, w1, w3, w2)


def _ffn_dense_kernel(h_ref, m_ref, g_ref, w1_ref, w3_ref, w2_ref, o_ref, xs_ref, *, tm, n_ctx):
    ti = pl.program_id(1)
    j = pl.program_id(2)

    @pl.when(j == 0)
    def _():
        def store(rs, y):
            xs_ref[rs, :] = y.astype(BF16)
        _norm_modulate_rows(h_ref, g_ref, m_ref, 3, ti * tm, tm, n_ctx, store)
        o_ref[...] = jnp.zeros_like(o_ref)

    o_ref[0] += _swiglu_slice(xs_ref[...], w1_ref[...], w3_ref[...], w2_ref[...])

    @pl.when(j == pl.num_programs(2) - 1)
    def _():
        gate = _mod_row(m_ref, 5, _row_is_ctx(ti, tm, n_ctx))
        o_ref[0] = h_ref[0] + gate * o_ref[0]


def _ffn_dense(h, modsel, g, w1, w3, w2, layer, n_ctx, tf=256):
    b, t, d = h.shape
    f = w1.shape[2]
    tm = t // 2
    assert t % tm == 0 and f % tf == 0 and tm % NORM_ROWS == 0 and n_ctx % NORM_ROWS == 0
    kern = functools.partial(_ffn_dense_kernel, tm=tm, n_ctx=n_ctx)
    return pl.pallas_call(
        kern, grid=(b, t // tm, f // tf),
        in_specs=[pl.BlockSpec((1, tm, d), lambda bi, ti, j: (bi, ti, 0)),
                  pl.BlockSpec((None, 2, 6, d), lambda bi, ti, j: (bi, 0, 0, 0)),
                  pl.BlockSpec((1, d), lambda bi, ti, j: (0, 0)),
                  pl.BlockSpec((None, d, tf), lambda bi, ti, j: (layer, 0, j)),
                  pl.BlockSpec((None, d, tf), lambda bi, ti, j: (layer, 0, j)),
                  pl.BlockSpec((None, tf, d), lambda bi, ti, j: (layer, j, 0))],
        out_specs=pl.BlockSpec((1, tm, d), lambda bi, ti, j: (bi, ti, 0)),
        out_shape=jax.ShapeDtypeStruct((b, t, d), F32),
        scratch_shapes=[pltpu.VMEM((tm, d), BF16)],
        compiler_params=_params(("arbitrary", "arbitrary", "arbitrary"), FFN_VMEM_LIMIT), name="ffn_dense",
    )(h, modsel, g.reshape(1, d), w1, w3, w2)


def _moe_final_kernel(h_ref, y0_ref, y1_ref, w0_ref, w1_ref, m_ref, g_ref, o_ref):
    y = w0_ref[0] * y0_ref[0] + w1_ref[0] * y1_ref[0]
    hn = h_ref[0] + m_ref[1, 5:6, :] * y
    ms = jnp.mean(hn * hn, axis=-1, keepdims=True)
    o_ref[0] = hn * lax.rsqrt(ms + EPS) * g_ref[...]


def _moe_final(h, y01, w0, w1, modsel, g, row0, tm=256):
    b, t, d = h.shape
    n_lat = t - row0
    off = row0 // tm
    spec = pl.BlockSpec((1, tm, d), lambda bi, ti: (bi, ti, 0))
    wspec = pl.BlockSpec((1, tm, 1), lambda bi, ti: (bi, ti, 0))
    yspec = lambda slot: pl.BlockSpec((None, 1, tm, d), lambda bi, ti: (slot, bi, ti, 0))
    return pl.pallas_call(
        _moe_final_kernel, grid=(b, n_lat // tm),
        in_specs=[pl.BlockSpec((1, tm, d), lambda bi, ti: (bi, ti + off, 0)), yspec(0), yspec(1), wspec, wspec,
                  pl.BlockSpec((None, 2, 6, d), lambda bi, ti: (bi, 0, 0, 0)),
                  pl.BlockSpec((1, d), lambda bi, ti: (0, 0))],
        out_specs=spec, out_shape=jax.ShapeDtypeStruct((b, n_lat, d), F32),
        compiler_params=_params(("arbitrary", "arbitrary")), name="moe_final",
    )(h, y01, y01, w0, w1, modsel, g.reshape(1, d))


def _routing_tables(sel, gates, n_tiles):
    n, n_exp = sel.shape
    seli = sel.astype(jnp.int32)
    counts = jnp.sum(seli, axis=0)
    rank = jnp.cumsum(seli, axis=0) - seli
    tiles_e = (counts + FFN_TM - 1) // FFN_TM
    tile_end = jnp.cumsum(tiles_e)
    tile_start = tile_end - tiles_e
    pos = tile_start[None, :] * FFN_TM + rank
    n_rows = n_tiles * FFN_TM
    expert = jnp.arange(n_exp, dtype=jnp.int32)[None, :]
    e_lo = jnp.min(jnp.where(sel, expert, n_exp), axis=1, keepdims=True)
    e_hi = jnp.max(jnp.where(sel, expert, -1), axis=1, keepdims=True)
    pick = lambda a, e: jnp.sum(jnp.where(expert == e, a, 0), axis=1)
    pos0, pos1 = pick(pos, e_lo), pick(pos, e_hi)
    w0, w1 = pick(gates, e_lo), pick(gates, e_hi)
    tok = jnp.arange(n, dtype=jnp.int32)
    src = (jnp.arange(n_rows, dtype=jnp.int32) % n).at[jnp.concatenate([pos0, pos1])].set(
        jnp.concatenate([tok, tok]), mode="promise_in_bounds", unique_indices=True)

    tile = jnp.arange(n_tiles, dtype=jnp.int32)
    te = jnp.minimum(jnp.sum((tile_end[None, :] <= tile[:, None]).astype(jnp.int32), axis=1), n_exp - 1)
    tr = jnp.clip(counts[te] - (tile - tile_start[te]) * FFN_TM, 0, FFN_TM)
    tr = jnp.where(tile < tile_end[-1], tr, 0)
    te = jnp.concatenate([te, tile_end[-1:] - 1])
    return src, pos0, pos1, w0, w1, te.astype(jnp.int32), tr.astype(jnp.int32)


def kernel(x, c, ctx, c_ctx, w_mod, b_mod, norm_mix, norm_ffn, w_in, w_out, pool_w, pool_scale, diff_lam,
           diff_subln, sgu_ln_g, sgu_ln_b, sgu_w, sgu_b, win_sink, ffn_w1, ffn_w3, ffn_w2, moe_router,
           moe_w1, moe_w3, moe_w2, norm_final):
    b, n_lat, d = x.shape
    n_ctx = ctx.shape[1]
    t = n_ctx + n_lat
    depth = w_mod.shape[0]
    segments = ((0, n_ctx), (n_ctx, n_lat))

    c_rows = jnp.zeros((8, d), F32).at[:b].set(c).at[b].set(c_ctx)
    mod = _adaln(c_rows, w_mod, b_mod).reshape(depth, 8, 6, d)
    cos_t, sin_t = _rope_tables(n_ctx, n_lat)
    h = jnp.concatenate([ctx, x], axis=1)

    out = None
    for l in range(depth):
        last = l == depth - 1
        lat_m = mod[l, :b]
        ctx_m = jnp.broadcast_to(mod[l, b:b + 1], lat_m.shape)
        modsel = jnp.stack([ctx_m, lat_m], axis=1)

        p_all = _inproj(h, modsel, norm_mix[l], w_in, l, n_ctx)

        lam_init = 0.8 - 0.6 * math.exp(-0.3 * l)
        lq1, lk1, lq2, lk2 = diff_lam[l].astype(F32)
        lam = jnp.exp(jnp.sum(lq1 * lk1)) - jnp.exp(jnp.sum(lq2 * lk2)) + lam_init

        mixes = (_pool(p_all, pool_w[l], pool_scale[l], segments),
                 _diff(p_all, lam, diff_subln[l], cos_t, sin_t, n_ctx, lam_init),
                 _sgu(p_all, sgu_ln_g[l], sgu_ln_b[l], sgu_w[l], sgu_b[l]),
                 _win(p_all, win_sink[l], cos_t, sin_t, n_ctx))
        h = _outproj(mixes, w_out, l, h, modsel, n_ctx)

        i = l // 2
        row0 = n_ctx if last else 0
        rows = b * (t - row0)
        if l % 2 == 0:
            assert not last
            h = _ffn_dense(h, modsel, norm_ffn[l], ffn_w1, ffn_w3, ffn_w2, i, n_ctx)
        else:
            assert last
            f, gates, sel = _ffn_in_route(h, modsel, norm_ffn[l], n_ctx, row0, moe_router[i])
            f = f.reshape(rows, d)
            n_exp = moe_w1.shape[1]
            n_tiles = (2 * rows) // FFN_TM + n_exp
            src, pos0, pos1, w0, w1, te, tr = _routing_tables(
                sel.reshape(rows, n_exp) > 0, gates.reshape(rows, n_exp), n_tiles)
            rows_of = lambda a, idx: a.at[idx].get(mode="promise_in_bounds")
            ys = _ffn(rows_of(f, src), moe_w1[i], moe_w3[i], moe_w2[i], te, tr)
            y01 = rows_of(ys, jnp.concatenate([pos0, pos1])).reshape(2, b, t - row0, d)
            out = _moe_final(h, y01, w0.reshape(b, t - row0, 1), w1.reshape(b, t - row0, 1),
                             modsel, norm_final, row0)
    return out
```

```python
import functools
import math

import jax
import jax.numpy as jnp
from jax import lax
from jax.experimental import pallas as pl
from jax.experimental.pallas import tpu as pltpu

F32 = jnp.float32
BF16 = jnp.bfloat16

GRID_W = 64
GROUP_WIDTH = 512
HEAD_DIM = 64
ROPE_BASE = 10000.0
EPS = 1e-6
NEG_INF = -1e30
LANES = 128

POOL_WINDOWS = (2, 4, 8, 16)
POOL_PAD = 16
DIFF_HEADS = GROUP_WIDTH // (2 * HEAD_DIM)
DIFF_STEP_HEADS = 4
SGU_CHUNK = 128
SGU_GROUPS = 4
WIN_HEADS = GROUP_WIDTH // HEAD_DIM
WIN_KV_HEADS = 2
WIN_REP = WIN_HEADS // WIN_KV_HEADS
WINDOW = 128
WIN_BLOCK = 128
WIN_QB = 2 * WIN_BLOCK
WIN_BAND = WIN_QB + 2 * WIN_BLOCK
N_EXPERTS = 8

COL_POOL = 0
COL_DQ = GROUP_WIDTH
COL_DK = 2 * GROUP_WIDTH
COL_DV = 3 * GROUP_WIDTH
COL_Z = 4 * GROUP_WIDTH
COL_WQ = 6 * GROUP_WIDTH
COL_WK = COL_WQ + WIN_HEADS * HEAD_DIM
COL_WV = COL_WK + WIN_KV_HEADS * HEAD_DIM
IN_COLS = COL_WV + WIN_KV_HEADS * HEAD_DIM

VMEM_LIMIT = 56 * 1024 * 1024
NORM_ROWS = 32


def _params(sem, vmem=VMEM_LIMIT):
    return pltpu.CompilerParams(dimension_semantics=sem, vmem_limit_bytes=vmem)


def _sigmoid(x):
    return 1.0 / (1.0 + jnp.exp(-x))


def _row_is_ctx(tile_idx, tm, n_ctx):
    rows = tile_idx * tm + lax.broadcasted_iota(jnp.int32, (tm, 1), 0)
    return rows < n_ctx


def _mod_row(m_ref, k, is_ctx):
    return jnp.where(is_ctx, m_ref[0, k:k + 1, :], m_ref[1, k:k + 1, :])


def _norm_modulate_rows(x_ref, g_ref, m_ref, k_shift, first_row, tm, n_ctx, store):
    def body(r, carry):
        r0 = pl.multiple_of(r * NORM_ROWS, NORM_ROWS)
        which = ((first_row + r0) >= n_ctx).astype(jnp.int32)
        rs = pl.ds(r0, NORM_ROWS)
        x = x_ref[0, rs, :]
        ms = jnp.mean(x * x, axis=-1, keepdims=True)
        a = g_ref[...] * (1.0 + m_ref[which, k_shift + 1:k_shift + 2, :])
        store(rs, x * lax.rsqrt(ms + EPS) * a + m_ref[which, k_shift:k_shift + 1, :])
        return carry
    lax.fori_loop(0, tm // NORM_ROWS, body, 0, unroll=4)


def _adaln_kernel(c_ref, w_ref, b_ref, o_ref):
    cv = c_ref[...]
    s = (cv * _sigmoid(cv)).astype(BF16)
    o_ref[0] = jnp.dot(s, w_ref[0].astype(BF16), preferred_element_type=F32) + b_ref[0]


def _adaln(c_rows, w_mod, b_mod, tn=1024):
    depth, d, n = w_mod.shape
    r = c_rows.shape[0]
    return pl.pallas_call(
        _adaln_kernel,
        grid=(depth, n // tn),
        in_specs=[pl.BlockSpec((r, d), lambda l, j: (0, 0)),
                  pl.BlockSpec((1, d, tn), lambda l, j: (l, 0, j)),
                  pl.BlockSpec((1, 1, tn), lambda l, j: (l, 0, j))],
        out_specs=pl.BlockSpec((1, r, tn), lambda l, j: (l, 0, j)),
        out_shape=jax.ShapeDtypeStruct((depth, r, n), F32),
        compiler_params=_params(("arbitrary", "arbitrary")),
        name="adaln",
    )(c_rows, w_mod, b_mod.reshape(depth, 1, n))


def _inproj_kernel(x_ref, m_ref, g_ref, w_ref, o_ref, xn_ref, *, tm, n_ctx):
    @pl.when(pl.program_id(2) == 0)
    def _():
        def store(rs, y):
            xn_ref[rs, :] = y.astype(BF16)
        _norm_modulate_rows(x_ref, g_ref, m_ref, 0, pl.program_id(1) * tm, tm, n_ctx, store)

    o_ref[0] = jnp.dot(xn_ref[...], w_ref[...].astype(BF16), preferred_element_type=F32)


def _rope(a, cos, sin):
    lane = lax.broadcasted_iota(jnp.int32, (1, LANES), 1)
    first_half = (lane % HEAD_DIM) < (HEAD_DIM // 2)
    partner = jnp.where(first_half, pltpu.roll(a, LANES - HEAD_DIM // 2, 1), pltpu.roll(a, HEAD_DIM // 2, 1))
    return a * cos + partner * sin


def _inproj(h, modsel, g, w, layer, n_ctx, tn=768):
    b, t, d = h.shape
    tm = t // 2
    assert tm % NORM_ROWS == 0 and n_ctx % NORM_ROWS == 0 and IN_COLS % tn == 0
    kern = functools.partial(_inproj_kernel, tm=tm, n_ctx=n_ctx)
    return pl.pallas_call(
        kern,
        grid=(b, t // tm, IN_COLS // tn),
        in_specs=[pl.BlockSpec((1, tm, d), lambda bi, ti, j: (bi, ti, 0)),
                  pl.BlockSpec((None, 2, 6, d), lambda bi, ti, j: (bi, 0, 0, 0)),
                  pl.BlockSpec((1, d), lambda bi, ti, j: (0, 0)),
                  pl.BlockSpec((None, d, tn), lambda bi, ti, j: (layer, 0, j))],
        out_specs=pl.BlockSpec((1, tm, tn), lambda bi, ti, j: (bi, ti, j)),
        out_shape=jax.ShapeDtypeStruct((b, t, IN_COLS), F32),
        scratch_shapes=[pltpu.VMEM((tm, d), BF16)],
        compiler_params=_params(("arbitrary", "arbitrary", "arbitrary")),
        name="inproj",
    )(h, modsel, g.reshape(1, d), w)


def _rope_tables(n_ctx, n_lat):
    n_rows = n_lat // GRID_W
    rows = jnp.repeat(jnp.arange(n_rows, dtype=F32), GRID_W)
    cols = jnp.tile(jnp.arange(GRID_W, dtype=F32), n_rows)
    n_freq = HEAD_DIM // 4
    inv_freq = ROPE_BASE ** (-jnp.arange(n_freq, dtype=F32) / n_freq)
    ang = jnp.concatenate([rows[:, None] * inv_freq, cols[:, None] * inv_freq], axis=-1)
    cos, sin = jnp.cos(ang), jnp.sin(ang)
    cos_h = jnp.concatenate([cos, cos], axis=-1)
    sin_h = jnp.concatenate([-sin, sin], axis=-1)
    cos_t = jnp.concatenate([jnp.ones((n_ctx, HEAD_DIM), F32), cos_h], axis=0)
    sin_t = jnp.concatenate([jnp.zeros((n_ctx, HEAD_DIM), F32), sin_h], axis=0)
    rep = LANES // HEAD_DIM
    return jnp.tile(cos_t, (1, rep)), jnp.tile(sin_t, (1, rep))


def _pool_kernel(a_ref, w_ref, s_ref, o_ref, pad_ref, *, segments):
    for g, win in enumerate(POOL_WINDOWS):
        half = win // 2
        cs = slice(g * LANES, (g + 1) * LANES)
        wg = w_ref[g].astype(BF16)
        for s0, n in segments:
            a = a_ref[0, s0:s0 + n, cs]
            pad_ref[0:POOL_PAD, :] = jnp.zeros((POOL_PAD, LANES), F32)
            pad_ref[POOL_PAD:POOL_PAD + n, :] = a
            pad_ref[POOL_PAD + n:2 * POOL_PAD + n, :] = jnp.zeros((POOL_PAD, LANES), F32)
            tot = pad_ref[POOL_PAD - half:POOL_PAD - half + n, :]
            for k in range(-half + 1, half):
                tot = tot + pad_ref[POOL_PAD + k:POOL_PAD + k + n, :]
            t = lax.broadcasted_iota(jnp.int32, (n, 1), 0)
            cnt = (jnp.minimum(t + half, n) - jnp.maximum(t - half, 0)).astype(F32)
            dlt = (tot / cnt - a).astype(BF16)
            y = jnp.dot(dlt, wg, preferred_element_type=F32)
            o_ref[0, s0:s0 + n, cs] = (y * s_ref[:, cs]).astype(BF16)


def _pool(p_all, w, scale, segments):
    b, t, _ = p_all.shape
    max_n = max(n for _, n in segments)
    kern = functools.partial(_pool_kernel, segments=segments)
    return pl.pallas_call(
        kern,
        grid=(b,),
        in_specs=[pl.BlockSpec((1, t, GROUP_WIDTH), lambda bi: (bi, 0, COL_POOL // GROUP_WIDTH)),
                  pl.BlockSpec(w.shape, lambda bi: (0, 0, 0)),
                  pl.BlockSpec((1, GROUP_WIDTH), lambda bi: (0, 0))],
        out_specs=pl.BlockSpec((1, t, GROUP_WIDTH), lambda bi: (bi, 0, 0)),
        out_shape=jax.ShapeDtypeStruct((b, t, GROUP_WIDTH), BF16),
        scratch_shapes=[pltpu.VMEM((max_n + 2 * POOL_PAD, LANES), F32)],
        compiler_params=_params(("arbitrary",)),
        name="pool",
    )(p_all, w, scale.reshape(1, GROUP_WIDTH))


def _diff_kernel(lam_ref, q_ref, k_ref, v_ref, g_ref, cos_ref, sin_ref, o_ref, kt_ref, vb_ref,
                 *, tq, n_ctx, out_scale):
    ti = pl.program_id(2)
    lam = lam_ref[0]
    w = 2 * HEAD_DIM

    @pl.when(ti == 0)
    def _():
        for hh in range(DIFF_STEP_HEADS):
            cs = slice(hh * w, (hh + 1) * w)
            kt_ref[cs, :] = _rope(k_ref[0, :, cs], cos_ref[...], sin_ref[...]).T.astype(BF16)
        vb_ref[...] = v_ref[0].astype(BF16)

    def attend(n_keys):
        lane = lax.broadcasted_iota(jnp.int32, (1, w), 1)
        rows = pl.ds(pl.multiple_of(ti * tq, tq), tq)
        cos = cos_ref[rows, :]
        sin = sin_ref[rows, :]
        for hh in range(DIFF_STEP_HEADS):
            cs = slice(hh * w, (hh + 1) * w)
            q = _rope(q_ref[0, :, cs], cos, sin) * (HEAD_DIM ** -0.5)
            q1 = jnp.where(lane < HEAD_DIM, q, 0.0).astype(BF16)
            q2 = jnp.where(lane >= HEAD_DIM, q, 0.0).astype(BF16)
            kt = kt_ref[cs, 0:n_keys]
            v = vb_ref[0:n_keys, cs]
            s1 = jnp.dot(q1, kt, preferred_element_type=F32)
            s2 = jnp.dot(q2, kt, preferred_element_type=F32)
            e1 = jnp.exp(s1 - jnp.max(s1, axis=-1, keepdims=True))
            e2 = jnp.exp(s2 - jnp.max(s2, axis=-1, keepdims=True))
            r1 = 1.0 / jnp.sum(e1, axis=-1, keepdims=True)
            r2 = lam / jnp.sum(e2, axis=-1, keepdims=True)
            o = (jnp.dot(e1.astype(BF16), v, preferred_element_type=F32) * r1
                 - jnp.dot(e2.astype(BF16), v, preferred_element_type=F32) * r2)
            ms = jnp.mean(o * o, axis=-1, keepdims=True)
            o_ref[0, :, cs] = (o * lax.rsqrt(ms + EPS) * g_ref[...] * out_scale).astype(BF16)

    @pl.when(ti * tq < n_ctx)
    def _():
        attend(n_ctx)

    @pl.when(ti * tq >= n_ctx)
    def _():
        attend(k_ref.shape[1])


def _diff(p_all, lam, subln, cos_t, sin_t, n_ctx, lam_init, tq=256):
    b, t, _ = p_all.shape
    assert n_ctx % tq == 0 and t % tq == 0 and DIFF_HEADS % DIFF_STEP_HEADS == 0
    w = 2 * HEAD_DIM
    sw = DIFF_STEP_HEADS * w
    table = pl.BlockSpec((t, LANES), lambda bi, hi, ti: (0, 0))
    kern = functools.partial(_diff_kernel, tq=tq, n_ctx=n_ctx, out_scale=1.0 - lam_init)
    return pl.pallas_call(
        kern,
        grid=(b, DIFF_HEADS // DIFF_STEP_HEADS, t // tq),
        in_specs=[pl.BlockSpec(memory_space=pltpu.SMEM),
                  pl.BlockSpec((1, tq, sw), lambda bi, hi, ti: (bi, ti, COL_DQ // sw + hi)),
                  pl.BlockSpec((1, t, sw), lambda bi, hi, ti: (bi, 0, COL_DK // sw + hi)),
                  pl.BlockSpec((1, t, sw), lambda bi, hi, ti: (bi, 0, COL_DV // sw + hi)),
                  pl.BlockSpec((1, w), lambda bi, hi, ti: (0, 0)), table, table],
        out_specs=pl.BlockSpec((1, tq, sw), lambda bi, hi, ti: (bi, ti, hi)),
        out_shape=jax.ShapeDtypeStruct((b, t, GROUP_WIDTH), BF16),
        scratch_shapes=[pltpu.VMEM((sw, t), BF16), pltpu.VMEM((t, sw), BF16)],
        compiler_params=_params(("arbitrary", "arbitrary", "arbitrary")),
        name="diff_attn",
    )(lam.reshape(1), p_all, p_all, p_all, subln.reshape(1, w), cos_t, sin_t)


def _gelu_tanh(x):
    return 0.5 * x * (1.0 + jnp.tanh(math.sqrt(2.0 / math.pi) * (x + 0.044715 * (x * x * x))))


def _sgu_kernel(z_ref, g_ref, b_ref, w_ref, bs_ref, o_ref, *, tm):
    z = _gelu_tanh(z_ref[0])
    u = z[:, :GROUP_WIDTH]
    v = z[:, GROUP_WIDTH:]
    mu = jnp.mean(v, axis=-1, keepdims=True)
    var = jnp.mean(jnp.square(v - mu), axis=-1, keepdims=True)
    vn = ((v - mu) * lax.rsqrt(var + EPS) * g_ref[...] + b_ref[...]).astype(BF16)
    for g in range(SGU_GROUPS):
        wg = w_ref[g].astype(BF16)
        bias = bs_ref[:, g:g + 1]
        cs = slice(g * LANES, (g + 1) * LANES)
        for c in range(tm // SGU_CHUNK):
            rs = slice(c * SGU_CHUNK, (c + 1) * SGU_CHUNK)
            sv = jnp.dot(wg, vn[rs, cs], preferred_element_type=F32) + bias
            o_ref[0, rs, cs] = (u[rs, cs] * sv).astype(BF16)


def _sgu(p_all, ln_g, ln_b, w_s, b_s, tm=768):
    b, t, _ = p_all.shape
    zw = 2 * GROUP_WIDTH
    kern = functools.partial(_sgu_kernel, tm=tm)
    return pl.pallas_call(
        kern,
        grid=(b, t // tm),
        in_specs=[pl.BlockSpec((1, tm, zw), lambda bi, ti: (bi, ti, COL_Z // zw)),
                  pl.BlockSpec((1, GROUP_WIDTH), lambda bi, ti: (0, 0)),
                  pl.BlockSpec((1, GROUP_WIDTH), lambda bi, ti: (0, 0)),
                  pl.BlockSpec(w_s.shape, lambda bi, ti: (0, 0, 0)),
                  pl.BlockSpec((SGU_CHUNK, SGU_GROUPS), lambda bi, ti: (0, 0))],
        out_specs=pl.BlockSpec((1, tm, GROUP_WIDTH), lambda bi, ti: (bi, ti, 0)),
        out_shape=jax.ShapeDtypeStruct((b, t, GROUP_WIDTH), BF16),
        compiler_params=_params(("arbitrary", "arbitrary")),
        name="sgu",
    )(p_all, ln_g.reshape(1, GROUP_WIDTH), ln_b.reshape(1, GROUP_WIDTH), w_s, b_s.T)


def _win_band_start(step, t):
    return jnp.clip(step * WIN_QB - WIN_BLOCK, 0, t - WIN_BAND)


def _win_bias(n_ctx, t):
    step = jnp.arange(t // WIN_QB, dtype=jnp.int32)[:, None, None]
    qpos = step * WIN_QB + jnp.arange(WIN_QB, dtype=jnp.int32)[None, :, None]
    kpos = _win_band_start(step, t) + jnp.arange(WIN_BAND, dtype=jnp.int32)[None, None, :]
    valid = (kpos >= n_ctx) & (qpos >= n_ctx) & (jnp.abs(kpos - qpos) <= WINDOW)
    return jnp.where(valid, 0.0, NEG_INF).astype(F32)


def _win_kernel(sink_ref, q_ref, k_ref, v_ref, bias_ref, cos_ref, sin_ref, o_ref, kt_ref, vv_ref, *, n_ctx):
    n = pl.program_id(1)
    t = k_ref.shape[1]
    blk = WIN_QB
    band = WIN_BAND
    lane = lax.broadcasted_iota(jnp.int32, (1, LANES), 1)
    low = lane < HEAD_DIM

    @pl.when(n == 0)
    def _():
        kr = _rope(k_ref[0], cos_ref[...], sin_ref[...])
        for g in range(WIN_KV_HEADS):
            mine = (lane >= g * HEAD_DIM) & (lane < (g + 1) * HEAD_DIM)
            km = jnp.where(mine, kr, 0.0)
            vm = jnp.where(mine, v_ref[0], 0.0)
            kt_ref[g] = (km + pltpu.roll(km, HEAD_DIM, 1)).T.astype(BF16)
            vv_ref[g] = (vm + pltpu.roll(vm, HEAD_DIM, 1)).astype(BF16)

    rows = WIN_REP * blk
    head = lax.broadcasted_iota(jnp.int32, (rows, 1), 0) // blk
    start = pl.multiple_of(_win_band_start(n, t), WIN_BLOCK)
    q_rows = pl.ds(pl.multiple_of(n * blk, blk), blk)
    cos = cos_ref[q_rows, :]
    sin = sin_ref[q_rows, :]
    for g in range(WIN_KV_HEADS):
        pieces = []
        for pair in range(WIN_REP // 2):
            col = (g * (WIN_REP // 2) + pair) * LANES
            qp = _rope(q_ref[0, :, col:col + LANES], cos, sin) * (HEAD_DIM ** -0.5)
            pieces += [jnp.where(low, qp, 0.0), jnp.where(low, 0.0, qp)]
        qs = jnp.concatenate(pieces, axis=0).astype(BF16)
        s_c = jnp.dot(qs, kt_ref[g, :, 0:n_ctx], preferred_element_type=F32)
        s_b = jnp.dot(qs, kt_ref[g, :, pl.ds(start, band)], preferred_element_type=F32)
        s_b = (s_b.reshape(WIN_REP, blk, band) + bias_ref[...]).reshape(rows, band)
        sink = jnp.full((rows, 1), sink_ref[g * WIN_REP], F32)
        for r in range(1, WIN_REP):
            sink = jnp.where(head == r, sink_ref[g * WIN_REP + r], sink)
        chunks = lambda a: [a[:, c * LANES:(c + 1) * LANES] for c in range(a.shape[1] // LANES)]
        m = jnp.maximum(jnp.max(functools.reduce(jnp.maximum, chunks(s_c) + chunks(s_b)), axis=-1, keepdims=True),
                        sink)
        e_c = jnp.exp(s_c - m)
        e_b = jnp.exp(s_b - m)
        den = (jnp.sum(functools.reduce(jnp.add, chunks(e_c) + chunks(e_b)), axis=-1, keepdims=True)
               + jnp.exp(sink - m))
        o = (jnp.dot(e_c.astype(BF16), vv_ref[g, 0:n_ctx, :], preferred_element_type=F32)
             + jnp.dot(e_b.astype(BF16), vv_ref[g, pl.ds(start, band), :], preferred_element_type=F32)) / den
        for pair in range(WIN_REP // 2):
            col = (g * (WIN_REP // 2) + pair) * LANES
            lo = o[(2 * pair) * blk:(2 * pair + 1) * blk]
            hi = o[(2 * pair + 1) * blk:(2 * pair + 2) * blk]
            o_ref[0, :, col:col + LANES] = jnp.where(low, lo, hi).astype(BF16)


def _win(p_all, sink, cos_t, sin_t, n_ctx):
    b, t, _ = p_all.shape
    qw = WIN_REP * HEAD_DIM
    table = pl.BlockSpec((t, LANES), lambda bi, ni: (0, 0))
    assert t % WIN_QB == 0 and t >= WIN_BAND and qw * WIN_KV_HEADS == GROUP_WIDTH
    lo = -(-(n_ctx + WIN_BLOCK) // WIN_QB)
    hi = (t - WIN_BAND + WIN_BLOCK) // WIN_QB

    def bias_block(bi, ni):
        return (jnp.where((ni >= lo) & (ni <= hi), lo, ni), 0, 0)

    kern = functools.partial(_win_kernel, n_ctx=n_ctx)
    return pl.pallas_call(
        kern,
        grid=(b, t // WIN_QB),
        in_specs=[pl.BlockSpec(memory_space=pltpu.SMEM),
                  pl.BlockSpec((1, WIN_QB, GROUP_WIDTH), lambda bi, ni: (bi, ni, COL_WQ // GROUP_WIDTH)),
                  pl.BlockSpec((1, t, LANES), lambda bi, ni: (bi, 0, COL_WK // LANES)),
                  pl.BlockSpec((1, t, LANES), lambda bi, ni: (bi, 0, COL_WV // LANES)),
                  pl.BlockSpec((1, WIN_QB, WIN_BAND), bias_block), table, table],
        out_specs=pl.BlockSpec((1, WIN_QB, GROUP_WIDTH), lambda bi, ni: (bi, ni, 0)),
        out_shape=jax.ShapeDtypeStruct((b, t, GROUP_WIDTH), BF16),
        scratch_shapes=[pltpu.VMEM((WIN_KV_HEADS, LANES, t), BF16), pltpu.VMEM((WIN_KV_HEADS, t, LANES), BF16)],
        compiler_params=_params(("arbitrary", "arbitrary")),
        name="win_attn",
    )(sink, p_all, p_all, p_all, _win_bias(n_ctx, t), cos_t, sin_t)


def _outproj_kernel(a_ref, b_ref, c_ref, d_ref, w_ref, h_ref, m_ref, o_ref, wb_ref, *, tm, n_ctx):
    ti = pl.program_id(2)

    @pl.when((pl.program_id(1) == 0) & (ti == 0))
    def _():
        wb_ref[...] = w_ref[...].astype(BF16)

    y = None
    for i, r in enumerate((a_ref, b_ref, c_ref, d_ref)):
        part = jnp.dot(r[0], wb_ref[i * GROUP_WIDTH:(i + 1) * GROUP_WIDTH, :], preferred_element_type=F32)
        y = part if y is None else y + part
    gate = _mod_row(m_ref, 2, _row_is_ctx(ti, tm, n_ctx))
    o_ref[0] = h_ref[0] + gate * y


def _outproj(mixes, w, layer, h, modsel, n_ctx, tm=768, tn=1024):
    b, t, d = h.shape
    tn = min(tn, d)
    assert d % tn == 0 and t % tm == 0
    kern = functools.partial(_outproj_kernel, tm=tm, n_ctx=n_ctx)
    mix_spec = pl.BlockSpec((1, tm, GROUP_WIDTH), lambda j, bi, ti: (bi, ti, 0))
    return pl.pallas_call(
        kern,
        grid=(d // tn, b, t // tm),
        in_specs=[mix_spec, mix_spec, mix_spec, mix_spec,
                  pl.BlockSpec((None, 4 * GROUP_WIDTH, tn), lambda j, bi, ti: (layer, 0, j)),
                  pl.BlockSpec((1, tm, tn), lambda j, bi, ti: (bi, ti, j)),
                  pl.BlockSpec((None, 2, 6, tn), lambda j, bi, ti: (bi, 0, 0, j))],
        out_specs=pl.BlockSpec((1, tm, tn), lambda j, bi, ti: (bi, ti, j)),
        out_shape=jax.ShapeDtypeStruct((b, t, d), F32),
        scratch_shapes=[pltpu.VMEM((4 * GROUP_WIDTH, tn), BF16)],
        compiler_params=_params(("arbitrary", "arbitrary", "arbitrary")),
        name="outproj",
    )(*mixes, w, h, modsel)


def _ffn_in_kernel(x_ref, m_ref, g_ref, o_ref, *, tm, n_ctx, row0):
    def store(rs, y):
        o_ref[0, rs, :] = y
    _norm_modulate_rows(x_ref, g_ref, m_ref, 3, row0 + pl.program_id(1) * tm, tm, n_ctx, store)


def _route_kernel(x_ref, m_ref, g_ref, r_ref, o_ref, gate_ref, sel_ref, *, tm, n_ctx, row0):
    _ffn_in_kernel(x_ref, m_ref, g_ref, o_ref, tm=tm, n_ctx=n_ctx, row0=row0)
    f = o_ref[0]
    lane = lax.broadcasted_iota(jnp.int32, (1, LANES), 1)
    logits = jnp.full((tm, LANES), -jnp.inf, F32)
    for e in range(N_EXPERTS):
        logits = jnp.where(lane == e, jnp.sum(f * r_ref[e:e + 1, :], axis=-1, keepdims=True), logits)
    m1 = jnp.max(logits, axis=-1, keepdims=True)
    i1 = jnp.min(jnp.where(logits == m1, lane, LANES), axis=-1, keepdims=True)
    rest = jnp.where(lane == i1, -jnp.inf, logits)
    m2 = jnp.max(rest, axis=-1, keepdims=True)
    i2 = jnp.min(jnp.where(rest == m2, lane, LANES), axis=-1, keepdims=True)
    e2 = jnp.exp(m2 - m1)
    w1 = 1.0 / (1.0 + e2)
    w2 = e2 / (1.0 + e2)
    gate_ref[0] = (jnp.where(lane == i1, w1, 0.0) + jnp.where(lane == i2, w2, 0.0))[:, :N_EXPERTS]
    sel_ref[0] = ((lane == i1) | (lane == i2)).astype(jnp.int32)[:, :N_EXPERTS]


def _ffn_in_route(h, modsel, g, n_ctx, row0, router, tm=256):
    b, t, d = h.shape
    nt = (t - row0) // tm
    off = row0 // tm
    in_specs = [pl.BlockSpec((1, tm, d), lambda bi, ti: (bi, ti + off, 0)),
                pl.BlockSpec((None, 2, 6, d), lambda bi, ti: (bi, 0, 0, 0)),
                pl.BlockSpec((1, d), lambda bi, ti: (0, 0))]
    f_spec = pl.BlockSpec((1, tm, d), lambda bi, ti: (bi, ti, 0))
    f_shape = jax.ShapeDtypeStruct((b, t - row0, d), F32)
    assert router.shape == (d, N_EXPERTS)
    e_spec = pl.BlockSpec((1, tm, N_EXPERTS), lambda bi, ti: (bi, ti, 0))
    kern = functools.partial(_route_kernel, tm=tm, n_ctx=n_ctx, row0=row0)
    return pl.pallas_call(
        kern, grid=(b, nt),
        in_specs=in_specs + [pl.BlockSpec((N_EXPERTS, d), lambda bi, ti: (0, 0))],
        out_specs=[f_spec, e_spec, e_spec],
        out_shape=[f_shape, jax.ShapeDtypeStruct((b, t - row0, N_EXPERTS), F32),
                   jax.ShapeDtypeStruct((b, t - row0, N_EXPERTS), jnp.int32)],
        compiler_params=_params(("arbitrary", "arbitrary")), name="ffn_in_route",
    )(h, modsel, g.reshape(1, d), router.T)


FFN_TM = 1152
FFN_SUB = 128
FFN_VMEM_LIMIT = 60 * 1024 * 1024


def _swiglu_slice(xs, w1, w3, w2):
    h1 = jnp.dot(xs, w1.astype(BF16), preferred_element_type=F32)
    h3 = jnp.dot(xs, w3.astype(BF16), preferred_element_type=F32)
    act = (h1 * _sigmoid(h1) * h3).astype(BF16)
    return jnp.dot(act, w2.astype(BF16), preferred_element_type=F32)


def _ffn_kernel(te_ref, tr_ref, x_ref, w1_ref, w3_ref, w2_ref, o_ref, xs_ref):
    i = pl.program_id(0)
    j = pl.program_id(1)
    n_blocks = (tr_ref[i] + FFN_SUB - 1) // FFN_SUB

    @pl.when(j == 0)
    def _():
        o_ref[...] = jnp.zeros_like(o_ref)

    @pl.when((j == 0) & (n_blocks > 0))
    def _():
        xs_ref[...] = x_ref[...].astype(BF16)

    for nb in range(1, FFN_TM // FFN_SUB + 1):
        @pl.when(n_blocks == nb)
        def _(nb=nb):
            m = nb * FFN_SUB
            o_ref[0:m, :] += _swiglu_slice(xs_ref[0:m, :], w1_ref[0], w3_ref[0], w2_ref[0])


def _ffn(x_rows, w1, w3, w2, tile_expert, tile_rows, tf=256):
    nr, d = x_rows.shape
    n_exp, _, f = w1.shape
    nj = f // tf
    n_tiles = nr // FFN_TM

    def xrow(i, j, te, tr):
        return (jnp.where(tr[i] > 0, i, te[n_tiles]), 0)

    def wcol(i, j, te, tr):
        return (te[i], 0, jnp.where(tr[i] > 0, j, nj - 1))

    def wrow(i, j, te, tr):
        return (te[i], jnp.where(tr[i] > 0, j, nj - 1), 0)

    return pl.pallas_call(
        _ffn_kernel,
        grid_spec=pltpu.PrefetchScalarGridSpec(
            num_scalar_prefetch=2,
            grid=(n_tiles, nj),
            in_specs=[pl.BlockSpec((FFN_TM, d), xrow),
                      pl.BlockSpec((1, d, tf), wcol),
                      pl.BlockSpec((1, d, tf), wcol),
                      pl.BlockSpec((1, tf, d), wrow)],
            out_specs=pl.BlockSpec((FFN_TM, d), lambda i, j, te, tr: (i, 0)),
            scratch_shapes=[pltpu.VMEM((FFN_TM, d), BF16)]),
        out_shape=jax.ShapeDtypeStruct((nr, d), F32),
        compiler_params=_params(("arbitrary", "arbitrary"), FFN_VMEM_LIMIT),
        name="ffn",
    )(tile_expert, tile_rows, x_rows, w1, w3, w2)


def _ffn_dense_kernel(h_ref, m_ref, g_ref, w1_ref, w3_ref, w2_ref, o_ref, xs_ref, *, tm, n_ctx):
    ti = pl.program_id(1)
    j = pl.program_id(2)

    @pl.when(j == 0)
    def _():
        def store(rs, y):
            xs_ref[rs, :] = y.astype(BF16)
        _norm_modulate_rows(h_ref, g_ref, m_ref, 3, ti * tm, tm, n_ctx, store)
        o_ref[...] = jnp.zeros_like(o_ref)

    o_ref[0] += _swiglu_slice(xs_ref[...], w1_ref[...], w3_ref[...], w2_ref[...])

    @pl.when(j == pl.num_programs(2) - 1)
    def _():
        gate = _mod_row(m_ref, 5, _row_is_ctx(ti, tm, n_ctx))
        o_ref[0] = h_ref[0] + gate * o_ref[0]


def _ffn_dense(h, modsel, g, w1, w3, w2, layer, n_ctx, tf=256):
    b, t, d = h.shape
    f = w1.shape[2]
    tm = t // 2
    assert t % tm == 0 and f % tf == 0 and tm % NORM_ROWS == 0 and n_ctx % NORM_ROWS == 0
    kern = functools.partial(_ffn_dense_kernel, tm=tm, n_ctx=n_ctx)
    return pl.pallas_call(
        kern, grid=(b, t // tm, f // tf),
        in_specs=[pl.BlockSpec((1, tm, d), lambda bi, ti, j: (bi, ti, 0)),
                  pl.BlockSpec((None, 2, 6, d), lambda bi, ti, j: (bi, 0, 0, 0)),
                  pl.BlockSpec((1, d), lambda bi, ti, j: (0, 0)),
                  pl.BlockSpec((None, d, tf), lambda bi, ti, j: (layer, 0, j)),
                  pl.BlockSpec((None, d, tf), lambda bi, ti, j: (layer, 0, j)),
                  pl.BlockSpec((None, tf, d), lambda bi, ti, j: (layer, j, 0))],
        out_specs=pl.BlockSpec((1, tm, d), lambda bi, ti, j: (bi, ti, 0)),
        out_shape=jax.ShapeDtypeStruct((b, t, d), F32),
        scratch_shapes=[pltpu.VMEM((tm, d), BF16)],
        compiler_params=_params(("arbitrary", "arbitrary", "arbitrary"), FFN_VMEM_LIMIT), name="ffn_dense",
    )(h, modsel, g.reshape(1, d), w1, w3, w2)


def _moe_final_kernel(h_ref, y0_ref, y1_ref, w0_ref, w1_ref, m_ref, g_ref, o_ref):
    y = w0_ref[0] * y0_ref[0] + w1_ref[0] * y1_ref[0]
    hn = h_ref[0] + m_ref[1, 5:6, :] * y
    ms = jnp.mean(hn * hn, axis=-1, keepdims=True)
    o_ref[0] = hn * lax.rsqrt(ms + EPS) * g_ref[...]


def _moe_final(h, y01, w0, w1, modsel, g, row0, tm=256):
    b, t, d = h.shape
    n_lat = t - row0
    off = row0 // tm
    spec = pl.BlockSpec((1, tm, d), lambda bi, ti: (bi, ti, 0))
    wspec = pl.BlockSpec((1, tm, 1), lambda bi, ti: (bi, ti, 0))
    yspec = lambda slot: pl.BlockSpec((None, 1, tm, d), lambda bi, ti: (slot, bi, ti, 0))
    return pl.pallas_call(
        _moe_final_kernel, grid=(b, n_lat // tm),
        in_specs=[pl.BlockSpec((1, tm, d), lambda bi, ti: (bi, ti + off, 0)), yspec(0), yspec(1), wspec, wspec,
                  pl.BlockSpec((None, 2, 6, d), lambda bi, ti: (bi, 0, 0, 0)),
                  pl.BlockSpec((1, d), lambda bi, ti: (0, 0))],
        out_specs=spec, out_shape=jax.ShapeDtypeStruct((b, n_lat, d), F32),
        compiler_params=_params(("arbitrary", "arbitrary")), name="moe_final",
    )(h, y01, y01, w0, w1, modsel, g.reshape(1, d))


def _routing_tables(sel, gates, n_tiles):
    n, n_exp = sel.shape
    seli = sel.astype(jnp.int32)
    counts = jnp.sum(seli, axis=0)
    rank = jnp.cumsum(seli, axis=0) - seli
    tiles_e = (counts + FFN_TM - 1) // FFN_TM
    tile_end = jnp.cumsum(tiles_e)
    tile_start = tile_end - tiles_e
    pos = tile_start[None, :] * FFN_TM + rank
    n_rows = n_tiles * FFN_TM
    expert = jnp.arange(n_exp, dtype=jnp.int32)[None, :]
    e_lo = jnp.min(jnp.where(sel, expert, n_exp), axis=1, keepdims=True)
    e_hi = jnp.max(jnp.where(sel, expert, -1), axis=1, keepdims=True)
    pick = lambda a, e: jnp.sum(jnp.where(expert == e, a, 0), axis=1)
    pos0, pos1 = pick(pos, e_lo), pick(pos, e_hi)
    w0, w1 = pick(gates, e_lo), pick(gates, e_hi)
    tok = jnp.arange(n, dtype=jnp.int32)
    src = (jnp.arange(n_rows, dtype=jnp.int32) % n).at[jnp.concatenate([pos0, pos1])].set(
        jnp.concatenate([tok, tok]), mode="promise_in_bounds", unique_indices=True)

    tile = jnp.arange(n_tiles, dtype=jnp.int32)
    te = jnp.minimum(jnp.sum((tile_end[None, :] <= tile[:, None]).astype(jnp.int32), axis=1), n_exp - 1)
    tr = jnp.clip(counts[te] - (tile - tile_start[te]) * FFN_TM, 0, FFN_TM)
    tr = jnp.where(tile < tile_end[-1], tr, 0)
    te = jnp.concatenate([te, tile_end[-1:] - 1])
    return src, pos0, pos1, w0, w1, te.astype(jnp.int32), tr.astype(jnp.int32)


def kernel(x, c, ctx, c_ctx, w_mod, b_mod, norm_mix, norm_ffn, w_in, w_out, pool_w, pool_scale, diff_lam,
           diff_subln, sgu_ln_g, sgu_ln_b, sgu_w, sgu_b, win_sink, ffn_w1, ffn_w3, ffn_w2, moe_router,
           moe_w1, moe_w3, moe_w2, norm_final):
    b, n_lat, d = x.shape
    n_ctx = ctx.shape[1]
    t = n_ctx + n_lat
    depth = w_mod.shape[0]
    segments = ((0, n_ctx), (n_ctx, n_lat))

    c_rows = jnp.zeros((8, d), F32).at[:b].set(c).at[b].set(c_ctx)
    mod = _adaln(c_rows, w_mod, b_mod).reshape(depth, 8, 6, d)
    cos_t, sin_t = _rope_tables(n_ctx, n_lat)
    h = jnp.concatenate([ctx, x], axis=1)

    out = None
    for l in range(depth):
        last = l == depth - 1
        lat_m = mod[l, :b]
        ctx_m = jnp.broadcast_to(mod[l, b:b + 1], lat_m.shape)
        modsel = jnp.stack([ctx_m, lat_m], axis=1)

        p_all = _inproj(h, modsel, norm_mix[l], w_in, l, n_ctx)

        lam_init = 0.8 - 0.6 * math.exp(-0.3 * l)
        lq1, lk1, lq2, lk2 = diff_lam[l].astype(F32)
        lam = jnp.exp(jnp.sum(lq1 * lk1)) - jnp.exp(jnp.sum(lq2 * lk2)) + lam_init

        mixes = (_pool(p_all, pool_w[l], pool_scale[l], segments),
                 _diff(p_all, lam, diff_subln[l], cos_t, sin_t, n_ctx, lam_init),
                 _sgu(p_all, sgu_ln_g[l], sgu_ln_b[l], sgu_w[l], sgu_b[l]),
                 _win(p_all, win_sink[l], cos_t, sin_t, n_ctx))
        h = _outproj(mixes, w_out, l, h, modsel, n_ctx)

        i = l // 2
        row0 = n_ctx if last else 0
        rows = b * (t - row0)
        if l % 2 == 0:
            assert not last
            h = _ffn_dense(h, modsel, norm_ffn[l], ffn_w1, ffn_w3, ffn_w2, i, n_ctx)
        else:
            assert last
            f, gates, sel = _ffn_in_route(h, modsel, norm_ffn[l], n_ctx, row0, moe_router[i])
            f = f.reshape(rows, d)
            n_exp = moe_w1.shape[1]
            n_tiles = (2 * rows) // FFN_TM + n_exp
            src, pos0, pos1, w0, w1, te, tr = _routing_tables(
                sel.reshape(rows, n_exp) > 0, gates.reshape(rows, n_exp), n_tiles)
            rows_of = lambda a, idx: a.at[idx].get(mode="promise_in_bounds")
            ys = _ffn(rows_of(f, src), moe_w1[i], moe_w3[i], moe_w2[i], te, tr)
            y01 = rows_of(ys, jnp.concatenate([pos0, pos1])).reshape(2, b, t - row0, d)
            out = _moe_final(h, y01, w0.reshape(b, t - row0, 1), w1.reshape(b, t - row0, 1),
                             modsel, norm_final, row0)
    return out
```

```python
import functools
import math

import jax
import jax.numpy as jnp
from jax import lax
from jax.experimental import pallas as pl
from jax.experimental.pallas import tpu as pltpu

F32 = jnp.float32
BF16 = jnp.bfloat16

GRID_W = 64
GROUP_WIDTH = 512
HEAD_DIM = 64
ROPE_BASE = 10000.0
EPS = 1e-6
NEG_INF = -1e30
LANES = 128

POOL_WINDOWS = (2, 4, 8, 16)
POOL_PAD = 16
DIFF_HEADS = GROUP_WIDTH // (2 * HEAD_DIM)
DIFF_STEP_HEADS = 4
SGU_CHUNK = 128
SGU_GROUPS = 4
WIN_HEADS = GROUP_WIDTH // HEAD_DIM
WIN_KV_HEADS = 2
WIN_REP = WIN_HEADS // WIN_KV_HEADS
WINDOW = 128
WIN_BLOCK = 128
WIN_QB = 2 * WIN_BLOCK
WIN_BAND = WIN_QB + 2 * WIN_BLOCK
N_EXPERTS = 8

COL_POOL = 0
COL_DQ = GROUP_WIDTH
COL_DK = 2 * GROUP_WIDTH
COL_DV = 3 * GROUP_WIDTH
COL_Z = 4 * GROUP_WIDTH
COL_WQ = 6 * GROUP_WIDTH
COL_WK = COL_WQ + WIN_HEADS * HEAD_DIM
COL_WV = COL_WK + WIN_KV_HEADS * HEAD_DIM
IN_COLS = COL_WV + WIN_KV_HEADS * HEAD_DIM

VMEM_LIMIT = 56 * 1024 * 1024
NORM_ROWS = 32


def _params(sem, vmem=VMEM_LIMIT, fuse_inputs=None):
    return pltpu.CompilerParams(dimension_semantics=sem, vmem_limit_bytes=vmem, allow_input_fusion=fuse_inputs)


def _sigmoid(x):
    return 1.0 / (1.0 + jnp.exp(-x))


def _row_is_ctx(tile_idx, tm, n_ctx):
    rows = tile_idx * tm + lax.broadcasted_iota(jnp.int32, (tm, 1), 0)
    return rows < n_ctx


def _mod_row(m_ref, k, is_ctx):
    return jnp.where(is_ctx, m_ref[0, k:k + 1, :], m_ref[1, k:k + 1, :])


def _norm_modulate_rows(x_ref, g_ref, m_ref, k_shift, first_row, tm, n_ctx, store):
    def body(r, carry):
        r0 = pl.multiple_of(r * NORM_ROWS, NORM_ROWS)
        which = ((first_row + r0) >= n_ctx).astype(jnp.int32)
        rs = pl.ds(r0, NORM_ROWS)
        x = x_ref[0, rs, :]
        ms = jnp.mean(x * x, axis=-1, keepdims=True)
        a = g_ref[...] * (1.0 + m_ref[which, k_shift + 1:k_shift + 2, :])
        store(rs, x * lax.rsqrt(ms + EPS) * a + m_ref[which, k_shift:k_shift + 1, :])
        return carry
    lax.fori_loop(0, tm // NORM_ROWS, body, 0, unroll=4)


def _adaln_kernel(c_ref, w_ref, b_ref, o_ref):
    cv = c_ref[...]
    s = (cv * _sigmoid(cv)).astype(BF16)
    o_ref[0] = jnp.dot(s, w_ref[0].astype(BF16), preferred_element_type=F32) + b_ref[0]


def _adaln(c_rows, w_mod, b_mod, tn=1024):
    depth, d, n = w_mod.shape
    r = c_rows.shape[0]
    return pl.pallas_call(
        _adaln_kernel,
        grid=(depth, n // tn),
        in_specs=[pl.BlockSpec((r, d), lambda l, j: (0, 0)),
                  pl.BlockSpec((1, d, tn), lambda l, j: (l, 0, j)),
                  pl.BlockSpec((1, 1, tn), lambda l, j: (l, 0, j))],
        out_specs=pl.BlockSpec((1, r, tn), lambda l, j: (l, 0, j)),
        out_shape=jax.ShapeDtypeStruct((depth, r, n), F32),
        compiler_params=_params(("arbitrary", "arbitrary")),
        name="adaln",
    )(c_rows, w_mod, b_mod.reshape(depth, 1, n))


def _inproj_kernel(x_ref, m_ref, g_ref, w_ref, o_ref, xn_ref, *, tm, n_ctx):
    @pl.when(pl.program_id(2) == 0)
    def _():
        def store(rs, y):
            xn_ref[rs, :] = y.astype(BF16)
        _norm_modulate_rows(x_ref, g_ref, m_ref, 0, pl.program_id(1) * tm, tm, n_ctx, store)

    o_ref[0] = jnp.dot(xn_ref[...], w_ref[...].astype(BF16), preferred_element_type=F32)


def _rope(a, cos, sin):
    lane = lax.broadcasted_iota(jnp.int32, (1, LANES), 1)
    first_half = (lane % HEAD_DIM) < (HEAD_DIM // 2)
    partner = jnp.where(first_half, pltpu.roll(a, LANES - HEAD_DIM // 2, 1), pltpu.roll(a, HEAD_DIM // 2, 1))
    return a * cos + partner * sin


def _inproj(h, modsel, g, w, layer, n_ctx, tn=768):
    b, t, d = h.shape
    tm = t // 2
    assert tm % NORM_ROWS == 0 and n_ctx % NORM_ROWS == 0 and IN_COLS % tn == 0
    kern = functools.partial(_inproj_kernel, tm=tm, n_ctx=n_ctx)
    return pl.pallas_call(
        kern,
        grid=(b, t // tm, IN_COLS // tn),
        in_specs=[pl.BlockSpec((1, tm, d), lambda bi, ti, j: (bi, ti, 0)),
                  pl.BlockSpec((None, 2, 6, d), lambda bi, ti, j: (bi, 0, 0, 0)),
                  pl.BlockSpec((1, d), lambda bi, ti, j: (0, 0)),
                  pl.BlockSpec((None, d, tn), lambda bi, ti, j: (layer, 0, j))],
        out_specs=pl.BlockSpec((1, tm, tn), lambda bi, ti, j: (bi, ti, j)),
        out_shape=jax.ShapeDtypeStruct((b, t, IN_COLS), F32),
        scratch_shapes=[pltpu.VMEM((tm, d), BF16)],
        compiler_params=_params(("arbitrary", "arbitrary", "arbitrary"), fuse_inputs=[True, False, False, False]),
        name="inproj",
    )(h, modsel, g.reshape(1, d), w)


def _rope_tables(n_ctx, n_lat):
    n_rows = n_lat // GRID_W
    rows = jnp.repeat(jnp.arange(n_rows, dtype=F32), GRID_W)
    cols = jnp.tile(jnp.arange(GRID_W, dtype=F32), n_rows)
    n_freq = HEAD_DIM // 4
    inv_freq = ROPE_BASE ** (-jnp.arange(n_freq, dtype=F32) / n_freq)
    ang = jnp.concatenate([rows[:, None] * inv_freq, cols[:, None] * inv_freq], axis=-1)
    cos, sin = jnp.cos(ang), jnp.sin(ang)
    cos_h = jnp.concatenate([cos, cos], axis=-1)
    sin_h = jnp.concatenate([-sin, sin], axis=-1)
    cos_t = jnp.concatenate([jnp.ones((n_ctx, HEAD_DIM), F32), cos_h], axis=0)
    sin_t = jnp.concatenate([jnp.zeros((n_ctx, HEAD_DIM), F32), sin_h], axis=0)
    rep = LANES // HEAD_DIM
    return jnp.tile(cos_t, (1, rep)), jnp.tile(sin_t, (1, rep))


def _pool_kernel(a_ref, w_ref, s_ref, o_ref, pad_ref, *, segments):
    for g, win in enumerate(POOL_WINDOWS):
        half = win // 2
        cs = slice(g * LANES, (g + 1) * LANES)
        wg = w_ref[g].astype(BF16)
        for s0, n in segments:
            a = a_ref[0, s0:s0 + n, cs]
            pad_ref[0:POOL_PAD, :] = jnp.zeros((POOL_PAD, LANES), F32)
            pad_ref[POOL_PAD:POOL_PAD + n, :] = a
            pad_ref[POOL_PAD + n:2 * POOL_PAD + n, :] = jnp.zeros((POOL_PAD, LANES), F32)
            tot = pad_ref[POOL_PAD - half:POOL_PAD - half + n, :]
            for k in range(-half + 1, half):
                tot = tot + pad_ref[POOL_PAD + k:POOL_PAD + k + n, :]
            t = lax.broadcasted_iota(jnp.int32, (n, 1), 0)
            cnt = (jnp.minimum(t + half, n) - jnp.maximum(t - half, 0)).astype(F32)
            dlt = (tot / cnt - a).astype(BF16)
            y = jnp.dot(dlt, wg, preferred_element_type=F32)
            o_ref[0, s0:s0 + n, cs] = (y * s_ref[:, cs]).astype(BF16)


def _pool(p_all, w, scale, segments):
    b, t, _ = p_all.shape
    max_n = max(n for _, n in segments)
    kern = functools.partial(_pool_kernel, segments=segments)
    return pl.pallas_call(
        kern,
        grid=(b,),
        in_specs=[pl.BlockSpec((1, t, GROUP_WIDTH), lambda bi: (bi, 0, COL_POOL // GROUP_WIDTH)),
                  pl.BlockSpec(w.shape, lambda bi: (0, 0, 0)),
                  pl.BlockSpec((1, GROUP_WIDTH), lambda bi: (0, 0))],
        out_specs=pl.BlockSpec((1, t, GROUP_WIDTH), lambda bi: (bi, 0, 0)),
        out_shape=jax.ShapeDtypeStruct((b, t, GROUP_WIDTH), BF16),
        scratch_shapes=[pltpu.VMEM((max_n + 2 * POOL_PAD, LANES), F32)],
        compiler_params=_params(("arbitrary",)),
        name="pool",
    )(p_all, w, scale.reshape(1, GROUP_WIDTH))


def _diff_kernel(lam_ref, q_ref, k_ref, v_ref, g_ref, cos_ref, sin_ref, o_ref, kt_ref, vb_ref,
                 *, tq, n_ctx, out_scale):
    ti = pl.program_id(2)
    lam = lam_ref[0]
    w = 2 * HEAD_DIM

    @pl.when(ti == 0)
    def _():
        for hh in range(DIFF_STEP_HEADS):
            cs = slice(hh * w, (hh + 1) * w)
            kt_ref[cs, :] = _rope(k_ref[0, :, cs], cos_ref[...], sin_ref[...]).T.astype(BF16)
        vb_ref[...] = v_ref[0].astype(BF16)

    def attend(n_keys):
        lane = lax.broadcasted_iota(jnp.int32, (1, w), 1)
        rows = pl.ds(pl.multiple_of(ti * tq, tq), tq)
        cos = cos_ref[rows, :]
        sin = sin_ref[rows, :]
        for hh in range(DIFF_STEP_HEADS):
            cs = slice(hh * w, (hh + 1) * w)
            q = _rope(q_ref[0, :, cs], cos, sin) * (HEAD_DIM ** -0.5)
            q1 = jnp.where(lane < HEAD_DIM, q, 0.0).astype(BF16)
            q2 = jnp.where(lane >= HEAD_DIM, q, 0.0).astype(BF16)
            kt = kt_ref[cs, 0:n_keys]
            v = vb_ref[0:n_keys, cs]
            s1 = jnp.dot(q1, kt, preferred_element_type=F32)
            s2 = jnp.dot(q2, kt, preferred_element_type=F32)
            e1 = jnp.exp(s1 - jnp.max(s1, axis=-1, keepdims=True))
            e2 = jnp.exp(s2 - jnp.max(s2, axis=-1, keepdims=True))
            r1 = 1.0 / jnp.sum(e1, axis=-1, keepdims=True)
            r2 = lam / jnp.sum(e2, axis=-1, keepdims=True)
            o = (jnp.dot(e1.astype(BF16), v, preferred_element_type=F32) * r1
                 - jnp.dot(e2.astype(BF16), v, preferred_element_type=F32) * r2)
            ms = jnp.mean(o * o, axis=-1, keepdims=True)
            o_ref[0, :, cs] = (o * lax.rsqrt(ms + EPS) * g_ref[...] * out_scale).astype(BF16)

    @pl.when(ti * tq < n_ctx)
    def _():
        attend(n_ctx)

    @pl.when(ti * tq >= n_ctx)
    def _():
        attend(k_ref.shape[1])


def _diff(p_all, lam, subln, cos_t, sin_t, n_ctx, lam_init, tq=256):
    b, t, _ = p_all.shape
    assert n_ctx % tq == 0 and t % tq == 0 and DIFF_HEADS % DIFF_STEP_HEADS == 0
    w = 2 * HEAD_DIM
    sw = DIFF_STEP_HEADS * w
    table = pl.BlockSpec((t, LANES), lambda bi, hi, ti: (0, 0))
    kern = functools.partial(_diff_kernel, tq=tq, n_ctx=n_ctx, out_scale=1.0 - lam_init)
    return pl.pallas_call(
        kern,
        grid=(b, DIFF_HEADS // DIFF_STEP_HEADS, t // tq),
        in_specs=[pl.BlockSpec(memory_space=pltpu.SMEM),
                  pl.BlockSpec((1, tq, sw), lambda bi, hi, ti: (bi, ti, COL_DQ // sw + hi)),
                  pl.BlockSpec((1, t, sw), lambda bi, hi, ti: (bi, 0, COL_DK // sw + hi)),
                  pl.BlockSpec((1, t, sw), lambda bi, hi, ti: (bi, 0, COL_DV // sw + hi)),
                  pl.BlockSpec((1, w), lambda bi, hi, ti: (0, 0)), table, table],
        out_specs=pl.BlockSpec((1, tq, sw), lambda bi, hi, ti: (bi, ti, hi)),
        out_shape=jax.ShapeDtypeStruct((b, t, GROUP_WIDTH), BF16),
        scratch_shapes=[pltpu.VMEM((sw, t), BF16), pltpu.VMEM((t, sw), BF16)],
        compiler_params=_params(("arbitrary", "arbitrary", "arbitrary")),
        name="diff_attn",
    )(lam.reshape(1), p_all, p_all, p_all, subln.reshape(1, w), cos_t, sin_t)


def _gelu_tanh(x):
    return 0.5 * x * (1.0 + jnp.tanh(math.sqrt(2.0 / math.pi) * (x + 0.044715 * (x * x * x))))


def _sgu_kernel(z_ref, g_ref, b_ref, w_ref, bs_ref, o_ref, *, tm):
    z = _gelu_tanh(z_ref[0])
    u = z[:, :GROUP_WIDTH]
    v = z[:, GROUP_WIDTH:]
    mu = jnp.mean(v, axis=-1, keepdims=True)
    var = jnp.mean(jnp.square(v - mu), axis=-1, keepdims=True)
    vn = ((v - mu) * lax.rsqrt(var + EPS) * g_ref[...] + b_ref[...]).astype(BF16)
    for g in range(SGU_GROUPS):
        wg = w_ref[g].astype(BF16)
        bias = bs_ref[:, g:g + 1]
        cs = slice(g * LANES, (g + 1) * LANES)
        for c in range(tm // SGU_CHUNK):
            rs = slice(c * SGU_CHUNK, (c + 1) * SGU_CHUNK)
            sv = jnp.dot(wg, vn[rs, cs], preferred_element_type=F32) + bias
            o_ref[0, rs, cs] = (u[rs, cs] * sv).astype(BF16)


def _sgu(p_all, ln_g, ln_b, w_s, b_s, tm=768):
    b, t, _ = p_all.shape
    zw = 2 * GROUP_WIDTH
    kern = functools.partial(_sgu_kernel, tm=tm)
    return pl.pallas_call(
        kern,
        grid=(b, t // tm),
        in_specs=[pl.BlockSpec((1, tm, zw), lambda bi, ti: (bi, ti, COL_Z // zw)),
                  pl.BlockSpec((1, GROUP_WIDTH), lambda bi, ti: (0, 0)),
                  pl.BlockSpec((1, GROUP_WIDTH), lambda bi, ti: (0, 0)),
                  pl.BlockSpec(w_s.shape, lambda bi, ti: (0, 0, 0)),
                  pl.BlockSpec((SGU_CHUNK, SGU_GROUPS), lambda bi, ti: (0, 0))],
        out_specs=pl.BlockSpec((1, tm, GROUP_WIDTH), lambda bi, ti: (bi, ti, 0)),
        out_shape=jax.ShapeDtypeStruct((b, t, GROUP_WIDTH), BF16),
        compiler_params=_params(("arbitrary", "arbitrary")),
        name="sgu",
    )(p_all, ln_g.reshape(1, GROUP_WIDTH), ln_b.reshape(1, GROUP_WIDTH), w_s, b_s.T)


def _win_band_start(step, t):
    return jnp.clip(step * WIN_QB - WIN_BLOCK, 0, t - WIN_BAND)


def _win_bias(n_ctx, t):
    step = jnp.arange(t // WIN_QB, dtype=jnp.int32)[:, None, None]
    qpos = step * WIN_QB + jnp.arange(WIN_QB, dtype=jnp.int32)[None, :, None]
    kpos = _win_band_start(step, t) + jnp.arange(WIN_BAND, dtype=jnp.int32)[None, None, :]
    valid = (kpos >= n_ctx) & (qpos >= n_ctx) & (jnp.abs(kpos - qpos) <= WINDOW)
    return jnp.where(valid, 0.0, NEG_INF).astype(F32)


def _win_kernel(sink_ref, q_ref, k_ref, v_ref, bias_ref, cos_ref, sin_ref, o_ref, kt_ref, vv_ref, *, n_ctx):
    n = pl.program_id(1)
    t = k_ref.shape[1]
    blk = WIN_QB
    band = WIN_BAND
    lane = lax.broadcasted_iota(jnp.int32, (1, LANES), 1)
    low = lane < HEAD_DIM

    @pl.when(n == 0)
    def _():
        kr = _rope(k_ref[0], cos_ref[...], sin_ref[...])
        for g in range(WIN_KV_HEADS):
            mine = (lane >= g * HEAD_DIM) & (lane < (g + 1) * HEAD_DIM)
            km = jnp.where(mine, kr, 0.0)
            vm = jnp.where(mine, v_ref[0], 0.0)
            kt_ref[g] = (km + pltpu.roll(km, HEAD_DIM, 1)).T.astype(BF16)
            vv_ref[g] = (vm + pltpu.roll(vm, HEAD_DIM, 1)).astype(BF16)

    rows = WIN_REP * blk
    head = lax.broadcasted_iota(jnp.int32, (rows, 1), 0) // blk
    start = pl.multiple_of(_win_band_start(n, t), WIN_BLOCK)
    q_rows = pl.ds(pl.multiple_of(n * blk, blk), blk)
    cos = cos_ref[q_rows, :]
    sin = sin_ref[q_rows, :]
    for g in range(WIN_KV_HEADS):
        pieces = []
        for pair in range(WIN_REP // 2):
            col = (g * (WIN_REP // 2) + pair) * LANES
            qp = _rope(q_ref[0, :, col:col + LANES], cos, sin) * (HEAD_DIM ** -0.5)
            pieces += [jnp.where(low, qp, 0.0), jnp.where(low, 0.0, qp)]
        qs = jnp.concatenate(pieces, axis=0).astype(BF16)
        s_c = jnp.dot(qs, kt_ref[g, :, 0:n_ctx], preferred_element_type=F32)
        s_b = jnp.dot(qs, kt_ref[g, :, pl.ds(start, band)], preferred_element_type=F32)
        s_b = (s_b.reshape(WIN_REP, blk, band) + bias_ref[...]).reshape(rows, band)
        sink = jnp.full((rows, 1), sink_ref[g * WIN_REP], F32)
        for r in range(1, WIN_REP):
            sink = jnp.where(head == r, sink_ref[g * WIN_REP + r], sink)
        chunks = lambda a: [a[:, c * LANES:(c + 1) * LANES] for c in range(a.shape[1] // LANES)]
        m = jnp.maximum(jnp.max(functools.reduce(jnp.maximum, chunks(s_c) + chunks(s_b)), axis=-1, keepdims=True),
                        sink)
        e_c = jnp.exp(s_c - m)
        e_b = jnp.exp(s_b - m)
        den = (jnp.sum(functools.reduce(jnp.add, chunks(e_c) + chunks(e_b)), axis=-1, keepdims=True)
               + jnp.exp(sink - m))
        o = (jnp.dot(e_c.astype(BF16), vv_ref[g, 0:n_ctx, :], preferred_element_type=F32)
             + jnp.dot(e_b.astype(BF16), vv_ref[g, pl.ds(start, band), :], preferred_element_type=F32)) / den
        for pair in range(WIN_REP // 2):
            col = (g * (WIN_REP // 2) + pair) * LANES
            lo = o[(2 * pair) * blk:(2 * pair + 1) * blk]
            hi = o[(2 * pair + 1) * blk:(2 * pair + 2) * blk]
            o_ref[0, :, col:col + LANES] = jnp.where(low, lo, hi).astype(BF16)


def _win(p_all, sink, cos_t, sin_t, n_ctx):
    b, t, _ = p_all.shape
    qw = WIN_REP * HEAD_DIM
    table = pl.BlockSpec((t, LANES), lambda bi, ni: (0, 0))
    assert t % WIN_QB == 0 and t >= WIN_BAND and qw * WIN_KV_HEADS == GROUP_WIDTH
    lo = -(-(n_ctx + WIN_BLOCK) // WIN_QB)
    hi = (t - WIN_BAND + WIN_BLOCK) // WIN_QB

    def bias_block(bi, ni):
        return (jnp.where((ni >= lo) & (ni <= hi), lo, ni), 0, 0)

    kern = functools.partial(_win_kernel, n_ctx=n_ctx)
    return pl.pallas_call(
        kern,
        grid=(b, t // WIN_QB),
        in_specs=[pl.BlockSpec(memory_space=pltpu.SMEM),
                  pl.BlockSpec((1, WIN_QB, GROUP_WIDTH), lambda bi, ni: (bi, ni, COL_WQ // GROUP_WIDTH)),
                  pl.BlockSpec((1, t, LANES), lambda bi, ni: (bi, 0, COL_WK // LANES)),
                  pl.BlockSpec((1, t, LANES), lambda bi, ni: (bi, 0, COL_WV // LANES)),
                  pl.BlockSpec((1, WIN_QB, WIN_BAND), bias_block), table, table],
        out_specs=pl.BlockSpec((1, WIN_QB, GROUP_WIDTH), lambda bi, ni: (bi, ni, 0)),
        out_shape=jax.ShapeDtypeStruct((b, t, GROUP_WIDTH), BF16),
        scratch_shapes=[pltpu.VMEM((WIN_KV_HEADS, LANES, t), BF16), pltpu.VMEM((WIN_KV_HEADS, t, LANES), BF16)],
        compiler_params=_params(("arbitrary", "arbitrary")),
        name="win_attn",
    )(sink, p_all, p_all, p_all, _win_bias(n_ctx, t), cos_t, sin_t)


def _outproj_kernel(a_ref, b_ref, c_ref, d_ref, w_ref, h_ref, m_ref, o_ref, wb_ref, *, tm, n_ctx):
    ti = pl.program_id(2)

    @pl.when((pl.program_id(1) == 0) & (ti == 0))
    def _():
        wb_ref[...] = w_ref[...].astype(BF16)

    y = None
    for i, r in enumerate((a_ref, b_ref, c_ref, d_ref)):
        part = jnp.dot(r[0], wb_ref[i * GROUP_WIDTH:(i + 1) * GROUP_WIDTH, :], preferred_element_type=F32)
        y = part if y is None else y + part
    gate = _mod_row(m_ref, 2, _row_is_ctx(ti, tm, n_ctx))
    o_ref[0] = h_ref[0] + gate * y


def _outproj(mixes, w, layer, h, modsel, n_ctx, tm=768, tn=1024):
    b, t, d = h.shape
    tn = min(tn, d)
    assert d % tn == 0 and t % tm == 0
    kern = functools.partial(_outproj_kernel, tm=tm, n_ctx=n_ctx)
    mix_spec = pl.BlockSpec((1, tm, GROUP_WIDTH), lambda j, bi, ti: (bi, ti, 0))
    return pl.pallas_call(
        kern,
        grid=(d // tn, b, t // tm),
        in_specs=[mix_spec, mix_spec, mix_spec, mix_spec,
                  pl.BlockSpec((None, 4 * GROUP_WIDTH, tn), lambda j, bi, ti: (layer, 0, j)),
                  pl.BlockSpec((1, tm, tn), lambda j, bi, ti: (bi, ti, j)),
                  pl.BlockSpec((None, 2, 6, tn), lambda j, bi, ti: (bi, 0, 0, j))],
        out_specs=pl.BlockSpec((1, tm, tn), lambda j, bi, ti: (bi, ti, j)),
        out_shape=jax.ShapeDtypeStruct((b, t, d), F32),
        scratch_shapes=[pltpu.VMEM((4 * GROUP_WIDTH, tn), BF16)],
        compiler_params=_params(("arbitrary", "arbitrary", "arbitrary"),
                                fuse_inputs=[False, False, False, False, False, True, False]),
        name="outproj",
    )(*mixes, w, h, modsel)


def _ffn_in_kernel(x_ref, m_ref, g_ref, o_ref, *, tm, n_ctx, row0):
    def store(rs, y):
        o_ref[0, rs, :] = y
    _norm_modulate_rows(x_ref, g_ref, m_ref, 3, row0 + pl.program_id(1) * tm, tm, n_ctx, store)


def _route_kernel(x_ref, m_ref, g_ref, r_ref, o_ref, gate_ref, sel_ref, *, tm, n_ctx, row0):
    _ffn_in_kernel(x_ref, m_ref, g_ref, o_ref, tm=tm, n_ctx=n_ctx, row0=row0)
    f = o_ref[0]
    lane = lax.broadcasted_iota(jnp.int32, (1, LANES), 1)
    logits = jnp.full((tm, LANES), -jnp.inf, F32)
    for e in range(N_EXPERTS):
        logits = jnp.where(lane == e, jnp.sum(f * r_ref[e:e + 1, :], axis=-1, keepdims=True), logits)
    m1 = jnp.max(logits, axis=-1, keepdims=True)
    i1 = jnp.min(jnp.where(logits == m1, lane, LANES), axis=-1, keepdims=True)
    rest = jnp.where(lane == i1, -jnp.inf, logits)
    m2 = jnp.max(rest, axis=-1, keepdims=True)
    i2 = jnp.min(jnp.where(rest == m2, lane, LANES), axis=-1, keepdims=True)
    e2 = jnp.exp(m2 - m1)
    w1 = 1.0 / (1.0 + e2)
    w2 = e2 / (1.0 + e2)
    gate_ref[0] = (jnp.where(lane == i1, w1, 0.0) + jnp.where(lane == i2, w2, 0.0))[:, :N_EXPERTS]
    sel_ref[0] = ((lane == i1) | (lane == i2)).astype(jnp.int32)[:, :N_EXPERTS]


def _ffn_in_route(h, modsel, g, n_ctx, row0, router, tm=256):
    b, t, d = h.shape
    nt = (t - row0) // tm
    off = row0 // tm
    in_specs = [pl.BlockSpec((1, tm, d), lambda bi, ti: (bi, ti + off, 0)),
                pl.BlockSpec((None, 2, 6, d), lambda bi, ti: (bi, 0, 0, 0)),
                pl.BlockSpec((1, d), lambda bi, ti: (0, 0))]
    f_spec = pl.BlockSpec((1, tm, d), lambda bi, ti: (bi, ti, 0))
    f_shape = jax.ShapeDtypeStruct((b, t - row0, d), F32)
    assert router.shape == (d, N_EXPERTS)
    e_spec = pl.BlockSpec((1, tm, N_EXPERTS), lambda bi, ti: (bi, ti, 0))
    kern = functools.partial(_route_kernel, tm=tm, n_ctx=n_ctx, row0=row0)
    return pl.pallas_call(
        kern, grid=(b, nt),
        in_specs=in_specs + [pl.BlockSpec((N_EXPERTS, d), lambda bi, ti: (0, 0))],
        out_specs=[f_spec, e_spec, e_spec],
        out_shape=[f_shape, jax.ShapeDtypeStruct((b, t - row0, N_EXPERTS), F32),
                   jax.ShapeDtypeStruct((b, t - row0, N_EXPERTS), jnp.int32)],
        compiler_params=_params(("arbitrary", "arbitrary")), name="ffn_in_route",
    )(h, modsel, g.reshape(1, d), router.T)


FFN_TM = 1152
FFN_SUB = 192
FFN_VMEM_LIMIT = 60 * 1024 * 1024


def _swiglu_slice(xs, w1, w3, w2):
    h1 = jnp.dot(xs, w1.astype(BF16), preferred_element_type=F32)
    h3 = jnp.dot(xs, w3.astype(BF16), preferred_element_type=F32)
    act = (h1 * _sigmoid(h1) * h3).astype(BF16)
    return jnp.dot(act, w2.astype(BF16), preferred_element_type=F32)


def _ffn_kernel(te_ref, tr_ref, x_ref, w1_ref, w3_ref, w2_ref, o_ref, xs_ref):
    i = pl.program_id(0)
    j = pl.program_id(1)
    n_blocks = (tr_ref[i] + FFN_SUB - 1) // FFN_SUB

    @pl.when(j == 0)
    def _():
        o_ref[...] = jnp.zeros_like(o_ref)

    @pl.when((j == 0) & (n_blocks > 0))
    def _():
        xs_ref[...] = x_ref[...].astype(BF16)

    for nb in range(1, FFN_TM // FFN_SUB + 1):
        @pl.when(n_blocks == nb)
        def _(nb=nb):
            m = nb * FFN_SUB
            o_ref[0:m, :] += _swiglu_slice(xs_ref[0:m, :], w1_ref[0], w3_ref[0], w2_ref[0])


def _ffn(x_rows, w1, w3, w2, tile_expert, tile_rows, tf=256):
    nr, d = x_rows.shape
    n_exp, _, f = w1.shape
    nj = f // tf
    n_tiles = nr // FFN_TM

    def xrow(i, j, te, tr):
        return (jnp.where(tr[i] > 0, i, te[n_tiles]), 0)

    def wcol(i, j, te, tr):
        return (te[i], 0, jnp.where(tr[i] > 0, j, nj - 1))

    def wrow(i, j, te, tr):
        return (te[i], jnp.where(tr[i] > 0, j, nj - 1), 0)

    return pl.pallas_call(
        _ffn_kernel,
        grid_spec=pltpu.PrefetchScalarGridSpec(
            num_scalar_prefetch=2,
            grid=(n_tiles, nj),
            in_specs=[pl.BlockSpec((FFN_TM, d), xrow),
                      pl.BlockSpec((1, d, tf), wcol),
                      pl.BlockSpec((1, d, tf), wcol),
                      pl.BlockSpec((1, tf, d), wrow)],
            out_specs=pl.BlockSpec((FFN_TM, d), lambda i, j, te, tr: (i, 0)),
            scratch_shapes=[pltpu.VMEM((FFN_TM, d), BF16)]),
        out_shape=jax.ShapeDtypeStruct((nr, d), F32),
        compiler_params=_params(("arbitrary", "arbitrary"), FFN_VMEM_LIMIT),
        name="ffn",
    )(tile_expert, tile_rows, x_rows, w1, w3, w2)


def _ffn_dense_kernel(h_ref, m_ref, g_ref, w1_ref, w3_ref, w2_ref, o_ref, xs_ref, *, tm, n_ctx):
    ti = pl.program_id(1)
    j = pl.program_id(2)

    @pl.when(j == 0)
    def _():
        def store(rs, y):
            xs_ref[rs, :] = y.astype(BF16)
        _norm_modulate_rows(h_ref, g_ref, m_ref, 3, ti * tm, tm, n_ctx, store)
        o_ref[...] = jnp.zeros_like(o_ref)

    o_ref[0] += _swiglu_slice(xs_ref[...], w1_ref[...], w3_ref[...], w2_ref[...])

    @pl.when(j == pl.num_programs(2) - 1)
    def _():
        gate = _mod_row(m_ref, 5, _row_is_ctx(ti, tm, n_ctx))
        o_ref[0] = h_ref[0] + gate * o_ref[0]


def _ffn_dense(h, modsel, g, w1, w3, w2, layer, n_ctx, tf=256):
    b, t, d = h.shape
    f = w1.shape[2]
    tm = t // 2
    assert t % tm == 0 and f % tf == 0 and tm % NORM_ROWS == 0 and n_ctx % NORM_ROWS == 0
    kern = functools.partial(_ffn_dense_kernel, tm=tm, n_ctx=n_ctx)
    return pl.pallas_call(
        kern, grid=(b, t // tm, f // tf),
        in_specs=[pl.BlockSpec((1, tm, d), lambda bi, ti, j: (bi, ti, 0)),
                  pl.BlockSpec((None, 2, 6, d), lambda bi, ti, j: (bi, 0, 0, 0)),
                  pl.BlockSpec((1, d), lambda bi, ti, j: (0, 0)),
                  pl.BlockSpec((None, d, tf), lambda bi, ti, j: (layer, 0, j)),
                  pl.BlockSpec((None, d, tf), lambda bi, ti, j: (layer, 0, j)),
                  pl.BlockSpec((None, tf, d), lambda bi, ti, j: (layer, j, 0))],
        out_specs=pl.BlockSpec((1, tm, d), lambda bi, ti, j: (bi, ti, 0)),
        out_shape=jax.ShapeDtypeStruct((b, t, d), F32),
        scratch_shapes=[pltpu.VMEM((tm, d), BF16)],
        compiler_params=_params(("arbitrary", "arbitrary", "arbitrary"), FFN_VMEM_LIMIT), name="ffn_dense",
    )(h, modsel, g.reshape(1, d), w1, w3, w2)


def _moe_final_kernel(h_ref, y0_ref, y1_ref, w0_ref, w1_ref, m_ref, g_ref, o_ref):
    y = w0_ref[0] * y0_ref[0] + w1_ref[0] * y1_ref[0]
    hn = h_ref[0] + m_ref[1, 5:6, :] * y
    ms = jnp.mean(hn * hn, axis=-1, keepdims=True)
    o_ref[0] = hn * lax.rsqrt(ms + EPS) * g_ref[...]


def _moe_final(h, y01, w0, w1, modsel, g, row0, tm=256):
    b, t, d = h.shape
    n_lat = t - row0
    off = row0 // tm
    spec = pl.BlockSpec((1, tm, d), lambda bi, ti: (bi, ti, 0))
    wspec = pl.BlockSpec((1, tm, 1), lambda bi, ti: (bi, ti, 0))
    yspec = lambda slot: pl.BlockSpec((None, 1, tm, d), lambda bi, ti: (slot, bi, ti, 0))
    return pl.pallas_call(
        _moe_final_kernel, grid=(b, n_lat // tm),
        in_specs=[pl.BlockSpec((1, tm, d), lambda bi, ti: (bi, ti + off, 0)), yspec(0), yspec(1), wspec, wspec,
                  pl.BlockSpec((None, 2, 6, d), lambda bi, ti: (bi, 0, 0, 0)),
                  pl.BlockSpec((1, d), lambda bi, ti: (0, 0))],
        out_specs=spec, out_shape=jax.ShapeDtypeStruct((b, n_lat, d), F32),
        compiler_params=_params(("arbitrary", "arbitrary")), name="moe_final",
    )(h, y01, y01, w0, w1, modsel, g.reshape(1, d))


def _routing_tables(sel, gates, n_tiles):
    n, n_exp = sel.shape
    seli = sel.astype(jnp.int32)
    counts = jnp.sum(seli, axis=0)
    rank = jnp.cumsum(seli, axis=0) - seli
    tiles_e = (counts + FFN_TM - 1) // FFN_TM
    tile_end = jnp.cumsum(tiles_e)
    tile_start = tile_end - tiles_e
    pos = tile_start[None, :] * FFN_TM + rank
    n_rows = n_tiles * FFN_TM
    expert = jnp.arange(n_exp, dtype=jnp.int32)[None, :]
    e_lo = jnp.min(jnp.where(sel, expert, n_exp), axis=1, keepdims=True)
    e_hi = jnp.max(jnp.where(sel, expert, -1), axis=1, keepdims=True)
    pick = lambda a, e: jnp.sum(jnp.where(expert == e, a, 0), axis=1)
    pos0, pos1 = pick(pos, e_lo), pick(pos, e_hi)
    w0, w1 = pick(gates, e_lo), pick(gates, e_hi)
    tok = jnp.arange(n, dtype=jnp.int32)
    src = (jnp.arange(n_rows, dtype=jnp.int32) % n).at[jnp.concatenate([pos0, pos1])].set(
        jnp.concatenate([tok, tok]), mode="promise_in_bounds", unique_indices=True)

    tile = jnp.arange(n_tiles, dtype=jnp.int32)
    te = jnp.minimum(jnp.sum((tile_end[None, :] <= tile[:, None]).astype(jnp.int32), axis=1), n_exp - 1)
    tr = jnp.clip(counts[te] - (tile - tile_start[te]) * FFN_TM, 0, FFN_TM)
    tr = jnp.where(tile < tile_end[-1], tr, 0)
    te = jnp.concatenate([te, tile_end[-1:] - 1])
    return src, pos0, pos1, w0, w1, te.astype(jnp.int32), tr.astype(jnp.int32)


def kernel(x, c, ctx, c_ctx, w_mod, b_mod, norm_mix, norm_ffn, w_in, w_out, pool_w, pool_scale, diff_lam,
           diff_subln, sgu_ln_g, sgu_ln_b, sgu_w, sgu_b, win_sink, ffn_w1, ffn_w3, ffn_w2, moe_router,
           moe_w1, moe_w3, moe_w2, norm_final):
    b, n_lat, d = x.shape
    n_ctx = ctx.shape[1]
    t = n_ctx + n_lat
    depth = w_mod.shape[0]
    segments = ((0, n_ctx), (n_ctx, n_lat))

    c_rows = jnp.zeros((8, d), F32).at[:b].set(c).at[b].set(c_ctx)
    mod = _adaln(c_rows, w_mod, b_mod).reshape(depth, 8, 6, d)
    cos_t, sin_t = _rope_tables(n_ctx, n_lat)
    h = jnp.concatenate([ctx, x], axis=1)

    out = None
    for l in range(depth):
        last = l == depth - 1
        lat_m = mod[l, :b]
        ctx_m = jnp.broadcast_to(mod[l, b:b + 1], lat_m.shape)
        modsel = jnp.stack([ctx_m, lat_m], axis=1)

        p_all = _inproj(h, modsel, norm_mix[l], w_in, l, n_ctx)

        lam_init = 0.8 - 0.6 * math.exp(-0.3 * l)
        lq1, lk1, lq2, lk2 = diff_lam[l].astype(F32)
        lam = jnp.exp(jnp.sum(lq1 * lk1)) - jnp.exp(jnp.sum(lq2 * lk2)) + lam_init

        mixes = (_pool(p_all, pool_w[l], pool_scale[l], segments),
                 _diff(p_all, lam, diff_subln[l], cos_t, sin_t, n_ctx, lam_init),
                 _sgu(p_all, sgu_ln_g[l], sgu_ln_b[l], sgu_w[l], sgu_b[l]),
                 _win(p_all, win_sink[l], cos_t, sin_t, n_ctx))
        h = _outproj(mixes, w_out, l, h, modsel, n_ctx)

        i = l // 2
        row0 = n_ctx if last else 0
        rows = b * (t - row0)
        if l % 2 == 0:
            assert not last
            h = _ffn_dense(h, modsel, norm_ffn[l], ffn_w1, ffn_w3, ffn_w2, i, n_ctx)
        else:
            assert last
            f, gates, sel = _ffn_in_route(h, modsel, norm_ffn[l], n_ctx, row0, moe_router[i])
            f = f.reshape(rows, d)
            n_exp = moe_w1.shape[1]
            n_tiles = (2 * rows) // FFN_TM + n_exp
            src, pos0, pos1, w0, w1, te, tr = _routing_tables(
                sel.reshape(rows, n_exp) > 0, gates.reshape(rows, n_exp), n_tiles)
            rows_of = lambda a, idx: a.at[idx].get(mode="promise_in_bounds")
            ys = _ffn(rows_of(f, src), moe_w1[i], moe_w3[i], moe_w2[i], te, tr)
            y01 = rows_of(ys, jnp.concatenate([pos0, pos1])).reshape(2, b, t - row0, d)
            out = _moe_final(h, y01, w0.reshape(b, t - row0, 1), w1.reshape(b, t - row0, 1),
                             modsel, norm_final, row0)
    return out
```
